```python
import math
import jax
import jax.numpy as jnp
from jax import lax
import numpy as np


D_MODEL = 1024
BATCH = 4
SEQ = 8192
DEPTH = 2

HEAD_DIM = 64
EPS = 1e-6
WIN_Q_HEADS = 6
WIN_KV_HEADS = 2
WIN_GROUP = WIN_Q_HEADS // WIN_KV_HEADS
WINDOW = 128
WIN_BLOCK = WINDOW
DIFF_HEADS = 4
DIFF_QK_DIM = HEAD_DIM // 2
DIFF_V_DIM = HEAD_DIM
Q_BLOCK = 128
DN_HEADS = 6
DN_DK = HEAD_DIM
DN_DV = HEAD_DIM
DN_CHUNK = 64
CONV_W = 5
WIN_WIDTH = WIN_Q_HEADS * HEAD_DIM
DIFF_WIDTH = DIFF_HEADS * DIFF_V_DIM
DN_WIDTH = DN_HEADS * DN_DV
MIX_WIDTH = WIN_WIDTH + DIFF_WIDTH + DN_WIDTH
IN_SPLITS = (
    WIN_Q_HEADS * HEAD_DIM,
    WIN_KV_HEADS * HEAD_DIM,
    WIN_KV_HEADS * HEAD_DIM,
    DIFF_HEADS * 2 * DIFF_QK_DIM,
    DIFF_HEADS * 2 * DIFF_QK_DIM,
    DIFF_HEADS * DIFF_V_DIM,
    DN_HEADS * (2 * DN_DK + DN_DV),
    DN_HEADS * DN_DV,
    2 * DN_HEADS,
    2 * DN_HEADS,
)
MIX_IN = sum(IN_SPLITS)
D_FF = 2752

kernel_name = 'hybrid_parallel_head_encoder'


def rms_norm(x, g):
    xf = x.astype(jnp.float32)
    y = xf * lax.rsqrt(jnp.mean(xf * xf, axis=-1, keepdims=True) + EPS)
    return (y * g.astype(jnp.float32)).astype(x.dtype)


def l2_norm(x):
    xf = x.astype(jnp.float32)
    return xf * lax.rsqrt(jnp.sum(xf * xf, axis=-1, keepdims=True) + EPS)


def swiglu(h, w_in, w_out):
    gate, up = jnp.split(h @ w_in, 2, axis=-1)
    return (jax.nn.silu(gate) * up) @ w_out


def alibi_slopes(n):
    return 2.0 ** (-8.0 * jnp.arange(1, n + 1, dtype=jnp.float32) / n)


def window_attention(q, k, v, sink, slopes):
    B, S, KH, G, d = q.shape
    nb = S // WIN_BLOCK
    kw_len = 3 * WIN_BLOCK

    def band(t):
        tp = jnp.pad(t, ((0, 0), (WIN_BLOCK, WIN_BLOCK), (0, 0), (0, 0)))
        tb = tp.reshape(B, nb + 2, WIN_BLOCK, KH, d)
        return jnp.concatenate([tb[:, :-2], tb[:, 1:-1], tb[:, 2:]], axis=2)

    kw, vw = band(k), band(v)
    qb = q.reshape(B, nb, WIN_BLOCK, KH, G, d)
    rel = jnp.arange(kw_len)[None, :] - WIN_BLOCK - jnp.arange(WIN_BLOCK)[:, None]
    kpos = jnp.arange(nb)[:, None] * WIN_BLOCK - WIN_BLOCK + jnp.arange(kw_len)[None, :]
    valid = (jnp.abs(rel) <= WINDOW)[None] & ((kpos >= 0) & (kpos < S))[:, None, :]
    dist = jnp.abs(rel).astype(jnp.float32)
    s = jnp.einsum('bnqhgd,bnkhd->bnhgqk', qb, kw).astype(jnp.float32) * (d ** -0.5)
    s = s - slopes[:, :, None, None] * dist
    s = jnp.where(valid[None, :, None, None], s, -1e30)
    sink_b = sink.astype(jnp.float32)[:, :, None, None]
    m = jnp.maximum(jnp.max(s, axis=-1, keepdims=True), sink_b)
    e = jnp.exp(s - m)
    p = e / (jnp.sum(e, axis=-1, keepdims=True) + jnp.exp(sink_b - m))
    o = jnp.einsum('bnhgqk,bnkhd->bnqhgd', p.astype(v.dtype), vw)
    return o.reshape(B, S, KH * G * d)


def diff_attention(q, k, v, lam, slopes):
    B, S, H, _, dq = q.shape
    dv = v.shape[-1]
    nb = S // Q_BLOCK
    qb = jnp.moveaxis(q.reshape(B, nb, Q_BLOCK, H, 2, dq), 1, 0)
    kpos = jnp.arange(S)

    def one_block(args):
        q_blk, i = args
        qpos = i * Q_BLOCK + jnp.arange(Q_BLOCK)
        dist = jnp.abs(qpos[:, None] - kpos[None, :]).astype(jnp.float32)
        s = jnp.einsum('bqhmd,bkhmd->bhmqk', q_blk, k).astype(jnp.float32) * (dq ** -0.5)
        s = s - slopes[:, None, None, None] * dist
        p = jax.nn.softmax(s, axis=-1)
        pd = p[:, :, 0] - lam * p[:, :, 1]
        return jnp.einsum('bhqk,bkhd->bqhd', pd.astype(v.dtype), v)

    o = lax.map(one_block, (qb, jnp.arange(nb)))
    return jnp.moveaxis(o, 0, 1).reshape(B, S, H, dv)


def short_conv(x, w):
    return lax.conv_general_dilated(
        x, w[:, None, :].astype(x.dtype), window_strides=(1,),
        padding=[(CONV_W // 2, CONV_W // 2)],
        dimension_numbers=('NWC', 'WIO', 'NWC'),
        feature_group_count=x.shape[-1])


def gated_delta_chunked(q, k, v, beta, g):
    B, S, H, dk = q.shape
    dv = v.shape[-1]
    C = DN_CHUNK
    n = S // C
    f32 = jnp.float32
    ch = lambda t: t.astype(f32).reshape(B, n, C, H, t.shape[-1]).transpose(0, 3, 1, 2, 4)
    chs = lambda t: t.astype(f32).reshape(B, n, C, H).transpose(0, 3, 1, 2)
    qc, kc, vc = ch(q), ch(k), ch(v)
    bc, gc = chs(beta), jnp.cumsum(chs(g), axis=-1)
    tril = jnp.tri(C, dtype=bool)
    tril_strict = jnp.tri(C, k=-1, dtype=bool)
    gdiff = gc[..., :, None] - gc[..., None, :]
    decay = jnp.where(tril, jnp.exp(jnp.where(tril, gdiff, 0.0)), 0.0)
    kb = kc * bc[..., None]
    vb = vc * bc[..., None]
    m_low = jnp.where(tril_strict, jnp.einsum('bhncd,bhnsd->bhncs', kb, kc) * decay, 0.0)
    a_mat = jnp.eye(C, dtype=f32) + m_low
    u = lax.linalg.triangular_solve(a_mat, vb, left_side=True, lower=True, unit_diagonal=True)
    w = lax.linalg.triangular_solve(a_mat, kb * jnp.exp(gc)[..., None], left_side=True,
                                    lower=True, unit_diagonal=True)
    qk = jnp.where(tril, jnp.einsum('bhncd,bhnsd->bhncs', qc, kc) * decay, 0.0)
    q_g = qc * jnp.exp(gc)[..., None]
    k_g = kc * jnp.exp(gc[..., -1:] - gc)[..., None]
    g_last = jnp.exp(gc[..., -1])

    def step(state, inp):
        u_i, w_i, qk_i, qg_i, kg_i, gl_i = inp
        v_new = u_i - jnp.einsum('bhck,bhkv->bhcv', w_i, state)
        o_i = jnp.einsum('bhck,bhkv->bhcv', qg_i, state) + jnp.einsum('bhcs,bhsv->bhcv', qk_i, v_new)
        state = state * gl_i[..., None, None] + jnp.einsum('bhck,bhcv->bhkv', kg_i, v_new)
        return state, o_i

    mv = lambda t: jnp.moveaxis(t, 2, 0)
    state0 = jnp.zeros((B, H, dk, dv), f32)
    _, o = lax.scan(step, state0, (mv(u), mv(w), mv(qk), mv(q_g), mv(k_g), mv(g_last)))
    o = o.transpose(1, 0, 3, 2, 4).reshape(B, S, H, dv)
    return o.astype(v.dtype)


def hybrid_mixer(h, w_in, conv_w, sink, diff_lam, diff_g, a_log, dt_bias, dn_g, w_out, lam_init):
    B, S, _ = h.shape
    f32 = jnp.float32
    cuts = [int(c) for c in np.cumsum(IN_SPLITS)[:-1]]
    (wq, wk, wv, dq, dk, dv, dn_qkv, dn_z, dn_beta, dn_a) = jnp.split(h @ w_in, cuts, axis=-1)

    o_win = window_attention(
        wq.reshape(B, S, WIN_KV_HEADS, WIN_GROUP, HEAD_DIM),
        wk.reshape(B, S, WIN_KV_HEADS, HEAD_DIM),
        wv.reshape(B, S, WIN_KV_HEADS, HEAD_DIM),
        sink.reshape(WIN_KV_HEADS, WIN_GROUP),
        alibi_slopes(WIN_Q_HEADS).reshape(WIN_KV_HEADS, WIN_GROUP))

    lq1, lk1, lq2, lk2 = diff_lam.astype(f32)
    lam = jnp.exp(jnp.sum(lq1 * lk1)) - jnp.exp(jnp.sum(lq2 * lk2)) + lam_init
    o_d = diff_attention(
        dq.reshape(B, S, DIFF_HEADS, 2, DIFF_QK_DIM),
        dk.reshape(B, S, DIFF_HEADS, 2, DIFF_QK_DIM),
        dv.reshape(B, S, DIFF_HEADS, DIFF_V_DIM),
        lam, alibi_slopes(DIFF_HEADS))
    o_diff = (rms_norm(o_d, diff_g) * (1.0 - lam_init)).reshape(B, S, DIFF_WIDTH)

    qkv = jax.nn.silu(short_conv(dn_qkv, conv_w))
    q, k, v = jnp.split(qkv, [DN_HEADS * DN_DK, 2 * DN_HEADS * DN_DK], axis=-1)
    q = (l2_norm(q.reshape(B, S, DN_HEADS, DN_DK)) * (DN_DK ** -0.5)).astype(h.dtype)
    k = l2_norm(k.reshape(B, S, DN_HEADS, DN_DK)).astype(h.dtype)
    v = v.reshape(B, S, DN_HEADS, DN_DV)
    beta = jax.nn.sigmoid(dn_beta.astype(f32)).reshape(B, S, 2, DN_HEADS)
    g = -jnp.exp(a_log.astype(f32)) * jax.nn.softplus(
        dn_a.astype(f32).reshape(B, S, 2, DN_HEADS) + dt_bias.astype(f32))
    o_fwd = gated_delta_chunked(q, k, v, beta[:, :, 0], g[:, :, 0])
    flip = lambda t: jnp.flip(t, axis=1)
    o_bwd = flip(gated_delta_chunked(flip(q), flip(k), flip(v), flip(beta[:, :, 1]), flip(g[:, :, 1])))
    o_c = o_fwd + o_bwd
    o_dn = (rms_norm(o_c, dn_g) * jax.nn.silu(dn_z.reshape(B, S, DN_HEADS, DN_DV))).reshape(B, S, DN_WIDTH)

    o = jnp.concatenate([o_win, o_diff.astype(h.dtype), o_dn.astype(h.dtype)], axis=-1)
    return o @ w_out


def setup_inputs(seed: int = 0) -> dict:
    key = jax.random.key(seed)
    ks = jax.random.split(key, 20)
    f32 = jnp.float32

    def dense(k, shape, fan_in):
        return jax.random.normal(k, shape, f32) * fan_in ** -0.5

    def gain(k, shape):
        return 1.0 + 0.02 * jax.random.normal(k, shape, f32)

    x = jax.random.normal(ks[0], (BATCH, SEQ, D_MODEL), f32)
    ln_ffn1 = gain(ks[1], (DEPTH, D_MODEL))
    ffn1_w_in = dense(ks[2], (DEPTH, D_MODEL, 2 * D_FF), D_MODEL)
    ffn1_w_out = dense(ks[3], (DEPTH, D_FF, D_MODEL), D_FF)
    ln_mix = gain(ks[4], (DEPTH, D_MODEL))
    w_mix_in = dense(ks[5], (DEPTH, D_MODEL, MIX_IN), D_MODEL)
    conv_w = dense(ks[6], (DEPTH, CONV_W, DN_HEADS * (2 * DN_DK + DN_DV)), CONV_W)
    sink_logits = 0.5 * jax.random.normal(ks[7], (DEPTH, WIN_Q_HEADS), f32)
    diff_lambda = 0.1 * jax.random.normal(ks[8], (DEPTH, 4, DIFF_QK_DIM), f32)
    diff_norm_g = gain(ks[9], (DEPTH, DIFF_V_DIM))
    dn_A_log = jnp.log(jax.random.uniform(ks[10], (DEPTH, 2, DN_HEADS), f32, 1.0, 16.0))
    dt = jnp.exp(jax.random.uniform(ks[11], (DEPTH, 2, DN_HEADS), f32, math.log(1e-3), math.log(1e-1)))
    dn_dt_bias = dt + jnp.log(-jnp.expm1(-dt))
    dn_norm_g = gain(ks[12], (DEPTH, DN_DV))
    w_mix_out = dense(ks[13], (DEPTH, MIX_WIDTH, D_MODEL), MIX_WIDTH)
    ln_ffn2 = gain(ks[14], (DEPTH, D_MODEL))
    ffn2_w_in = dense(ks[15], (DEPTH, D_MODEL, 2 * D_FF), D_MODEL)
    ffn2_w_out = dense(ks[16], (DEPTH, D_FF, D_MODEL), D_FF)
    ln_final = gain(ks[17], (D_MODEL,))
    return {'x': x, 'ln_ffn1': ln_ffn1, 'ffn1_w_in': ffn1_w_in, 'ffn1_w_out': ffn1_w_out,
            'ln_mix': ln_mix, 'w_mix_in': w_mix_in, 'conv_w': conv_w, 'sink_logits': sink_logits,
            'diff_lambda': diff_lambda, 'diff_norm_g': diff_norm_g, 'dn_A_log': dn_A_log,
            'dn_dt_bias': dn_dt_bias, 'dn_norm_g': dn_norm_g, 'w_mix_out': w_mix_out,
            'ln_ffn2': ln_ffn2, 'ffn2_w_in': ffn2_w_in, 'ffn2_w_out': ffn2_w_out, 'ln_final': ln_final}


def reference(x, ln_ffn1, ffn1_w_in, ffn1_w_out, ln_mix, w_mix_in, conv_w, sink_logits,
              diff_lambda, diff_norm_g, dn_A_log, dn_dt_bias, dn_norm_g, w_mix_out,
              ln_ffn2, ffn2_w_in, ffn2_w_out, ln_final):
    for l in range(DEPTH):
        lam_init = 0.8 - 0.6 * math.exp(-0.3 * l)
        x = x + 0.5 * swiglu(rms_norm(x, ln_ffn1[l]), ffn1_w_in[l], ffn1_w_out[l])
        h = rms_norm(x, ln_mix[l])
        x = x + hybrid_mixer(h, w_mix_in[l], conv_w[l], sink_logits[l], diff_lambda[l],
                             diff_norm_g[l], dn_A_log[l], dn_dt_bias[l], dn_norm_g[l],
                             w_mix_out[l], lam_init)
        x = x + 0.5 * swiglu(rms_norm(x, ln_ffn2[l]), ffn2_w_in[l], ffn2_w_out[l])
    return rms_norm(x, ln_final)
```

```python
import functools
import math

import jax
import jax.numpy as jnp
from jax import lax
from jax.experimental import pallas as pl
from jax.experimental.pallas import tpu as pltpu

F32 = jnp.float32
BF16 = jnp.bfloat16

D_MODEL = 1024
HEAD_DIM = 64
EPS = 1e-6
WIN_Q_HEADS = 6
WIN_KV_HEADS = 2
WIN_GROUP = WIN_Q_HEADS // WIN_KV_HEADS
WINDOW = 128
DIFF_HEADS = 4
DIFF_QK_DIM = HEAD_DIM // 2
DN_HEADS = 6
DN_CHUNK = 64
CONV_W = 5
D_FF = 2752
WIN_WIDTH = WIN_Q_HEADS * HEAD_DIM
WIN_KV_WIDTH = WIN_KV_HEADS * HEAD_DIM
DIFF_WIDTH = DIFF_HEADS * HEAD_DIM
DN_WIDTH = DN_HEADS * HEAD_DIM
DN_QKV = 3 * DN_WIDTH
MIX_WIDTH = WIN_WIDTH + DIFF_WIDTH + DN_WIDTH
OFF_WQ = 0
OFF_WK = OFF_WQ + WIN_WIDTH
OFF_WV = OFF_WK + WIN_KV_WIDTH
OFF_DQ = OFF_WV + WIN_KV_WIDTH
OFF_DK = OFF_DQ + DIFF_WIDTH
OFF_DV = OFF_DK + DIFF_WIDTH
OFF_DN = OFF_DV + DIFF_WIDTH
OFF_Z = OFF_DN + DN_QKV
OFF_BA = OFF_Z + DN_WIDTH
MIX_IN = OFF_BA + 4 * DN_HEADS

LANES = 128
FF_CHUNK = 256
D_FF_PAD = -(-D_FF // FF_CHUNK) * FF_CHUNK
VMEM_LIMIT = 56 * 1024 * 1024


def _rms(x, g):
    return x * lax.rsqrt(jnp.mean(x * x, axis=-1, keepdims=True) + EPS) * g


def _silu(x):
    return x * (1.0 / (1.0 + jnp.exp(-x)))


def _dot(a, b):
    return jnp.dot(a, b, preferred_element_type=F32)


def _dot_nt(a, b):
    return lax.dot_general(a, b, (((1,), (1,)), ((), ())), preferred_element_type=F32)


def _resident(shape):
    nd = len(shape)
    return pl.BlockSpec(shape, lambda *_: (0,) * nd, pipeline_mode=pl.Buffered(1))


def _ffn_body(x_ref, g_ref, wgu_ref, wo_ref, *rest, n_chunks, final):
    if final:
        gf_ref, o_ref, acc_ref = rest
    else:
        o_ref, acc_ref = rest
    x = x_ref[...]
    h = _rms(x, g_ref[...]).astype(BF16)
    for c in range(n_chunks):
        gu = _dot(h, wgu_ref[:, c * 2 * FF_CHUNK:(c + 1) * 2 * FF_CHUNK])
        a = (_silu(gu[:, :FF_CHUNK]) * gu[:, FF_CHUNK:]).astype(BF16)
        part = _dot(a, wo_ref[c * FF_CHUNK:(c + 1) * FF_CHUNK, :])
        if c == 0:
            acc_ref[...] = part
        else:
            acc_ref[...] += part
    y = x + 0.5 * acc_ref[...]
    if final:
        y = _rms(y, gf_ref[...])
    o_ref[...] = y


def _prep_ffn_weights(w_in, w_out):
    n = D_FF_PAD // FF_CHUNK
    pad = D_FF_PAD - D_FF
    wg = jnp.pad(w_in[:, :D_FF], ((0, 0), (0, pad))).reshape(D_MODEL, n, FF_CHUNK)
    wu = jnp.pad(w_in[:, D_FF:], ((0, 0), (0, pad))).reshape(D_MODEL, n, FF_CHUNK)
    wgu = jnp.concatenate([wg, wu], axis=2).reshape(D_MODEL, n * 2 * FF_CHUNK).astype(BF16)
    wo = jnp.pad(w_out, ((0, pad), (0, 0))).astype(BF16)
    return wgu, wo


def _ffn(x2d, g, w_in, w_out, g_final=None, *, tm=512):
    T = x2d.shape[0]
    tm = min(tm, T)
    n_chunks = D_FF_PAD // FF_CHUNK
    wgu, wo = _prep_ffn_weights(w_in, w_out)
    final = g_final is not None
    in_specs = [
        pl.BlockSpec((tm, D_MODEL), lambda i: (i, 0)),
        _resident((1, D_MODEL)),
        _resident(wgu.shape),
        _resident(wo.shape),
    ]
    args = [x2d, g.reshape(1, D_MODEL), wgu, wo]
    if final:
        in_specs.append(_resident((1, D_MODEL)))
        args.append(g_final.reshape(1, D_MODEL))
    return pl.pallas_call(
        functools.partial(_ffn_body, n_chunks=n_chunks, final=final),
        grid=(T // tm,),
        in_specs=in_specs,
        out_specs=pl.BlockSpec((tm, D_MODEL), lambda i: (i, 0)),
        out_shape=jax.ShapeDtypeStruct((T, D_MODEL), F32),
        scratch_shapes=[pltpu.VMEM((tm, D_MODEL), F32)],
        compiler_params=pltpu.CompilerParams(
            dimension_semantics=("parallel",), vmem_limit_bytes=VMEM_LIMIT),
        name="ffn_final" if final else "ffn",
    )(*args)


BA_PAD = LANES


def _inproj_body(x_ref, g_ref, w_ref, wkt_ref, wbat_ref,
                 wq_ref, wk_ref, wv_ref, dq_ref, dkt_ref, dv_ref, dn_ref, z_ref, ba_ref, bat_ref):
    h = _rms(x_ref[0], g_ref[...]).astype(BF16)
    wq_ref[0] = (_dot(h, w_ref[:, OFF_WQ:OFF_WK]) * (HEAD_DIM ** -0.5)).astype(BF16)
    wk_ref[0] = _dot(h, w_ref[:, OFF_WK:OFF_WV]).astype(BF16)
    wv_ref[0] = _dot(h, w_ref[:, OFF_WV:OFF_DQ]).astype(BF16)
    dq_ref[0] = (_dot(h, w_ref[:, OFF_DQ:OFF_DK]) * (DIFF_QK_DIM ** -0.5)).astype(BF16)
    dkt_ref[0] = _dot_nt(wkt_ref[...], h).astype(BF16)
    dv_ref[0] = _dot(h, w_ref[:, OFF_DV:OFF_DN]).astype(BF16)
    dn_ref[0] = _dot(h, w_ref[:, OFF_DN:OFF_Z])
    z_ref[0] = _dot(h, w_ref[:, OFF_Z:OFF_BA])
    ba_ref[0] = _dot(h, w_ref[:, OFF_BA:OFF_BA + BA_PAD])
    bat_ref[0] = _dot_nt(wbat_ref[...], h)


def _inproj(x, g, w_in, *, tm=512):
    B, S, _ = x.shape
    tm = min(tm, S)
    n_ba = 4 * DN_HEADS
    w = jnp.pad(w_in, ((0, 0), (0, OFF_BA + BA_PAD - MIX_IN))).astype(BF16)
    wkt = w_in[:, OFF_DK:OFF_DV].T.astype(BF16)
    wbat = w_in[:, OFF_BA:MIX_IN].T.astype(BF16)
    row = lambda n: pl.BlockSpec((1, tm, n), lambda b, i: (b, i, 0))
    col = lambda n: pl.BlockSpec((1, n, tm), lambda b, i: (b, 0, i))
    out_shapes = [
        jax.ShapeDtypeStruct((B, S, WIN_WIDTH), BF16),
        jax.ShapeDtypeStruct((B, S, WIN_KV_WIDTH), BF16),
        jax.ShapeDtypeStruct((B, S, WIN_KV_WIDTH), BF16),
        jax.ShapeDtypeStruct((B, S, DIFF_WIDTH), BF16),
        jax.ShapeDtypeStruct((B, DIFF_WIDTH, S), BF16),
        jax.ShapeDtypeStruct((B, S, DIFF_WIDTH), BF16),
        jax.ShapeDtypeStruct((B, S, DN_QKV), F32),
        jax.ShapeDtypeStruct((B, S, DN_WIDTH), F32),
        jax.ShapeDtypeStruct((B, S, BA_PAD), F32),
        jax.ShapeDtypeStruct((B, n_ba, S), F32),
    ]
    out_specs = [row(WIN_WIDTH), row(WIN_KV_WIDTH), row(WIN_KV_WIDTH), row(DIFF_WIDTH),
                 col(DIFF_WIDTH), row(DIFF_WIDTH), row(DN_QKV), row(DN_WIDTH), row(BA_PAD), col(n_ba)]
    return pl.pallas_call(
        _inproj_body,
        grid=(B, S // tm),
        in_specs=[row(D_MODEL), _resident((1, D_MODEL)), _resident(w.shape),
                  _resident(wkt.shape), _resident(wbat.shape)],
        out_specs=out_specs,
        out_shape=out_shapes,
        compiler_params=pltpu.CompilerParams(
            dimension_semantics=("parallel", "parallel"), vmem_limit_bytes=VMEM_LIMIT),
        name="inproj",
    )(x, g.reshape(1, D_MODEL), w, wkt, wbat)


def _win_body(q_ref, kp_ref, kc_ref, kn_ref, vp_ref, vc_ref, vn_ref, sink_ref, slope_ref, o_ref, *, n_blocks):
    i = pl.program_id(1)
    W = WINDOW
    q = q_ref[0]
    k = jnp.concatenate([kp_ref[0], kc_ref[0], kn_ref[0]], axis=0)
    v = jnp.concatenate([vp_ref[0], vc_ref[0], vn_ref[0]], axis=0)
    qi = lax.broadcasted_iota(jnp.int32, (W, 3 * W), 0)
    ki = lax.broadcasted_iota(jnp.int32, (W, 3 * W), 1)
    rel = ki - W - qi
    dist = jnp.abs(rel)
    valid = dist <= W
    valid = jnp.logical_and(valid, jnp.logical_or(ki >= W, i > 0))
    valid = jnp.logical_and(valid, jnp.logical_or(ki < 2 * W, i < n_blocks - 1))
    distf = dist.astype(F32)
    outs = []
    for hq in range(WIN_Q_HEADS):
        kh = hq // WIN_GROUP
        qh = q[:, hq * HEAD_DIM:(hq + 1) * HEAD_DIM]
        k_h = k[:, kh * HEAD_DIM:(kh + 1) * HEAD_DIM]
        v_h = v[:, kh * HEAD_DIM:(kh + 1) * HEAD_DIM]
        s = _dot_nt(qh, k_h) - slope_ref[:, hq:hq + 1] * distf
        s = jnp.where(valid, s, -1e30)
        sink = sink_ref[:, hq:hq + 1]
        m = jnp.maximum(jnp.max(s, axis=-1, keepdims=True), sink)
        e = jnp.exp(s - m)
        denom = jnp.sum(e, axis=-1, keepdims=True) + jnp.exp(sink - m)
        p = (e / denom).astype(BF16)
        outs.append(_dot(p, v_h))
    o_ref[0] = jnp.concatenate(outs, axis=-1).astype(BF16)


def _win_attention(q, k, v, sink, slopes):
    B, S, _ = q.shape
    W = WINDOW
    nb = S // W
    qspec = pl.BlockSpec((1, W, WIN_WIDTH), lambda b, i: (b, i, 0))
    prev = pl.BlockSpec((1, W, WIN_KV_WIDTH), lambda b, i: (b, jnp.maximum(i - 1, 0), 0))
    cur = pl.BlockSpec((1, W, WIN_KV_WIDTH), lambda b, i: (b, i, 0))
    nxt = pl.BlockSpec((1, W, WIN_KV_WIDTH), lambda b, i: (b, jnp.minimum(i + 1, nb - 1), 0))
    return pl.pallas_call(
        functools.partial(_win_body, n_blocks=nb),
        grid=(B, nb),
        in_specs=[qspec, prev, cur, nxt, prev, cur, nxt,
                  _resident((1, WIN_Q_HEADS)), _resident((1, WIN_Q_HEADS))],
        out_specs=pl.BlockSpec((1, W, WIN_WIDTH), lambda b, i: (b, i, 0)),
        out_shape=jax.ShapeDtypeStruct((B, S, WIN_WIDTH), BF16),
        compiler_params=pltpu.CompilerParams(dimension_semantics=("parallel", "parallel")),
        name="win_attn",
    )(q, k, k, k, v, v, v, sink.reshape(1, WIN_Q_HEADS).astype(F32), slopes.reshape(1, WIN_Q_HEADS))


def _diff_body(q_ref, kt_ref, v_ref, lam_ref, g_ref, o_ref, qs_ref, m_ref, l_ref, acc_ref,
               *, tq, tk, lam_init, slopes):
    qi = pl.program_id(1)
    ki = pl.program_id(2)
    nk = pl.num_programs(2)
    n_maps = 2 * DIFF_HEADS

    @pl.when(ki == 0)
    def _():
        q = q_ref[0]
        for hm in range(n_maps):
            qs_ref[hm] = q[:, hm * DIFF_QK_DIM:(hm + 1) * DIFF_QK_DIM]
        m_ref[...] = jnp.full(m_ref.shape, -jnp.inf, F32)
        l_ref[...] = jnp.zeros(l_ref.shape, F32)
        acc_ref[...] = jnp.zeros(acc_ref.shape, F32)

    qpos = qi * tq + lax.broadcasted_iota(jnp.int32, (tq, tk), 0)
    kpos = ki * tk + lax.broadcasted_iota(jnp.int32, (tq, tk), 1)
    dist = jnp.abs(qpos - kpos).astype(F32)
    for hm in range(n_maps):
        h = hm // 2
        kt = kt_ref[0, hm * DIFF_QK_DIM:(hm + 1) * DIFF_QK_DIM, :]
        s = _dot(qs_ref[hm], kt) - slopes[h] * dist
        m_old = m_ref[hm]
        m_new = jnp.maximum(m_old, jnp.max(s, axis=-1, keepdims=True))
        alpha = jnp.exp(m_old - m_new)
        e = jnp.exp(s - m_new)
        l_ref[hm] = alpha * l_ref[hm] + jnp.sum(e, axis=-1, keepdims=True)
        v2 = v_ref[0, :, (h // 2) * LANES:(h // 2 + 1) * LANES]
        acc_ref[hm] = alpha * acc_ref[hm] + _dot(e.astype(BF16), v2)
        m_ref[hm] = m_new

    @pl.when(ki == nk - 1)
    def _():
        lp = lam_ref[...]
        lam = (jnp.exp(jnp.sum(lp[0:1] * lp[1:2], axis=-1, keepdims=True))
               - jnp.exp(jnp.sum(lp[2:3] * lp[3:4], axis=-1, keepdims=True)) + lam_init)
        lane = lax.broadcasted_iota(jnp.int32, (tq, LANES), 1)
        left = lane < HEAD_DIM
        ys = []
        for h in range(DIFF_HEADS):
            o1 = acc_ref[2 * h] / l_ref[2 * h]
            o2 = acc_ref[2 * h + 1] / l_ref[2 * h + 1]
            od = o1 - lam * o2
            mine = left if h % 2 == 0 else jnp.logical_not(left)
            ms = jnp.sum(jnp.where(mine, od * od, 0.0), axis=-1, keepdims=True) * (1.0 / HEAD_DIM)
            ys.append(od * lax.rsqrt(ms + EPS) * g_ref[...] * (1.0 - lam_init))
        for p in range(DIFF_HEADS // 2):
            o_ref[0, :, p * LANES:(p + 1) * LANES] = jnp.where(left, ys[2 * p], ys[2 * p + 1]).astype(BF16)


def _diff_attention(q, kt, v, diff_lambda, diff_g, lam_init, *, tq=256, tk=512):
    B, S, _ = q.shape
    tq, tk = min(tq, S), min(tk, S)
    n_maps = 2 * DIFF_HEADS
    slopes = tuple(2.0 ** (-8.0 * (i + 1) / DIFF_HEADS) for i in range(DIFF_HEADS))
    g2 = jnp.concatenate([diff_g, diff_g]).reshape(1, LANES).astype(F32)
    return pl.pallas_call(
        functools.partial(_diff_body, tq=tq, tk=tk, lam_init=lam_init, slopes=slopes),
        grid=(B, S // tq, S // tk),
        in_specs=[pl.BlockSpec((1, tq, DIFF_WIDTH), lambda b, i, j: (b, i, 0)),
                  pl.BlockSpec((1, DIFF_WIDTH, tk), lambda b, i, j: (b, 0, j)),
                  pl.BlockSpec((1, tk, DIFF_WIDTH), lambda b, i, j: (b, j, 0)),
                  _resident((4, DIFF_QK_DIM)), _resident((1, LANES))],
        out_specs=pl.BlockSpec((1, tq, DIFF_WIDTH), lambda b, i, j: (b, i, 0)),
        out_shape=jax.ShapeDtypeStruct((B, S, DIFF_WIDTH), BF16),
        scratch_shapes=[pltpu.VMEM((n_maps, tq, DIFF_QK_DIM), BF16),
                        pltpu.VMEM((n_maps, tq, 1), F32),
                        pltpu.VMEM((n_maps, tq, 1), F32),
                        pltpu.VMEM((n_maps, tq, LANES), F32)],
        compiler_params=pltpu.CompilerParams(
            dimension_semantics=("parallel", "parallel", "arbitrary")),
        name="diff_attn",
    )(q, kt, v, diff_lambda.astype(F32), g2)


def _seg_sum64(x):
    lane = lax.broadcasted_iota(jnp.int32, x.shape, 1)
    for d in (32, 16, 8, 4, 2, 1):
        up = pltpu.roll(x, LANES - d, 1)
        dn = pltpu.roll(x, d, 1)
        x = x + jnp.where((lane & d) == 0, up, dn)
    return x


def _l2n64(x):
    parts = []
    for c in range(x.shape[1] // LANES):
        xc = x[:, c * LANES:(c + 1) * LANES]
        parts.append(xc * lax.rsqrt(_seg_sum64(xc * xc) + EPS))
    return jnp.concatenate(parts, axis=-1)


def _dnprep_body(x_ref, xp_ref, xn_ref, w_ref, ba_ref, bat_ref, ab_ref, abt_ref,
                 q_ref, k_ref, v_ref, bg_ref, gt_ref, *, ts, n_tiles):
    i = pl.program_id(1)
    H = DN_HEADS
    x = x_ref[0]
    xp = jnp.where(i > 0, xp_ref[0], 0.0)
    xn = jnp.where(i < n_tiles - 1, xn_ref[0], 0.0)
    xx = jnp.concatenate([xp, x, xn], axis=0)
    half = CONV_W // 2
    y = jnp.zeros_like(x)
    for j in range(CONV_W):
        y = y + w_ref[j:j + 1, :] * xx[8 - half + j:8 - half + j + ts, :]
    y = _silu(y)
    q_ref[0] = _l2n64(y[:, :DN_WIDTH]) * (HEAD_DIM ** -0.5)
    k_ref[0] = _l2n64(y[:, DN_WIDTH:2 * DN_WIDTH])
    v_ref[0] = y[:, 2 * DN_WIDTH:]

    def gates(raw, a_log, dt_bias):
        beta = 1.0 / (1.0 + jnp.exp(-raw))
        z = raw + dt_bias
        softplus = jnp.maximum(z, 0.0) + jnp.log(1.0 + jnp.exp(-jnp.abs(z)))
        return beta, -jnp.exp(a_log) * softplus

    ba = ba_ref[0]
    beta, g = gates(ba, ab_ref[0:1, :], ab_ref[1:2, :])
    lane = lax.broadcasted_iota(jnp.int32, ba.shape, 1)
    r = lax.broadcasted_iota(jnp.int32, ba.shape, 0) % DN_CHUNK
    bwd = jnp.logical_and(lane >= 3 * H, lane < 4 * H)
    gc = g
    for d in (1, 2, 4, 8, 16, 32):
        f = jnp.where(r >= d, pltpu.roll(gc, d, 0), 0.0)
        b = jnp.where(r < DN_CHUNK - d, pltpu.roll(gc, ts - d, 0), 0.0)
        gc = gc + jnp.where(bwd, b, f)
    bg_ref[0] = jnp.where(lane < 2 * H, beta, gc)
    bat = bat_ref[0]
    _, gt = gates(bat, abt_ref[:, 0:1], abt_ref[:, 1:2])
    row = lax.broadcasted_iota(jnp.int32, bat.shape, 0)
    c = lax.broadcasted_iota(jnp.int32, bat.shape, 1) % DN_CHUNK
    bwd_t = row >= 3 * H
    for d in (1, 2, 4, 8, 16, 32):
        f = jnp.where(c >= d, pltpu.roll(gt, d, 1), 0.0)
        b = jnp.where(c < DN_CHUNK - d, pltpu.roll(gt, ts - d, 1), 0.0)
        gt = gt + jnp.where(bwd_t, b, f)
    gt_ref[0] = gt


def _dn_prep(dn_raw, ba, bat, conv_w, a_log, dt_bias, *, ts=256):
    B, S, C = dn_raw.shape
    ts = min(ts, S)
    nt = S // ts
    n_ba = 4 * DN_HEADS
    hb = ts // 8
    zeros = jnp.zeros((2 * DN_HEADS,), F32)
    al = jnp.concatenate([zeros, a_log.reshape(-1).astype(F32)])
    db = jnp.concatenate([zeros, dt_bias.reshape(-1).astype(F32)])
    ab = jnp.pad(jnp.stack([al, db]), ((0, 0), (0, BA_PAD - n_ba)))
    abt = jnp.stack([al, db], axis=1)
    row = lambda n: pl.BlockSpec((1, ts, n), lambda b, i: (b, i, 0))
    return pl.pallas_call(
        functools.partial(_dnprep_body, ts=ts, n_tiles=nt),
        grid=(B, nt),
        in_specs=[row(C),
                  pl.BlockSpec((1, 8, C), lambda b, i: (b, jnp.maximum(i * hb - 1, 0), 0)),
                  pl.BlockSpec((1, 8, C), lambda b, i: (b, jnp.minimum((i + 1) * hb, S // 8 - 1), 0)),
                  _resident((CONV_W, C)), row(BA_PAD),
                  pl.BlockSpec((1, n_ba, ts), lambda b, i: (b, 0, i)),
                  _resident((2, BA_PAD)), _resident((n_ba, 2))],
        out_specs=[row(DN_WIDTH), row(DN_WIDTH), row(DN_WIDTH), row(BA_PAD),
                   pl.BlockSpec((1, n_ba, ts), lambda b, i: (b, 0, i))],
        out_shape=[jax.ShapeDtypeStruct((B, S, DN_WIDTH), F32)] * 3
        + [jax.ShapeDtypeStruct((B, S, BA_PAD), F32), jax.ShapeDtypeStruct((B, n_ba, S), F32)],
        compiler_params=pltpu.CompilerParams(dimension_semantics=("parallel", "parallel")),
        name="dn_prep",
    )(dn_raw, dn_raw, dn_raw, conv_w.astype(F32), ba, bat, ab, abt)


def _bd_rows(y, left):
    return jnp.concatenate([jnp.where(left, y, 0.0), jnp.where(left, 0.0, y)], axis=0).astype(BF16)


def _dnscan_body(qf_ref, kf_ref, vf_ref, bgf_ref, gtf_ref, qb_ref, kb_ref, vb_ref, bgb_ref, gtb_ref,
                 of_ref, ob_ref, st_ref, vn_ref, *, G):
    C = DN_CHUNK
    H = DN_HEADS
    R = G * C
    P = H // 2

    @pl.when(pl.program_id(1) == 0)
    def _():
        st_ref[...] = jnp.zeros(st_ref.shape, F32)

    vn_ref[...] = jnp.zeros(vn_ref.shape, F32)

    lane1 = lax.broadcasted_iota(jnp.int32, (C, LANES), 1)
    row1 = lax.broadcasted_iota(jnp.int32, (C, LANES), 0)
    left1 = lane1 < C
    s_idx = lane1 % C
    lane2 = lax.broadcasted_iota(jnp.int32, (C, 2 * LANES), 1)
    left2 = (lane2 % LANES) < C
    rr = lax.broadcasted_iota(jnp.int32, (LANES, LANES), 0)
    cc = lax.broadcasted_iota(jnp.int32, (LANES, LANES), 1)
    top = rr < C
    bdmask = (rr < C) == (cc < C)
    lane_r = lax.broadcasted_iota(jnp.int32, (LANES, R), 1)
    top_r = lax.broadcasted_iota(jnp.int32, (LANES, R), 0) < C

    dirs = ((0, qf_ref, kf_ref, vf_ref, bgf_ref, gtf_ref, of_ref),
            (1, qb_ref, kb_ref, vb_ref, bgb_ref, gtb_ref, ob_ref))
    pre = {}
    for d, q_ref, k_ref, v_ref, bg_ref, gt_ref, _ in dirs:
        tri = (row1 >= s_idx) if d == 0 else (row1 <= s_idx)
        strict = (row1 > s_idx) if d == 0 else (row1 < s_idx)
        last = C - 1 if d == 0 else 0
        for p in range(P):
            h0 = 2 * p
            sl = slice(p * LANES, (p + 1) * LANES)
            cb = d * H + h0
            cg = 2 * H + d * H + h0
            kT = k_ref[0, :, sl].T
            for c in range(G):
                rs = slice(c * C, (c + 1) * C)
                qc, kc, vc = q_ref[0, rs, sl], k_ref[0, rs, sl], v_ref[0, rs, sl]
                bexp = jnp.where(left1, bg_ref[0, rs, cb:cb + 1], bg_ref[0, rs, cb + 1:cb + 2])
                gcc = jnp.where(left1, bg_ref[0, rs, cg:cg + 1], bg_ref[0, rs, cg + 1:cg + 2])
                g0 = gt_ref[0, cg:cg + 1, rs]
                g1 = gt_ref[0, cg + 1:cg + 2, rs]
                grow = jnp.concatenate([g0, g1], axis=1)
                glast = gcc[last:last + 1, :]
                eg = jnp.exp(gcc)
                kb_ = kc * bexp
                X = jnp.concatenate([vc * bexp, kb_ * eg], axis=1)
                qg = qc * eg
                Z = _bd_rows(kc, left1)
                dec = jnp.where(tri, jnp.exp(jnp.minimum(gcc - grow, 0.0)), 0.0)
                N = jnp.where(strict, -_dot_nt(kb_.astype(BF16), Z) * dec, 0.0)
                qk = _dot_nt(qc.astype(BF16), Z) * dec
                for j in range(6):
                    Nb = N.astype(BF16)
                    X = X + _dot(Nb, _bd_rows(X, left2))
                    if j < 5:
                        N = _dot(Nb, _bd_rows(N, left1))
                gl0 = glast[:, 0:1]
                gl1 = glast[:, C:C + 1]
                gr0 = gt_ref[0, cg:cg + 1, :]
                gr1 = gt_ref[0, cg + 1:cg + 2, :]
                fac = jnp.exp(jnp.minimum(jnp.where(top_r, gl0 - gr0, gl1 - gr1), 0.0))
                in_chunk = (lane_r // C) == c
                kgT = jnp.where(in_chunk, kT * fac, 0.0).astype(BF16)
                glm = jnp.exp(jnp.where(top, gl0, gl1))
                pre[(d, p, c)] = (X[:, :LANES], X[:, LANES:].astype(BF16), qk.astype(BF16),
                                  qg.astype(BF16), kgT, glm)

    for step in range(G):
        for d, *_refs, o_ref in dirs:
            c = step if d == 0 else G - 1 - step
            rs = slice(c * C, (c + 1) * C)
            for p in range(P):
                u, w, qk, qg, kgT, glm = pre[(d, p, c)]
                sidx = d * P + p
                S_bd = st_ref[sidx]
                t1 = _dot(jnp.concatenate([w, qg], axis=0), S_bd.astype(BF16))
                vn = u - t1[:C]
                o = t1[C:] + _dot(qk, _bd_rows(vn, left1))
                o_ref[0, rs, p * LANES:(p + 1) * LANES] = o
                vn_ref[sidx, rs, :] = vn
                upd = _dot(kgT, vn_ref[sidx].astype(BF16))
                st_ref[sidx] = S_bd * glm + jnp.where(bdmask, upd, 0.0)
                vn_ref[sidx, rs, :] = jnp.zeros((C, LANES), F32)


def _dn_scan(q, k, v, bg, gt, *, G=4):
    B, S, _ = q.shape
    G = min(G, S // DN_CHUNK)
    R = G * DN_CHUNK
    nb = S // R
    n_ba = 4 * DN_HEADS
    P = DN_HEADS // 2
    fw = lambda n: pl.BlockSpec((1, R, n), lambda b, i: (b, i, 0))
    bw = lambda n: pl.BlockSpec((1, R, n), lambda b, i: (b, nb - 1 - i, 0))
    fwt = pl.BlockSpec((1, n_ba, R), lambda b, i: (b, 0, i))
    bwt = pl.BlockSpec((1, n_ba, R), lambda b, i: (b, 0, nb - 1 - i))
    return pl.pallas_call(
        functools.partial(_dnscan_body, G=G),
        grid=(B, nb),
        in_specs=[fw(DN_WIDTH), fw(DN_WIDTH), fw(DN_WIDTH), fw(BA_PAD), fwt,
                  bw(DN_WIDTH), bw(DN_WIDTH), bw(DN_WIDTH), bw(BA_PAD), bwt],
        out_specs=[fw(DN_WIDTH), bw(DN_WIDTH)],
        out_shape=[jax.ShapeDtypeStruct((B, S, DN_WIDTH), F32)] * 2,
        scratch_shapes=[pltpu.VMEM((2 * P, LANES, LANES), F32),
                        pltpu.VMEM((2 * P, R, LANES), F32)],
        compiler_params=pltpu.CompilerParams(dimension_semantics=("parallel", "arbitrary")),
        name="dn_scan",
    )(q, k, v, bg, gt, q, k, v, bg, gt)


def _outproj_body(x_ref, ow_ref, od_ref, of_ref, ob_ref, z_ref, g_ref, w_ref, o_ref, cat_ref):
    oc = of_ref[0] + ob_ref[0]
    parts = []
    for c in range(DN_WIDTH // LANES):
        x = oc[:, c * LANES:(c + 1) * LANES]
        parts.append(x * lax.rsqrt(_seg_sum64(x * x) * (1.0 / HEAD_DIM) + EPS))
    on = jnp.concatenate(parts, axis=-1) * g_ref[...] * _silu(z_ref[0])
    cat_ref[:, :WIN_WIDTH] = ow_ref[0]
    cat_ref[:, WIN_WIDTH:WIN_WIDTH + DIFF_WIDTH] = od_ref[0]
    cat_ref[:, WIN_WIDTH + DIFF_WIDTH:] = on.astype(BF16)
    o_ref[0] = x_ref[0] + _dot(cat_ref[...], w_ref[...])


def _outproj(x, o_win, o_diff, o_f, o_b, z, dn_g, w_out, *, tm=512):
    B, S, _ = x.shape
    tm = min(tm, S)
    row = lambda n: pl.BlockSpec((1, tm, n), lambda b, i: (b, i, 0))
    g = jnp.tile(dn_g.astype(F32), DN_HEADS).reshape(1, DN_WIDTH)
    return pl.pallas_call(
        _outproj_body,
        grid=(B, S // tm),
        in_specs=[row(D_MODEL), row(WIN_WIDTH), row(DIFF_WIDTH), row(DN_WIDTH), row(DN_WIDTH),
                  row(DN_WIDTH), _resident((1, DN_WIDTH)), _resident((MIX_WIDTH, D_MODEL))],
        out_specs=row(D_MODEL),
        out_shape=jax.ShapeDtypeStruct((B, S, D_MODEL), F32),
        scratch_shapes=[pltpu.VMEM((tm, MIX_WIDTH), BF16)],
        compiler_params=pltpu.CompilerParams(
            dimension_semantics=("parallel", "parallel"), vmem_limit_bytes=VMEM_LIMIT),
        name="outproj",
    )(x, o_win, o_diff, o_f, o_b, z, g, w_out.astype(BF16))


def _mixer(x, ln_mix, w_mix_in, conv_w, sink, diff_lam, diff_g, a_log, dt_bias, dn_g, w_mix_out, lam_init):
    wq, wk, wv, dq, dkt, dv, dn_raw, z, ba, bat = _inproj(x, ln_mix, w_mix_in)
    win_slopes = 2.0 ** (-8.0 * jnp.arange(1, WIN_Q_HEADS + 1, dtype=F32) / WIN_Q_HEADS)
    o_win = _win_attention(wq, wk, wv, sink, win_slopes)
    o_diff = _diff_attention(dq, dkt, dv, diff_lam, diff_g, lam_init)
    q, k, v, bg, gt = _dn_prep(dn_raw, ba, bat, conv_w, a_log, dt_bias)
    o_f, o_b = _dn_scan(q, k, v, bg, gt)
    return _outproj(x, o_win, o_diff, o_f, o_b, z, dn_g, w_mix_out)


def kernel(x, ln_ffn1, ffn1_w_in, ffn1_w_out, ln_mix, w_mix_in, conv_w, sink_logits, diff_lambda,
           diff_norm_g, dn_A_log, dn_dt_bias, dn_norm_g, w_mix_out, ln_ffn2, ffn2_w_in, ffn2_w_out,
           ln_final):
    B, S, D = x.shape
    depth = ln_ffn1.shape[0]
    for l in range(depth):
        lam_init = 0.8 - 0.6 * math.exp(-0.3 * l)
        x = _ffn(x.reshape(B * S, D), ln_ffn1[l], ffn1_w_in[l], ffn1_w_out[l]).reshape(B, S, D)
        x = _mixer(x, ln_mix[l], w_mix_in[l], conv_w[l], sink_logits[l], diff_lambda[l], diff_norm_g[l],
                   dn_A_log[l], dn_dt_bias[l], dn_norm_g[l], w_mix_out[l], lam_init)
        g_final = ln_final if l == depth - 1 else None
        x = _ffn(x.reshape(B * S, D), ln_ffn2[l], ffn2_w_in[l], ffn2_w_out[l], g_final).reshape(B, S, D)
    return x
```

```python
import functools
import math

import jax
import jax.numpy as jnp
from jax import lax
from jax.experimental import pallas as pl
from jax.experimental.pallas import tpu as pltpu

F32 = jnp.float32
BF16 = jnp.bfloat16

D_MODEL = 1024
HEAD_DIM = 64
EPS = 1e-6
WIN_Q_HEADS = 6
WIN_KV_HEADS = 2
WIN_GROUP = WIN_Q_HEADS // WIN_KV_HEADS
WINDOW = 128
DIFF_HEADS = 4
DIFF_QK_DIM = HEAD_DIM // 2
DN_HEADS = 6
DN_CHUNK = 64
CONV_W = 5
D_FF = 2752
WIN_WIDTH = WIN_Q_HEADS * HEAD_DIM
WIN_KV_WIDTH = WIN_KV_HEADS * HEAD_DIM
DIFF_WIDTH = DIFF_HEADS * HEAD_DIM
DN_WIDTH = DN_HEADS * HEAD_DIM
DN_QKV = 3 * DN_WIDTH
MIX_WIDTH = WIN_WIDTH + DIFF_WIDTH + DN_WIDTH
OFF_WQ = 0
OFF_WK = OFF_WQ + WIN_WIDTH
OFF_WV = OFF_WK + WIN_KV_WIDTH
OFF_DQ = OFF_WV + WIN_KV_WIDTH
OFF_DK = OFF_DQ + DIFF_WIDTH
OFF_DV = OFF_DK + DIFF_WIDTH
OFF_DN = OFF_DV + DIFF_WIDTH
OFF_Z = OFF_DN + DN_QKV
OFF_BA = OFF_Z + DN_WIDTH
MIX_IN = OFF_BA + 4 * DN_HEADS

LANES = 128
FF_CHUNK = 256
D_FF_PAD = -(-D_FF // FF_CHUNK) * FF_CHUNK
VMEM_LIMIT = 56 * 1024 * 1024
BF16_ROWS = 16
LOG2E = math.log2(math.e)
DIFF_TK = 512
DIFF_KROWS = 48
DIFF_BIAS_ROWS = 3
DIFF_KT_ROWS = 2 * DIFF_HEADS * DIFF_KROWS
DIFF_SLOPES = tuple(2.0 ** (-8.0 * (i + 1) / DIFF_HEADS) for i in range(DIFF_HEADS))


def _rms(x, g):
    return x * lax.rsqrt(jnp.mean(x * x, axis=-1, keepdims=True) + EPS) * g


def _silu(x):
    h = 0.5 * x
    return h + h * jnp.tanh(h)


def _seg_sum64(x):
    R, width = x.shape
    n = width // LANES
    rows = jnp.concatenate([x[:, c * LANES:(c + 1) * LANES] for c in range(n)], axis=0)
    r = lax.broadcasted_iota(jnp.int32, (LANES, LANES), 0) < HEAD_DIM
    c = lax.broadcasted_iota(jnp.int32, (LANES, LANES), 1) < HEAD_DIM
    ones_bd = (r == c).astype(F32).astype(BF16)
    hi = rows.astype(BF16)
    r1 = rows - hi.astype(F32)
    mid = r1.astype(BF16)
    lo = (r1 - mid.astype(F32)).astype(BF16)
    ss = _dot(hi, ones_bd) + _dot(mid, ones_bd) + _dot(lo, ones_bd)
    return jnp.concatenate([ss[c * R:(c + 1) * R] for c in range(n)], axis=1)


def _dot(a, b):
    return jnp.dot(a, b, preferred_element_type=F32)


def _dot_nt(a, b):
    return lax.dot_general(a, b, (((1,), (1,)), ((), ())), preferred_element_type=F32)


def _resident(shape):
    nd = len(shape)
    return pl.BlockSpec(shape, lambda *_: (0,) * nd, pipeline_mode=pl.Buffered(1))


def _ffn_body(x_ref, g_ref, wgu_ref, wo_ref, *rest, n_chunks, final):
    if final:
        gf_ref, o_ref, acc_ref = rest
    else:
        o_ref, acc_ref = rest
    x = x_ref[...]
    h = _rms(x, g_ref[...]).astype(BF16)
    for c in range(n_chunks):
        gu = _dot(h, wgu_ref[:, c * 2 * FF_CHUNK:(c + 1) * 2 * FF_CHUNK])
        a = (_silu(gu[:, :FF_CHUNK]) * gu[:, FF_CHUNK:]).astype(BF16)
        part = _dot(a, wo_ref[c * FF_CHUNK:(c + 1) * FF_CHUNK, :])
        if c == 0:
            acc_ref[...] = part
        else:
            acc_ref[...] += part
    y = x + 0.5 * acc_ref[...]
    if final:
        y = _rms(y, gf_ref[...])
    o_ref[...] = y


def _prep_ffn_weights(w_in, w_out):
    n = D_FF_PAD // FF_CHUNK
    pad = D_FF_PAD - D_FF
    wg = jnp.pad(w_in[:, :D_FF], ((0, 0), (0, pad))).reshape(D_MODEL, n, FF_CHUNK)
    wu = jnp.pad(w_in[:, D_FF:], ((0, 0), (0, pad))).reshape(D_MODEL, n, FF_CHUNK)
    wgu = jnp.concatenate([wg, wu], axis=2).reshape(D_MODEL, n * 2 * FF_CHUNK).astype(BF16)
    wo = jnp.pad(w_out, ((0, pad), (0, 0))).astype(BF16)
    return wgu, wo


def _ffn(x2d, g, w_in, w_out, g_final=None, *, tm=512):
    T = x2d.shape[0]
    tm = min(tm, T)
    n_chunks = D_FF_PAD // FF_CHUNK
    wgu, wo = _prep_ffn_weights(w_in, w_out)
    final = g_final is not None
    in_specs = [
        pl.BlockSpec((tm, D_MODEL), lambda i: (i, 0)),
        _resident((1, D_MODEL)),
        _resident(wgu.shape),
        _resident(wo.shape),
    ]
    args = [x2d, g.reshape(1, D_MODEL), wgu, wo]
    if final:
        in_specs.append(_resident((1, D_MODEL)))
        args.append(g_final.reshape(1, D_MODEL))
    return pl.pallas_call(
        functools.partial(_ffn_body, n_chunks=n_chunks, final=final),
        grid=(T // tm,),
        in_specs=in_specs,
        out_specs=pl.BlockSpec((tm, D_MODEL), lambda i: (i, 0)),
        out_shape=jax.ShapeDtypeStruct((T, D_MODEL), F32),
        scratch_shapes=[pltpu.VMEM((tm, D_MODEL), F32)],
        compiler_params=pltpu.CompilerParams(
            dimension_semantics=("parallel",), vmem_limit_bytes=VMEM_LIMIT),
        name="ffn_final" if final else "ffn",
    )(*args)


BA_PAD = LANES


def _inproj_body(x_ref, g_ref, w_ref, wkt_ref, wbat_ref,
                 wq_ref, wk_ref, wv_ref, dq_ref, dkt_ref, dv_ref, dn_ref, z_ref, ba_ref, bat_ref):
    h = _rms(x_ref[0], g_ref[...]).astype(BF16)
    wq_ref[0] = (_dot(h, w_ref[:, OFF_WQ:OFF_WK]) * (HEAD_DIM ** -0.5)).astype(BF16)
    wk_ref[0] = _dot(h, w_ref[:, OFF_WK:OFF_WV]).astype(BF16)
    wv_ref[0] = _dot(h, w_ref[:, OFF_WV:OFF_DQ]).astype(BF16)
    dq_ref[0] = (_dot(h, w_ref[:, OFF_DQ:OFF_DK]) * (DIFF_QK_DIM ** -0.5 * LOG2E)).astype(BF16)
    kt = _dot_nt(wkt_ref[...], h)
    tm = kt.shape[1]
    extra_rows = DIFF_KROWS - DIFF_QK_DIM
    pos = pl.program_id(1) * tm + lax.broadcasted_iota(jnp.int32, (extra_rows, tm), 1)
    kr = (pos % DIFF_TK).astype(F32)
    rowi = lax.broadcasted_iota(jnp.int32, (extra_rows, tm), 0)
    for hd in range(DIFF_HEADS):
        bias = (DIFF_SLOPES[hd] * LOG2E) * kr
        hi = bias.astype(BF16).astype(F32)
        mid = (bias - hi).astype(BF16).astype(F32)
        lo = bias - hi - mid
        extra = jnp.where(rowi == 0, hi, jnp.where(rowi == 1, mid, jnp.where(rowi == 2, lo, 0.0))).astype(BF16)
        for mp in range(2):
            hm = 2 * hd + mp
            dkt_ref[0, hm * DIFF_KROWS:hm * DIFF_KROWS + DIFF_QK_DIM, :] = (
                kt[hm * DIFF_QK_DIM:(hm + 1) * DIFF_QK_DIM, :].astype(BF16))
            dkt_ref[0, hm * DIFF_KROWS + DIFF_QK_DIM:(hm + 1) * DIFF_KROWS, :] = extra
    dv_ref[0] = _dot(h, w_ref[:, OFF_DV:OFF_DN]).astype(BF16)
    dn_ref[0] = _dot(h, w_ref[:, OFF_DN:OFF_Z])
    z_ref[0] = _dot(h, w_ref[:, OFF_Z:OFF_BA])
    ba_ref[0] = _dot(h, w_ref[:, OFF_BA:OFF_BA + BA_PAD])
    bat_ref[0] = _dot_nt(wbat_ref[...], h)


def _inproj(x, g, w_in, *, tm=512):
    B, S, _ = x.shape
    tm = min(tm, S)
    n_ba = 4 * DN_HEADS
    w = jnp.pad(w_in, ((0, 0), (0, OFF_BA + BA_PAD - MIX_IN))).astype(BF16)
    wkt = w_in[:, OFF_DK:OFF_DV].T.astype(BF16)
    wbat = w_in[:, OFF_BA:MIX_IN].T.astype(BF16)
    row = lambda n: pl.BlockSpec((1, tm, n), lambda b, i: (b, i, 0))
    col = lambda n: pl.BlockSpec((1, n, tm), lambda b, i: (b, 0, i))
    out_shapes = [
        jax.ShapeDtypeStruct((B, S, WIN_WIDTH), BF16),
        jax.ShapeDtypeStruct((B, S, WIN_KV_WIDTH), BF16),
        jax.ShapeDtypeStruct((B, S, WIN_KV_WIDTH), BF16),
        jax.ShapeDtypeStruct((B, S, DIFF_WIDTH), BF16),
        jax.ShapeDtypeStruct((B, DIFF_KT_ROWS, S), BF16),
        jax.ShapeDtypeStruct((B, S, DIFF_WIDTH), BF16),
        jax.ShapeDtypeStruct((B, S, DN_QKV), F32),
        jax.ShapeDtypeStruct((B, S, DN_WIDTH), F32),
        jax.ShapeDtypeStruct((B, S, BA_PAD), F32),
        jax.ShapeDtypeStruct((B, n_ba, S), F32),
    ]
    out_specs = [row(WIN_WIDTH), row(WIN_KV_WIDTH), row(WIN_KV_WIDTH), row(DIFF_WIDTH),
                 col(DIFF_KT_ROWS), row(DIFF_WIDTH), row(DN_QKV), row(DN_WIDTH), row(BA_PAD), col(n_ba)]
    return pl.pallas_call(
        _inproj_body,
        grid=(B, S // tm),
        in_specs=[row(D_MODEL), _resident((1, D_MODEL)), _resident(w.shape),
                  _resident(wkt.shape), _resident(wbat.shape)],
        out_specs=out_specs,
        out_shape=out_shapes,
        compiler_params=pltpu.CompilerParams(
            dimension_semantics=("parallel", "parallel"), vmem_limit_bytes=VMEM_LIMIT),
        name="inproj",
    )(x, g.reshape(1, D_MODEL), w, wkt, wbat)


def _win_body(q_ref, kp_ref, kc_ref, kn_ref, vp_ref, vc_ref, vn_ref, sink_ref, slope_ref, o_ref, *, n_blocks):
    i = pl.program_id(1)
    W = WINDOW
    q = q_ref[0]
    KW = 3 * W
    kw = jnp.concatenate([kp_ref[0], kc_ref[0], kn_ref[0]], axis=0).astype(F32)
    vw = jnp.concatenate([vp_ref[0], vc_ref[0], vn_ref[0]], axis=0).astype(F32)
    kw_sw = pltpu.roll(kw, HEAD_DIM, 1)
    vw_sw = pltpu.roll(vw, HEAD_DIM, 1)
    left = lax.broadcasted_iota(jnp.int32, (KW, LANES), 1) < HEAD_DIM

    def stack(first, second):
        return jnp.concatenate([jnp.where(left, first, 0.0), jnp.where(left, 0.0, second)], axis=0).astype(BF16)

    zk = (stack(kw, kw_sw), stack(kw, kw), stack(kw_sw, kw))
    zv = (stack(vw, vw_sw), stack(vw, vw), stack(vw_sw, vw))
    qi = lax.broadcasted_iota(jnp.int32, (W, 3 * W), 0)
    ki = lax.broadcasted_iota(jnp.int32, (W, 3 * W), 1)
    rel = ki - W - qi
    dist = jnp.abs(rel)
    valid = dist <= W
    valid = jnp.logical_and(valid, jnp.logical_or(ki >= W, i > 0))
    valid = jnp.logical_and(valid, jnp.logical_or(ki < 2 * W, i < n_blocks - 1))
    distf = dist.astype(F32)
    n_pairs = WIN_Q_HEADS // 2
    s2 = [_dot_nt(q[:, p * LANES:(p + 1) * LANES], zk[p]) for p in range(n_pairs)]
    probs = []
    for hq in range(WIN_Q_HEADS):
        p, t = divmod(hq, 2)
        s = s2[p][:, t * KW:(t + 1) * KW] - slope_ref[:, hq:hq + 1] * distf
        s = jnp.where(valid, s, -1e30)
        sink = sink_ref[:, hq:hq + 1]
        m = jnp.maximum(jnp.max(s, axis=-1, keepdims=True), sink)
        e = jnp.exp(s - m)
        denom = jnp.sum(e, axis=-1, keepdims=True) + jnp.exp(sink - m)
        probs.append((e * (1.0 / denom)).astype(BF16))
    for p in range(n_pairs):
        pp = jnp.concatenate(probs[2 * p:2 * p + 2], axis=1)
        o_ref[0, :, p * LANES:(p + 1) * LANES] = _dot(pp, zv[p]).astype(BF16)


def _win_attention(q, k, v, sink, slopes):
    B, S, _ = q.shape
    W = WINDOW
    nb = S // W
    qspec = pl.BlockSpec((1, W, WIN_WIDTH), lambda b, i: (b, i, 0))
    prev = pl.BlockSpec((1, W, WIN_KV_WIDTH), lambda b, i: (b, jnp.maximum(i - 1, 0), 0))
    cur = pl.BlockSpec((1, W, WIN_KV_WIDTH), lambda b, i: (b, i, 0))
    nxt = pl.BlockSpec((1, W, WIN_KV_WIDTH), lambda b, i: (b, jnp.minimum(i + 1, nb - 1), 0))
    return pl.pallas_call(
        functools.partial(_win_body, n_blocks=nb),
        grid=(B, nb),
        in_specs=[qspec, prev, cur, nxt, prev, cur, nxt,
                  _resident((1, WIN_Q_HEADS)), _resident((1, WIN_Q_HEADS))],
        out_specs=pl.BlockSpec((1, W, WIN_WIDTH), lambda b, i: (b, i, 0)),
        out_shape=jax.ShapeDtypeStruct((B, S, WIN_WIDTH), BF16),
        compiler_params=pltpu.CompilerParams(dimension_semantics=("parallel", "parallel")),
        name="win_attn",
    )(q, k, k, k, v, v, v, sink.reshape(1, WIN_Q_HEADS).astype(F32), slopes.reshape(1, WIN_Q_HEADS))


def _diff_body(q_ref, kt_ref, v_ref, lam_ref, g_ref, o_ref, qv_ref, m_ref, l_ref, acc_ref,
               s0_ref, st0_ref, s1_ref, st1_ref, *, tq, tk, n_kv, lam_init):
    n_maps = 2 * DIFF_HEADS
    n_groups = tk // LANES
    q0 = pl.program_id(1) * tq
    jd = q0 // tk
    q = q_ref[0]
    lane_x = lax.broadcasted_iota(jnp.int32, (tq, DIFF_KROWS - DIFF_QK_DIM), 1)
    for var, sign in enumerate((1.0, -1.0, 0.0)):
        extras = jnp.where(lane_x < DIFF_BIAS_ROWS, -sign, 0.0).astype(BF16)
        for hm in range(n_maps):
            qv_ref[var, hm, :, 0:DIFF_QK_DIM] = q[:, hm * DIFF_QK_DIM:(hm + 1) * DIFF_QK_DIM]
            qv_ref[var, hm, :, DIFF_QK_DIM:DIFF_KROWS] = extras
    m_ref[...] = jnp.full(m_ref.shape, -jnp.inf, F32)
    l_ref[...] = jnp.zeros(l_ref.shape, F32)
    acc_ref[...] = jnp.zeros(acc_ref.shape, F32)
    qpos = (q0 + lax.broadcasted_iota(jnp.int32, (tq, LANES), 0)).astype(F32)
    n_off = n_kv - 1
    bufs = ((s0_ref, st0_ref), (s1_ref, st1_ref))

    for h in range(DIFF_HEADS):
        c = DIFF_SLOPES[h] * LOG2E

        def scores(t, buf, diag=False, h=h, c=c):
            s_ref, st_ref = bufs[buf]
            if diag:
                j, var, valid = jd, 2, None
                kpos = (jd * tk + lax.broadcasted_iota(jnp.int32, (tq, tk), 1)).astype(F32)
                qp = (q0 + lax.broadcasted_iota(jnp.int32, (tq, tk), 0)).astype(F32)
                bias = c * jnp.abs(qp - kpos)
                rowoff = None
            else:
                valid = t < n_off
                j = jnp.minimum(t + (t >= jd).astype(jnp.int32), n_kv - 1)
                after = j > jd
                var = jnp.where(after, 0, 1)
                rowoff = jnp.where(after, c, -c) * ((j * tk).astype(F32) - qpos)
            k0 = pl.multiple_of(j * tk, tk)
            for mp in range(2):
                hm = 2 * h + mp
                kt = kt_ref[0, hm * DIFF_KROWS:(hm + 1) * DIFF_KROWS, pl.ds(k0, tk)]
                s = _dot(qv_ref[var, hm], kt)
                if diag:
                    s = s - bias
                s_ref[mp] = s
                mx = s[:, 0:LANES]
                for g in range(1, n_groups):
                    mx = jnp.maximum(mx, s[:, g * LANES:(g + 1) * LANES])
                mrow = jnp.broadcast_to(jnp.max(mx, axis=-1, keepdims=True), (tq, LANES))
                m_old = m_ref[hm]
                if diag:
                    m_new = jnp.maximum(m_old, mrow)
                    shift = m_new
                else:
                    m_new = jnp.where(valid, jnp.maximum(m_old, mrow - rowoff), m_old)
                    shift = jnp.where(valid, m_new + rowoff, 1e30)
                st_ref[mp, 0] = shift
                st_ref[mp, 1] = jnp.exp2(m_old - m_new)
                m_ref[hm] = m_new

        def values(t, buf, diag=False, h=h):
            s_ref, st_ref = bufs[buf]
            if diag:
                j = jd
            else:
                j = jnp.minimum(t + (t >= jd).astype(jnp.int32), n_kv - 1)
            k0 = pl.multiple_of(j * tk, tk)
            v2 = v_ref[0, pl.ds(k0, tk), (h // 2) * LANES:(h // 2 + 1) * LANES]
            for mp in range(2):
                hm = 2 * h + mp
                shift = st_ref[mp, 0]
                alpha = st_ref[mp, 1]
                es = [jnp.exp2(s_ref[mp, :, g * LANES:(g + 1) * LANES] - shift) for g in range(n_groups)]
                lp = es[0]
                for g in range(1, n_groups):
                    lp = lp + es[g]
                e = jnp.concatenate([x.astype(BF16) for x in es], axis=1)
                l_ref[hm] = alpha * l_ref[hm] + lp
                acc_ref[hm] = alpha * acc_ref[hm] + _dot(e, v2)

        scores(0, 0, diag=True)
        values(0, 0, diag=True)
        scores(0, 0)

        def pair(p, carry):
            t = 2 * p
            scores(t + 1, 1)
            values(t, 0)
            scores(t + 2, 0)
            values(t + 1, 1)
            return carry

        lax.fori_loop(0, (n_off + 1) // 2, pair, 0)

    lp = lam_ref[...]
    lam = (jnp.exp(jnp.sum(lp[0:1] * lp[1:2], axis=-1, keepdims=True))
           - jnp.exp(jnp.sum(lp[2:3] * lp[3:4], axis=-1, keepdims=True)) + lam_init)
    lane = lax.broadcasted_iota(jnp.int32, (tq, LANES), 1)
    left = lane < HEAD_DIM
    ys = []
    for h in range(DIFF_HEADS):
        o1 = acc_ref[2 * h] / jnp.sum(l_ref[2 * h], axis=-1, keepdims=True)
        o2 = acc_ref[2 * h + 1] / jnp.sum(l_ref[2 * h + 1], axis=-1, keepdims=True)
        od = o1 - lam * o2
        mine = left if h % 2 == 0 else jnp.logical_not(left)
        ms = jnp.sum(jnp.where(mine, od * od, 0.0), axis=-1, keepdims=True) * (1.0 / HEAD_DIM)
        ys.append(od * lax.rsqrt(ms + EPS) * g_ref[...] * (1.0 - lam_init))
    for p in range(DIFF_HEADS // 2):
        o_ref[0, :, p * LANES:(p + 1) * LANES] = jnp.where(left, ys[2 * p], ys[2 * p + 1]).astype(BF16)


def _diff_attention(q, kt, v, diff_lambda, diff_g, lam_init, *, tq=256):
    B, S, _ = q.shape
    tk = DIFF_TK
    tq = min(tq, S)
    assert S % tk == 0 and tk % tq == 0
    n_maps = 2 * DIFF_HEADS
    g2 = jnp.concatenate([diff_g, diff_g]).reshape(1, LANES).astype(F32)
    return pl.pallas_call(
        functools.partial(_diff_body, tq=tq, tk=tk, n_kv=S // tk, lam_init=lam_init),
        grid=(B, S // tq),
        in_specs=[pl.BlockSpec((1, tq, DIFF_WIDTH), lambda b, i: (b, i, 0)),
                  pl.BlockSpec((1, DIFF_KT_ROWS, S), lambda b, i: (b, 0, 0)),
                  pl.BlockSpec((1, S, DIFF_WIDTH), lambda b, i: (b, 0, 0)),
                  _resident((4, DIFF_QK_DIM)), _resident((1, LANES))],
        out_specs=pl.BlockSpec((1, tq, DIFF_WIDTH), lambda b, i: (b, i, 0)),
        out_shape=jax.ShapeDtypeStruct((B, S, DIFF_WIDTH), BF16),
        scratch_shapes=[pltpu.VMEM((3, n_maps, tq, DIFF_KROWS), BF16),
                        pltpu.VMEM((n_maps, tq, LANES), F32),
                        pltpu.VMEM((n_maps, tq, LANES), F32),
                        pltpu.VMEM((n_maps, tq, LANES), F32),
                        pltpu.VMEM((2, tq, tk), F32), pltpu.VMEM((2, 2, tq, LANES), F32),
                        pltpu.VMEM((2, tq, tk), F32), pltpu.VMEM((2, 2, tq, LANES), F32)],
        compiler_params=pltpu.CompilerParams(
            dimension_semantics=("parallel", "arbitrary"), vmem_limit_bytes=VMEM_LIMIT),
        name="diff_attn",
    )(q, kt, v, diff_lambda.astype(F32), g2)


def _l2n64(x):
    return x * lax.rsqrt(_seg_sum64(x * x) + EPS)


def _dnprep_body(x_ref, xp_ref, xn_ref, w_ref, ba_ref, bat_ref, ab_ref, abt_ref,
                 q_ref, k_ref, v_ref, bg_ref, gt_ref, *, ts, n_tiles):
    i = pl.program_id(1)
    H = DN_HEADS
    x = x_ref[0]
    xp = jnp.where(i > 0, xp_ref[0], 0.0)
    xn = jnp.where(i < n_tiles - 1, xn_ref[0], 0.0)
    xx = jnp.concatenate([xp, x, xn], axis=0)
    half = CONV_W // 2
    y = jnp.zeros_like(x)
    for j in range(CONV_W):
        y = y + w_ref[j:j + 1, :] * xx[8 - half + j:8 - half + j + ts, :]
    y = _silu(y)
    q_ref[0] = _l2n64(y[:, :DN_WIDTH]) * (HEAD_DIM ** -0.5)
    k_ref[0] = _l2n64(y[:, DN_WIDTH:2 * DN_WIDTH])
    v_ref[0] = y[:, 2 * DN_WIDTH:]

    def gates(raw, a_log, dt_bias):
        beta = 1.0 / (1.0 + jnp.exp(-raw))
        z = raw + dt_bias
        softplus = jnp.maximum(z, 0.0) + jnp.log(1.0 + jnp.exp(-jnp.abs(z)))
        return beta, -jnp.exp(a_log) * softplus

    ba = ba_ref[0]
    beta, g = gates(ba, ab_ref[0:1, :], ab_ref[1:2, :])
    lane = lax.broadcasted_iota(jnp.int32, ba.shape, 1)
    r = lax.broadcasted_iota(jnp.int32, ba.shape, 0) % DN_CHUNK
    bwd = jnp.logical_and(lane >= 3 * H, lane < 4 * H)
    gc = g
    for d in (1, 2, 4, 8, 16, 32):
        f = jnp.where(r >= d, pltpu.roll(gc, d, 0), 0.0)
        b = jnp.where(r < DN_CHUNK - d, pltpu.roll(gc, ts - d, 0), 0.0)
        gc = gc + jnp.where(bwd, b, f)
    bg_ref[0] = jnp.where(lane < 2 * H, beta, gc)
    bat = bat_ref[0]
    _, gt = gates(bat, abt_ref[:, 0:1], abt_ref[:, 1:2])
    row = lax.broadcasted_iota(jnp.int32, bat.shape, 0)
    c = lax.broadcasted_iota(jnp.int32, bat.shape, 1) % DN_CHUNK
    bwd_t = row >= 3 * H
    for d in (1, 2, 4, 8, 16, 32):
        f = jnp.where(c >= d, pltpu.roll(gt, d, 1), 0.0)
        b = jnp.where(c < DN_CHUNK - d, pltpu.roll(gt, ts - d, 1), 0.0)
        gt = gt + jnp.where(bwd_t, b, f)
    gt_ref[0] = gt


def _dn_prep(dn_raw, ba, bat, conv_w, a_log, dt_bias, *, ts=256):
    B, S, C = dn_raw.shape
    ts = min(ts, S)
    nt = S // ts
    n_ba = 4 * DN_HEADS
    hb = ts // 8
    zeros = jnp.zeros((2 * DN_HEADS,), F32)
    al = jnp.concatenate([zeros, a_log.reshape(-1).astype(F32)])
    db = jnp.concatenate([zeros, dt_bias.reshape(-1).astype(F32)])
    ab = jnp.pad(jnp.stack([al, db]), ((0, 0), (0, BA_PAD - n_ba)))
    abt = jnp.stack([al, db], axis=1)
    row = lambda n: pl.BlockSpec((1, ts, n), lambda b, i: (b, i, 0))
    return pl.pallas_call(
        functools.partial(_dnprep_body, ts=ts, n_tiles=nt),
        grid=(B, nt),
        in_specs=[row(C),
                  pl.BlockSpec((1, 8, C), lambda b, i: (b, jnp.maximum(i * hb - 1, 0), 0)),
                  pl.BlockSpec((1, 8, C), lambda b, i: (b, jnp.minimum((i + 1) * hb, S // 8 - 1), 0)),
                  _resident((CONV_W, C)), row(BA_PAD),
                  pl.BlockSpec((1, n_ba, ts), lambda b, i: (b, 0, i)),
                  _resident((2, BA_PAD)), _resident((n_ba, 2))],
        out_specs=[row(DN_WIDTH), row(DN_WIDTH), row(DN_WIDTH), row(BA_PAD),
                   pl.BlockSpec((1, n_ba, ts), lambda b, i: (b, 0, i))],
        out_shape=[jax.ShapeDtypeStruct((B, S, DN_WIDTH), F32)] * 3
        + [jax.ShapeDtypeStruct((B, S, BA_PAD), F32), jax.ShapeDtypeStruct((B, n_ba, S), F32)],
        compiler_params=pltpu.CompilerParams(dimension_semantics=("parallel", "parallel")),
        name="dn_prep",
    )(dn_raw, dn_raw, dn_raw, conv_w.astype(F32), ba, bat, ab, abt)


def _bd_rows(y, left):
    return jnp.concatenate([jnp.where(left, y, 0.0), jnp.where(left, 0.0, y)], axis=0).astype(BF16)


def _dnscan_body(qf_ref, kf_ref, vf_ref, bgf_ref, gtf_ref, qb_ref, kb_ref, vb_ref, bgb_ref, gtb_ref,
                 of_ref, ob_ref, st_ref, *, G):
    C = DN_CHUNK
    H = DN_HEADS
    P = H // 2

    @pl.when(pl.program_id(1) == 0)
    def _():
        st_ref[...] = jnp.zeros(st_ref.shape, F32)

    lane1 = lax.broadcasted_iota(jnp.int32, (C, LANES), 1)
    row1 = lax.broadcasted_iota(jnp.int32, (C, LANES), 0)
    left1 = lane1 < C
    s_idx = lane1 % C
    eye = (row1 == s_idx).astype(F32)
    lane2 = lax.broadcasted_iota(jnp.int32, (C, 2 * LANES), 1)
    left2 = (lane2 % LANES) < C
    rr = lax.broadcasted_iota(jnp.int32, (LANES, LANES), 0)
    cc = lax.broadcasted_iota(jnp.int32, (LANES, LANES), 1)
    top = rr < C
    bdmask = (rr < C) == (cc < C)
    top_c = lax.broadcasted_iota(jnp.int32, (LANES, C), 0) < C

    dirs = ((0, qf_ref, kf_ref, vf_ref, bgf_ref, gtf_ref, of_ref),
            (1, qb_ref, kb_ref, vb_ref, bgb_ref, gtb_ref, ob_ref))
    chains = []
    for d, q_ref, k_ref, v_ref, bg_ref, gt_ref, _ in dirs:
        tri = (row1 >= s_idx) if d == 0 else (row1 <= s_idx)
        strict = (row1 > s_idx) if d == 0 else (row1 < s_idx)
        last = C - 1 if d == 0 else 0
        for p in range(P):
            sl = slice(p * LANES, (p + 1) * LANES)
            cb = d * H + 2 * p
            cg = 2 * H + d * H + 2 * p
            kT = k_ref[0, :, sl].T
            for c in range(G):
                rs = slice(c * C, (c + 1) * C)
                qc, kc, vc = q_ref[0, rs, sl], k_ref[0, rs, sl], v_ref[0, rs, sl]
                bexp = jnp.where(left1, bg_ref[0, rs, cb:cb + 1], bg_ref[0, rs, cb + 1:cb + 2])
                gcc = jnp.where(left1, bg_ref[0, rs, cg:cg + 1], bg_ref[0, rs, cg + 1:cg + 2])
                g0 = gt_ref[0, cg:cg + 1, rs]
                g1 = gt_ref[0, cg + 1:cg + 2, rs]
                grow = jnp.concatenate([g0, g1], axis=1)
                glast = gcc[last:last + 1, :]
                gl0 = glast[:, 0:1]
                gl1 = glast[:, C:C + 1]
                eg = jnp.exp(gcc)
                kb_ = kc * bexp
                fac = jnp.exp(jnp.minimum(jnp.where(top_c, gl0 - g0, gl1 - g1), 0.0))
                chains.append(dict(
                    d=d, p=p, c=c,
                    rhs=jnp.concatenate([vc * bexp, kb_ * eg], axis=1),
                    qg=(qc * eg).astype(BF16),
                    kq=jnp.concatenate([kb_, qc], axis=0).astype(BF16),
                    Z=_bd_rows(kc, left1),
                    dec=jnp.where(tri, jnp.exp(jnp.minimum(gcc - grow, 0.0)), 0.0),
                    strict=strict,
                    kg=(kT[:, rs] * fac).astype(BF16),
                    glm=jnp.exp(jnp.where(top, gl0, gl1))))
    for ch in chains:
        kkqk = _dot_nt(ch["kq"], ch["Z"])
        ch["N"] = jnp.where(ch["strict"], -kkqk[:C] * ch["dec"], 0.0)
        ch["qk"] = (kkqk[C:] * ch["dec"]).astype(BF16)
    for ch in chains:
        ch["Pm"] = eye + ch["N"]
    for j in range(6):
        for ch in chains:
            nb = _bd_rows(ch["N"], left1)
            if j == 0:
                ch["N"] = _dot(ch["N"].astype(BF16), nb)
            elif j < 5:
                both = _dot(jnp.concatenate([ch["Pm"], ch["N"]], axis=0).astype(BF16), nb)
                ch["Pm"] = ch["Pm"] + both[:C]
                ch["N"] = both[C:]
            else:
                ch["Pm"] = ch["Pm"] + _dot(ch["Pm"].astype(BF16), nb)
    for ch in chains:
        X = _dot(ch["Pm"].astype(BF16), _bd_rows(ch["rhs"], left2))
        ch["u"] = X[:, :LANES]
        ch["wq"] = jnp.concatenate([X[:, LANES:].astype(BF16), ch["qg"]], axis=0)
    by_key = {(ch["d"], ch["p"], ch["c"]): ch for ch in chains}

    for step in range(G):
        live = []
        for d, *_refs, o_ref in dirs:
            c = step if d == 0 else G - 1 - step
            for p in range(P):
                live.append((d * P + p, o_ref, slice(c * C, (c + 1) * C), p, by_key[(d, p, c)]))
        S = [st_ref[sidx] for sidx, *_ in live]
        t1 = [_dot(ch["wq"], S[i].astype(BF16)) for i, (*_, ch) in enumerate(live)]
        vn = [ch["u"] - t1[i][:C] for i, (*_, ch) in enumerate(live)]
        for i, (sidx, o_ref, rs, p, ch) in enumerate(live):
            o_ref[0, rs, p * LANES:(p + 1) * LANES] = t1[i][C:] + _dot(ch["qk"], _bd_rows(vn[i], left1))
        for i, (sidx, o_ref, rs, p, ch) in enumerate(live):
            upd = _dot(ch["kg"], vn[i].astype(BF16))
            st_ref[sidx] = S[i] * ch["glm"] + jnp.where(bdmask, upd, 0.0)


def _dn_scan(q, k, v, bg, gt, *, G=4):
    B, S, _ = q.shape
    G = min(G, S // DN_CHUNK)
    R = G * DN_CHUNK
    nb = S // R
    n_ba = 4 * DN_HEADS
    P = DN_HEADS // 2
    fw = lambda n: pl.BlockSpec((1, R, n), lambda b, i: (b, i, 0))
    bw = lambda n: pl.BlockSpec((1, R, n), lambda b, i: (b, nb - 1 - i, 0))
    fwt = pl.BlockSpec((1, n_ba, R), lambda b, i: (b, 0, i))
    bwt = pl.BlockSpec((1, n_ba, R), lambda b, i: (b, 0, nb - 1 - i))
    return pl.pallas_call(
        functools.partial(_dnscan_body, G=G),
        grid=(B, nb),
        in_specs=[fw(DN_WIDTH), fw(DN_WIDTH), fw(DN_WIDTH), fw(BA_PAD), fwt,
                  bw(DN_WIDTH), bw(DN_WIDTH), bw(DN_WIDTH), bw(BA_PAD), bwt],
        out_specs=[fw(DN_WIDTH), bw(DN_WIDTH)],
        out_shape=[jax.ShapeDtypeStruct((B, S, DN_WIDTH), F32)] * 2,
        scratch_shapes=[pltpu.VMEM((2 * P, LANES, LANES), F32)],
        compiler_params=pltpu.CompilerParams(dimension_semantics=("parallel", "arbitrary")),
        name="dn_scan",
    )(q, k, v, bg, gt, q, k, v, bg, gt)


def _outproj_body(x_ref, ow_ref, od_ref, of_ref, ob_ref, z_ref, g_ref, w_ref, o_ref, cat_ref):
    oc = of_ref[0] + ob_ref[0]
    on = oc * lax.rsqrt(_seg_sum64(oc * oc) * (1.0 / HEAD_DIM) + EPS) * g_ref[...] * _silu(z_ref[0])
    cat_ref[:, :WIN_WIDTH] = ow_ref[0]
    cat_ref[:, WIN_WIDTH:WIN_WIDTH + DIFF_WIDTH] = od_ref[0]
    cat_ref[:, WIN_WIDTH + DIFF_WIDTH:] = on.astype(BF16)
    o_ref[0] = x_ref[0] + _dot(cat_ref[...], w_ref[...])


def _outproj(x, o_win, o_diff, o_f, o_b, z, dn_g, w_out, *, tm=512):
    B, S, _ = x.shape
    tm = min(tm, S)
    row = lambda n: pl.BlockSpec((1, tm, n), lambda b, i: (b, i, 0))
    g = jnp.tile(dn_g.astype(F32), DN_HEADS).reshape(1, DN_WIDTH)
    return pl.pallas_call(
        _outproj_body,
        grid=(B, S // tm),
        in_specs=[row(D_MODEL), row(WIN_WIDTH), row(DIFF_WIDTH), row(DN_WIDTH), row(DN_WIDTH),
                  row(DN_WIDTH), _resident((1, DN_WIDTH)), _resident((MIX_WIDTH, D_MODEL))],
        out_specs=row(D_MODEL),
        out_shape=jax.ShapeDtypeStruct((B, S, D_MODEL), F32),
        scratch_shapes=[pltpu.VMEM((tm, MIX_WIDTH), BF16)],
        compiler_params=pltpu.CompilerParams(
            dimension_semantics=("parallel", "parallel"), vmem_limit_bytes=VMEM_LIMIT),
        name="outproj",
    )(x, o_win, o_diff, o_f, o_b, z, g, w_out.astype(BF16))


def _mixer(x, ln_mix, w_mix_in, conv_w, sink, diff_lam, diff_g, a_log, dt_bias, dn_g, w_mix_out, lam_init):
    wq, wk, wv, dq, dkt, dv, dn_raw, z, ba, bat = _inproj(x, ln_mix, w_mix_in)
    win_slopes = 2.0 ** (-8.0 * jnp.arange(1, WIN_Q_HEADS + 1, dtype=F32) / WIN_Q_HEADS)
    o_win = _win_attention(wq, wk, wv, sink, win_slopes)
    o_diff = _diff_attention(dq, dkt, dv, diff_lam, diff_g, lam_init)
    q, k, v, bg, gt = _dn_prep(dn_raw, ba, bat, conv_w, a_log, dt_bias)
    o_f, o_b = _dn_scan(q, k, v, bg, gt)
    return _outproj(x, o_win, o_diff, o_f, o_b, z, dn_g, w_mix_out)


def kernel(x, ln_ffn1, ffn1_w_in, ffn1_w_out, ln_mix, w_mix_in, conv_w, sink_logits, diff_lambda,
           diff_norm_g, dn_A_log, dn_dt_bias, dn_norm_g, w_mix_out, ln_ffn2, ffn2_w_in, ffn2_w_out,
           ln_final):
    B, S, D = x.shape
    depth = ln_ffn1.shape[0]
    for l in range(depth):
        lam_init = 0.8 - 0.6 * math.exp(-0.3 * l)
        x = _ffn(x.reshape(B * S, D), ln_ffn1[l], ffn1_w_in[l], ffn1_w_out[l]).reshape(B, S, D)
        x = _mixer(x, ln_mix[l], w_mix_in[l], conv_w[l], sink_logits[l], diff_lambda[l], diff_norm_g[l],
                   dn_A_log[l], dn_dt_bias[l], dn_norm_g[l], w_mix_out[l], lam_init)
        g_final = ln_final if l == depth - 1 else None
        x = _ffn(x.reshape(B * S, D), ln_ffn2[l], ffn2_w_in[l], ffn2_w_out[l], g_final).reshape(B, S, D)
    return x
```

```python
import functools
import math

import jax
import jax.numpy as jnp
from jax import lax
from jax.experimental import pallas as pl
from jax.experimental.pallas import tpu as pltpu

F32 = jnp.float32
BF16 = jnp.bfloat16

D_MODEL = 1024
HEAD_DIM = 64
EPS = 1e-6
WIN_Q_HEADS = 6
WIN_KV_HEADS = 2
WIN_GROUP = WIN_Q_HEADS // WIN_KV_HEADS
WINDOW = 128
DIFF_HEADS = 4
DIFF_QK_DIM = HEAD_DIM // 2
DN_HEADS = 6
DN_CHUNK = 64
CONV_W = 5
D_FF = 2752
WIN_WIDTH = WIN_Q_HEADS * HEAD_DIM
WIN_KV_WIDTH = WIN_KV_HEADS * HEAD_DIM
DIFF_WIDTH = DIFF_HEADS * HEAD_DIM
DN_WIDTH = DN_HEADS * HEAD_DIM
DN_QKV = 3 * DN_WIDTH
MIX_WIDTH = WIN_WIDTH + DIFF_WIDTH + DN_WIDTH
OFF_WQ = 0
OFF_WK = OFF_WQ + WIN_WIDTH
OFF_WV = OFF_WK + WIN_KV_WIDTH
OFF_DQ = OFF_WV + WIN_KV_WIDTH
OFF_DK = OFF_DQ + DIFF_WIDTH
OFF_DV = OFF_DK + DIFF_WIDTH
OFF_DN = OFF_DV + DIFF_WIDTH
OFF_Z = OFF_DN + DN_QKV
OFF_BA = OFF_Z + DN_WIDTH
MIX_IN = OFF_BA + 4 * DN_HEADS

LANES = 128
FF_CHUNK = 256
D_FF_PAD = -(-D_FF // FF_CHUNK) * FF_CHUNK
VMEM_LIMIT = 56 * 1024 * 1024
BF16_ROWS = 16
LOG2E = math.log2(math.e)
DIFF_TK = 512
DIFF_KROWS = 48
DIFF_BIAS_ROWS = 3
DIFF_KT_ROWS = 2 * DIFF_HEADS * DIFF_KROWS
DIFF_SKIP_LOG2 = 150.0
DIFF_SLOPES = tuple(2.0 ** (-8.0 * (i + 1) / DIFF_HEADS) for i in range(DIFF_HEADS))


def _rms(x, g):
    return x * lax.rsqrt(jnp.mean(x * x, axis=-1, keepdims=True) + EPS) * g


def _silu(x):
    h = 0.5 * x
    return h + h * jnp.tanh(h)


def _seg_sum64(x):
    R, width = x.shape
    n = width // LANES
    rows = jnp.concatenate([x[:, c * LANES:(c + 1) * LANES] for c in range(n)], axis=0)
    r = lax.broadcasted_iota(jnp.int32, (LANES, LANES), 0) < HEAD_DIM
    c = lax.broadcasted_iota(jnp.int32, (LANES, LANES), 1) < HEAD_DIM
    ones_bd = (r == c).astype(F32).astype(BF16)
    hi = rows.astype(BF16)
    r1 = rows - hi.astype(F32)
    mid = r1.astype(BF16)
    lo = (r1 - mid.astype(F32)).astype(BF16)
    ss = _dot(hi, ones_bd) + _dot(mid, ones_bd) + _dot(lo, ones_bd)
    return jnp.concatenate([ss[c * R:(c + 1) * R] for c in range(n)], axis=1)


def _dot(a, b):
    return jnp.dot(a, b, preferred_element_type=F32)


def _dot_nt(a, b):
    return lax.dot_general(a, b, (((1,), (1,)), ((), ())), preferred_element_type=F32)


def _resident(shape):
    nd = len(shape)
    return pl.BlockSpec(shape, lambda *_: (0,) * nd, pipeline_mode=pl.Buffered(1))


def _ffn_body(x_ref, g_ref, wgu_ref, wo_ref, *rest, n_chunks, final):
    if final:
        gf_ref, o_ref, acc_ref = rest
    else:
        o_ref, acc_ref = rest
    x = x_ref[...]
    h = _rms(x, g_ref[...]).astype(BF16)
    for c in range(n_chunks):
        gu = _dot(h, wgu_ref[:, c * 2 * FF_CHUNK:(c + 1) * 2 * FF_CHUNK])
        a = (_silu(gu[:, :FF_CHUNK]) * gu[:, FF_CHUNK:]).astype(BF16)
        part = _dot(a, wo_ref[c * FF_CHUNK:(c + 1) * FF_CHUNK, :])
        if c == 0:
            acc_ref[...] = part
        else:
            acc_ref[...] += part
    y = x + 0.5 * acc_ref[...]
    if final:
        y = _rms(y, gf_ref[...])
    o_ref[...] = y


def _prep_ffn_weights(w_in, w_out):
    n = D_FF_PAD // FF_CHUNK
    pad = D_FF_PAD - D_FF
    wg = jnp.pad(w_in[:, :D_FF], ((0, 0), (0, pad))).reshape(D_MODEL, n, FF_CHUNK)
    wu = jnp.pad(w_in[:, D_FF:], ((0, 0), (0, pad))).reshape(D_MODEL, n, FF_CHUNK)
    wgu = jnp.concatenate([wg, wu], axis=2).reshape(D_MODEL, n * 2 * FF_CHUNK).astype(BF16)
    wo = jnp.pad(w_out, ((0, pad), (0, 0))).astype(BF16)
    return wgu, wo


def _ffn(x2d, g, w_in, w_out, g_final=None, *, tm=512):
    T = x2d.shape[0]
    tm = min(tm, T)
    n_chunks = D_FF_PAD // FF_CHUNK
    wgu, wo = _prep_ffn_weights(w_in, w_out)
    final = g_final is not None
    in_specs = [
        pl.BlockSpec((tm, D_MODEL), lambda i: (i, 0)),
        _resident((1, D_MODEL)),
        _resident(wgu.shape),
        _resident(wo.shape),
    ]
    args = [x2d, g.reshape(1, D_MODEL), wgu, wo]
    if final:
        in_specs.append(_resident((1, D_MODEL)))
        args.append(g_final.reshape(1, D_MODEL))
    return pl.pallas_call(
        functools.partial(_ffn_body, n_chunks=n_chunks, final=final),
        grid=(T // tm,),
        in_specs=in_specs,
        out_specs=pl.BlockSpec((tm, D_MODEL), lambda i: (i, 0)),
        out_shape=jax.ShapeDtypeStruct((T, D_MODEL), F32),
        scratch_shapes=[pltpu.VMEM((tm, D_MODEL), F32)],
        compiler_params=pltpu.CompilerParams(
            dimension_semantics=("parallel",), vmem_limit_bytes=VMEM_LIMIT),
        name="ffn_final" if final else "ffn",
    )(*args)


BA_PAD = LANES


def _inproj_body(x_ref, g_ref, w_ref, wkt_ref, wbat_ref,
                 wq_ref, wk_ref, wv_ref, dq_ref, dkt_ref, dv_ref, dn_ref, z_ref, ba_ref, bat_ref):
    h = _rms(x_ref[0], g_ref[...]).astype(BF16)
    wq_ref[0] = (_dot(h, w_ref[:, OFF_WQ:OFF_WK]) * (HEAD_DIM ** -0.5 * LOG2E)).astype(BF16)
    wk_ref[0] = _dot(h, w_ref[:, OFF_WK:OFF_WV]).astype(BF16)
    wv_ref[0] = _dot(h, w_ref[:, OFF_WV:OFF_DQ]).astype(BF16)
    dq_ref[0] = (_dot(h, w_ref[:, OFF_DQ:OFF_DK]) * (DIFF_QK_DIM ** -0.5 * LOG2E)).astype(BF16)
    kt = _dot_nt(wkt_ref[...], h)
    tm = kt.shape[1]
    extra_rows = DIFF_KROWS - DIFF_QK_DIM
    pos = pl.program_id(1) * tm + lax.broadcasted_iota(jnp.int32, (extra_rows, tm), 1)
    kr = (pos % DIFF_TK).astype(F32)
    rowi = lax.broadcasted_iota(jnp.int32, (extra_rows, tm), 0)
    for hd in range(DIFF_HEADS):
        bias = (DIFF_SLOPES[hd] * LOG2E) * kr
        hi = bias.astype(BF16).astype(F32)
        mid = (bias - hi).astype(BF16).astype(F32)
        lo = bias - hi - mid
        extra = jnp.where(rowi == 0, hi, jnp.where(rowi == 1, mid, jnp.where(rowi == 2, lo, 0.0))).astype(BF16)
        for mp in range(2):
            hm = 2 * hd + mp
            dkt_ref[0, hm * DIFF_KROWS:hm * DIFF_KROWS + DIFF_QK_DIM, :] = (
                kt[hm * DIFF_QK_DIM:(hm + 1) * DIFF_QK_DIM, :].astype(BF16))
            dkt_ref[0, hm * DIFF_KROWS + DIFF_QK_DIM:(hm + 1) * DIFF_KROWS, :] = extra
    dv_ref[0] = _dot(h, w_ref[:, OFF_DV:OFF_DN]).astype(BF16)
    dn_ref[0] = _dot(h, w_ref[:, OFF_DN:OFF_Z])
    z_ref[0] = _dot(h, w_ref[:, OFF_Z:OFF_BA])
    ba_ref[0] = _dot(h, w_ref[:, OFF_BA:OFF_BA + BA_PAD])
    bat_ref[0] = _dot_nt(wbat_ref[...], h)


def _inproj(x, g, w_in, *, tm=512):
    B, S, _ = x.shape
    tm = min(tm, S)
    n_ba = 4 * DN_HEADS
    w = jnp.pad(w_in, ((0, 0), (0, OFF_BA + BA_PAD - MIX_IN))).astype(BF16)
    wkt = w_in[:, OFF_DK:OFF_DV].T.astype(BF16)
    wbat = w_in[:, OFF_BA:MIX_IN].T.astype(BF16)
    row = lambda n: pl.BlockSpec((1, tm, n), lambda b, i: (b, i, 0))
    col = lambda n: pl.BlockSpec((1, n, tm), lambda b, i: (b, 0, i))
    out_shapes = [
        jax.ShapeDtypeStruct((B, S, WIN_WIDTH), BF16),
        jax.ShapeDtypeStruct((B, S, WIN_KV_WIDTH), BF16),
        jax.ShapeDtypeStruct((B, S, WIN_KV_WIDTH), BF16),
        jax.ShapeDtypeStruct((B, S, DIFF_WIDTH), BF16),
        jax.ShapeDtypeStruct((B, DIFF_KT_ROWS, S), BF16),
        jax.ShapeDtypeStruct((B, S, DIFF_WIDTH), BF16),
        jax.ShapeDtypeStruct((B, S, DN_QKV), F32),
        jax.ShapeDtypeStruct((B, S, DN_WIDTH), F32),
        jax.ShapeDtypeStruct((B, S, BA_PAD), F32),
        jax.ShapeDtypeStruct((B, n_ba, S), F32),
    ]
    out_specs = [row(WIN_WIDTH), row(WIN_KV_WIDTH), row(WIN_KV_WIDTH), row(DIFF_WIDTH),
                 col(DIFF_KT_ROWS), row(DIFF_WIDTH), row(DN_QKV), row(DN_WIDTH), row(BA_PAD), col(n_ba)]
    return pl.pallas_call(
        _inproj_body,
        grid=(B, S // tm),
        in_specs=[row(D_MODEL), _resident((1, D_MODEL)), _resident(w.shape),
                  _resident(wkt.shape), _resident(wbat.shape)],
        out_specs=out_specs,
        out_shape=out_shapes,
        compiler_params=pltpu.CompilerParams(
            dimension_semantics=("parallel", "parallel"), vmem_limit_bytes=VMEM_LIMIT),
        name="inproj",
    )(x, g.reshape(1, D_MODEL), w, wkt, wbat)


def _win_body(q_ref, kp_ref, kc_ref, kn_ref, vp_ref, vc_ref, vn_ref, sink_ref, slope_ref, o_ref,
              *, n_steps, nq):
    i = pl.program_id(1)
    W = WINDOW
    KW = 3 * W
    kw = jnp.concatenate([kp_ref[0], kc_ref[0], kn_ref[0]], axis=0).astype(F32)
    vw = jnp.concatenate([vp_ref[0], vc_ref[0], vn_ref[0]], axis=0).astype(F32)
    kw_sw = pltpu.roll(kw, HEAD_DIM, 1)
    vw_sw = pltpu.roll(vw, HEAD_DIM, 1)
    left = lax.broadcasted_iota(jnp.int32, (KW, LANES), 1) < HEAD_DIM
    qi = lax.broadcasted_iota(jnp.int32, (W, KW), 0)
    ki = lax.broadcasted_iota(jnp.int32, (W, KW), 1)
    dist = jnp.abs(ki - W - qi)
    band_bias = [jnp.where(dist <= W, (-LOG2E) * slope_ref[:, hq:hq + 1] * dist.astype(F32), -1e30)
                 for hq in range(WIN_Q_HEADS)]
    edge_lo = jnp.where(jnp.logical_or(ki >= W, i > 0), 0.0, -1e30)
    edge_hi = jnp.where(jnp.logical_or(ki < 2 * W, i < n_steps - 1), 0.0, -1e30)
    n_pairs = WIN_Q_HEADS // 2
    blocks = []
    for a in range(nq):
        rows = slice(a * W, a * W + KW)

        def stack(first, second, rows=rows):
            return jnp.concatenate([jnp.where(left, first[rows], 0.0),
                                    jnp.where(left, 0.0, second[rows])], axis=0).astype(BF16)

        zk = (stack(kw, kw_sw), stack(kw, kw), stack(kw_sw, kw))
        zv = (stack(vw, vw_sw), stack(vw, vw), stack(vw_sw, vw))
        edge = None
        if a == 0:
            edge = edge_lo
        if a == nq - 1:
            edge = edge_hi if edge is None else edge + edge_hi
        q = q_ref[0, a * W:(a + 1) * W, :]
        s2 = [_dot_nt(q[:, p * LANES:(p + 1) * LANES], zk[p]) for p in range(n_pairs)]
        blocks.append((a, edge, zv, s2))
    probs = {}
    for a, edge, zv, s2 in blocks:
        for hq in range(WIN_Q_HEADS):
            p, t = divmod(hq, 2)
            s = s2[p][:, t * KW:(t + 1) * KW] + band_bias[hq]
            if edge is not None:
                s = s + edge
            sink = LOG2E * sink_ref[:, hq:hq + 1]
            m = jnp.maximum(jnp.max(s, axis=-1, keepdims=True), sink)
            e = jnp.exp2(s - m)
            denom = jnp.sum(e, axis=-1, keepdims=True) + jnp.exp2(sink - m)
            probs[(a, hq)] = (e * (1.0 / denom)).astype(BF16)
    for a, edge, zv, s2 in blocks:
        for p in range(n_pairs):
            pp = jnp.concatenate([probs[(a, 2 * p)], probs[(a, 2 * p + 1)]], axis=1)
            o_ref[0, a * W:(a + 1) * W, p * LANES:(p + 1) * LANES] = _dot(pp, zv[p]).astype(BF16)


def _win_attention(q, k, v, sink, slopes, *, nq=2):
    B, S, _ = q.shape
    W = WINDOW
    nq = min(nq, S // W)
    ns = S // (nq * W)
    nb = S // W
    qspec = pl.BlockSpec((1, nq * W, WIN_WIDTH), lambda b, i: (b, i, 0))
    prev = pl.BlockSpec((1, W, WIN_KV_WIDTH), lambda b, i: (b, jnp.maximum(i * nq - 1, 0), 0))
    cur = pl.BlockSpec((1, nq * W, WIN_KV_WIDTH), lambda b, i: (b, i, 0))
    nxt = pl.BlockSpec((1, W, WIN_KV_WIDTH), lambda b, i: (b, jnp.minimum((i + 1) * nq, nb - 1), 0))
    return pl.pallas_call(
        functools.partial(_win_body, n_steps=ns, nq=nq),
        grid=(B, ns),
        in_specs=[qspec, prev, cur, nxt, prev, cur, nxt,
                  _resident((1, WIN_Q_HEADS)), _resident((1, WIN_Q_HEADS))],
        out_specs=pl.BlockSpec((1, nq * W, WIN_WIDTH), lambda b, i: (b, i, 0)),
        out_shape=jax.ShapeDtypeStruct((B, S, WIN_WIDTH), BF16),
        compiler_params=pltpu.CompilerParams(dimension_semantics=("parallel", "parallel")),
        name="win_attn",
    )(q, k, k, k, v, v, v, sink.reshape(1, WIN_Q_HEADS).astype(F32), slopes.reshape(1, WIN_Q_HEADS))


def _diff_ranges(q, kt, tq, tk):
    B, S, _ = q.shape
    nq, nk, n_maps = S // tq, S // tk, 2 * DIFF_HEADS
    qf = q.astype(F32).reshape(B, nq, tq, n_maps, DIFF_QK_DIM)
    qn = jnp.sqrt(jnp.max(jnp.sum(qf * qf, axis=-1), axis=2))
    kf = kt.astype(F32).reshape(B, n_maps, DIFF_KROWS, S)[:, :, :DIFF_QK_DIM]
    kn_pos = jnp.sqrt(jnp.sum(kf * kf, axis=2))
    kn = jnp.max(kn_pos.reshape(B, n_maps, nk, tk), axis=-1)
    kown = jnp.max(kn_pos.reshape(B, n_maps, nq, tq), axis=-1)
    reach = jnp.swapaxes(qn, 1, 2)[..., None] * (kn[:, :, None, :] + kown[..., None]) * (1.0 + 1e-3)
    reach = jnp.max(reach.reshape(B, DIFF_HEADS, 2, nq, nk), axis=2)
    q0 = jnp.arange(nq, dtype=jnp.int32) * tq
    k0 = jnp.arange(nk, dtype=jnp.int32) * tk
    dmin = jnp.maximum(jnp.maximum(k0[None, :] - (q0[:, None] + tq - 1), q0[:, None] - (k0[None, :] + tk - 1)), 0)
    c = jnp.asarray(DIFF_SLOPES, F32) * LOG2E
    need = reach - c[None, :, None, None] * dmin.astype(F32)[None, None] > -DIFF_SKIP_LOG2
    idx = jnp.arange(nk, dtype=jnp.int32)
    need = jnp.logical_or(need, (idx[None, :] == (q0 // tk)[:, None])[None, None])
    lo = jnp.min(jnp.where(need, idx, nk), axis=-1)
    hi = jnp.max(jnp.where(need, idx, -1), axis=-1)
    return jnp.stack([lo, hi], axis=-1).transpose(0, 2, 1, 3).reshape(-1).astype(jnp.int32)


def _diff_body(rng_ref, q_ref, kt_ref, v_ref, lam_ref, g_ref, o_ref, qv_ref, m_ref, l_ref, acc_ref,
               s0_ref, st0_ref, s1_ref, st1_ref, *, tq, tk, lam_init):
    n_maps = 2 * DIFF_HEADS
    n_groups = tk // LANES
    q0 = pl.program_id(1) * tq
    jd = q0 // tk
    q = q_ref[0]
    lane_x = lax.broadcasted_iota(jnp.int32, (tq, DIFF_KROWS - DIFF_QK_DIM), 1)
    for var, sign in enumerate((1.0, -1.0, 0.0)):
        extras = jnp.where(lane_x < DIFF_BIAS_ROWS, -sign, 0.0).astype(BF16)
        for hm in range(n_maps):
            qv_ref[var, hm, :, 0:DIFF_QK_DIM] = q[:, hm * DIFF_QK_DIM:(hm + 1) * DIFF_QK_DIM]
            qv_ref[var, hm, :, DIFF_QK_DIM:DIFF_KROWS] = extras
    m_ref[...] = jnp.full(m_ref.shape, -jnp.inf, F32)
    l_ref[...] = jnp.zeros(l_ref.shape, F32)
    acc_ref[...] = jnp.zeros(acc_ref.shape, F32)
    qpos = (q0 + lax.broadcasted_iota(jnp.int32, (tq, LANES), 0)).astype(F32)
    bufs = ((s0_ref, st0_ref), (s1_ref, st1_ref))
    rng_base = (pl.program_id(0) * pl.num_programs(1) + pl.program_id(1)) * (2 * DIFF_HEADS)

    for h in range(DIFF_HEADS):
        c = DIFF_SLOPES[h] * LOG2E
        lo = rng_ref[rng_base + 2 * h]
        n_off = rng_ref[rng_base + 2 * h + 1] - lo

        def tile_of(u, lo=lo):
            t = lo + u - 1
            return jnp.where(u == 0, jd, t + (t >= jd).astype(jnp.int32))

        def scores(u, buf, diag=False, h=h, c=c, tile_of=tile_of):
            s_ref, st_ref = bufs[buf]
            if diag:
                j, var = jd, 2
                kpos = (jd * tk + lax.broadcasted_iota(jnp.int32, (tq, tk), 1)).astype(F32)
                qp = (q0 + lax.broadcasted_iota(jnp.int32, (tq, tk), 0)).astype(F32)
                bias = c * jnp.abs(qp - kpos)
            else:
                j = tile_of(u)
                after = j > jd
                var = jnp.where(after, 0, 1)
                rowoff = jnp.where(after, c, -c) * ((j * tk).astype(F32) - qpos)
            k0 = pl.multiple_of(j * tk, tk)
            for mp in range(2):
                hm = 2 * h + mp
                kt = kt_ref[0, hm * DIFF_KROWS:(hm + 1) * DIFF_KROWS, pl.ds(k0, tk)]
                s = _dot(qv_ref[var, hm], kt)
                if diag:
                    s = s - bias
                s_ref[mp] = s
                mx = s[:, 0:LANES]
                for g in range(1, n_groups):
                    mx = jnp.maximum(mx, s[:, g * LANES:(g + 1) * LANES])
                mrow = jnp.broadcast_to(jnp.max(mx, axis=-1, keepdims=True), (tq, LANES))
                m_old = m_ref[hm]
                if diag:
                    m_new = jnp.maximum(m_old, mrow)
                    shift = m_new
                else:
                    m_new = jnp.maximum(m_old, mrow - rowoff)
                    shift = m_new + rowoff
                st_ref[mp, 0] = shift
                st_ref[mp, 1] = jnp.exp2(m_old - m_new)
                m_ref[hm] = m_new

        def values(u, buf, h=h, tile_of=tile_of):
            s_ref, st_ref = bufs[buf]
            k0 = pl.multiple_of(tile_of(u) * tk, tk)
            v2 = v_ref[0, pl.ds(k0, tk), (h // 2) * LANES:(h // 2 + 1) * LANES]
            for mp in range(2):
                hm = 2 * h + mp
                shift = st_ref[mp, 0]
                alpha = st_ref[mp, 1]
                e = jnp.exp2(s_ref[mp] - jnp.concatenate([shift] * n_groups, axis=1))
                lp = e[:, 0:LANES]
                for g in range(1, n_groups):
                    lp = lp + e[:, g * LANES:(g + 1) * LANES]
                l_ref[hm] = alpha * l_ref[hm] + lp
                acc_ref[hm] = alpha * acc_ref[hm] + _dot(e.astype(BF16), v2)

        scores(0, 0, diag=True)

        def pair(p, carry, scores=scores, values=values):
            u = 2 * p
            scores(u + 1, 1)
            values(u, 0)
            scores(u + 2, 0)
            values(u + 1, 1)
            return carry

        n_pairs = n_off // 2
        lax.fori_loop(0, n_pairs, pair, 0)
        u_last = 2 * n_pairs

        @pl.when(n_off % 2 == 1)
        def _(scores=scores, values=values, u_last=u_last):
            scores(u_last + 1, 1)
            values(u_last, 0)
            values(u_last + 1, 1)

        @pl.when(n_off % 2 == 0)
        def _(values=values, u_last=u_last):
            values(u_last, 0)

    lp = lam_ref[...]
    lam = (jnp.exp(jnp.sum(lp[0:1] * lp[1:2], axis=-1, keepdims=True))
           - jnp.exp(jnp.sum(lp[2:3] * lp[3:4], axis=-1, keepdims=True)) + lam_init)
    lane = lax.broadcasted_iota(jnp.int32, (tq, LANES), 1)
    left = lane < HEAD_DIM
    ys = []
    for h in range(DIFF_HEADS):
        o1 = acc_ref[2 * h] / jnp.sum(l_ref[2 * h], axis=-1, keepdims=True)
        o2 = acc_ref[2 * h + 1] / jnp.sum(l_ref[2 * h + 1], axis=-1, keepdims=True)
        od = o1 - lam * o2
        mine = left if h % 2 == 0 else jnp.logical_not(left)
        ms = jnp.sum(jnp.where(mine, od * od, 0.0), axis=-1, keepdims=True) * (1.0 / HEAD_DIM)
        ys.append(od * lax.rsqrt(ms + EPS) * g_ref[...] * (1.0 - lam_init))
    for p in range(DIFF_HEADS // 2):
        o_ref[0, :, p * LANES:(p + 1) * LANES] = jnp.where(left, ys[2 * p], ys[2 * p + 1]).astype(BF16)


def _diff_attention(q, kt, v, diff_lambda, diff_g, lam_init, *, tq=256):
    B, S, _ = q.shape
    tk = DIFF_TK
    tq = min(tq, S)
    assert S % tk == 0 and tk % tq == 0
    n_maps = 2 * DIFF_HEADS
    g2 = jnp.concatenate([diff_g, diff_g]).reshape(1, LANES).astype(F32)
    grid_spec = pltpu.PrefetchScalarGridSpec(
        num_scalar_prefetch=1,
        grid=(B, S // tq),
        in_specs=[pl.BlockSpec((1, tq, DIFF_WIDTH), lambda b, i, r: (b, i, 0)),
                  pl.BlockSpec((1, DIFF_KT_ROWS, S), lambda b, i, r: (b, 0, 0)),
                  pl.BlockSpec((1, S, DIFF_WIDTH), lambda b, i, r: (b, 0, 0)),
                  pl.BlockSpec((4, DIFF_QK_DIM), lambda b, i, r: (0, 0)),
                  pl.BlockSpec((1, LANES), lambda b, i, r: (0, 0))],
        out_specs=pl.BlockSpec((1, tq, DIFF_WIDTH), lambda b, i, r: (b, i, 0)),
        scratch_shapes=[pltpu.VMEM((3, n_maps, tq, DIFF_KROWS), BF16),
                        pltpu.VMEM((n_maps, tq, LANES), F32),
                        pltpu.VMEM((n_maps, tq, LANES), F32),
                        pltpu.VMEM((n_maps, tq, LANES), F32),
                        pltpu.VMEM((2, tq, tk), F32), pltpu.VMEM((2, 2, tq, LANES), F32),
                        pltpu.VMEM((2, tq, tk), F32), pltpu.VMEM((2, 2, tq, LANES), F32)])
    return pl.pallas_call(
        functools.partial(_diff_body, tq=tq, tk=tk, lam_init=lam_init),
        grid_spec=grid_spec,
        out_shape=jax.ShapeDtypeStruct((B, S, DIFF_WIDTH), BF16),
        compiler_params=pltpu.CompilerParams(
            dimension_semantics=("parallel", "arbitrary"), vmem_limit_bytes=VMEM_LIMIT),
        name="diff_attn",
    )(_diff_ranges(q, kt, tq, tk), q, kt, v, diff_lambda.astype(F32), g2)


def _l2n64(x):
    return x * lax.rsqrt(_seg_sum64(x * x) + EPS)


def _dnprep_body(x_ref, xp_ref, xn_ref, w_ref, ba_ref, bat_ref, ab_ref, abt_ref,
                 q_ref, k_ref, v_ref, bg_ref, gt_ref, *, ts, n_tiles):
    i = pl.program_id(1)
    H = DN_HEADS
    x = x_ref[0]
    xp = jnp.where(i > 0, xp_ref[0], 0.0)
    xn = jnp.where(i < n_tiles - 1, xn_ref[0], 0.0)
    xx = jnp.concatenate([xp, x, xn], axis=0)
    half = CONV_W // 2
    y = jnp.zeros_like(x)
    for j in range(CONV_W):
        y = y + w_ref[j:j + 1, :] * xx[8 - half + j:8 - half + j + ts, :]
    y = _silu(y)
    q_ref[0] = _l2n64(y[:, :DN_WIDTH]) * (HEAD_DIM ** -0.5)
    k_ref[0] = _l2n64(y[:, DN_WIDTH:2 * DN_WIDTH])
    v_ref[0] = y[:, 2 * DN_WIDTH:]

    def gates(raw, a_log, dt_bias):
        beta = 1.0 / (1.0 + jnp.exp(-raw))
        z = raw + dt_bias
        softplus = jnp.maximum(z, 0.0) + jnp.log(1.0 + jnp.exp(-jnp.abs(z)))
        return beta, -jnp.exp(a_log) * softplus

    ba = ba_ref[0]
    beta, g = gates(ba, ab_ref[0:1, :], ab_ref[1:2, :])
    lane = lax.broadcasted_iota(jnp.int32, ba.shape, 1)
    r = lax.broadcasted_iota(jnp.int32, ba.shape, 0) % DN_CHUNK
    bwd = jnp.logical_and(lane >= 3 * H, lane < 4 * H)
    gc = g
    for d in (1, 2, 4, 8, 16, 32):
        f = jnp.where(r >= d, pltpu.roll(gc, d, 0), 0.0)
        b = jnp.where(r < DN_CHUNK - d, pltpu.roll(gc, ts - d, 0), 0.0)
        gc = gc + jnp.where(bwd, b, f)
    bg_ref[0] = jnp.where(lane < 2 * H, beta, gc)
    bat = bat_ref[0]
    _, gt = gates(bat, abt_ref[:, 0:1], abt_ref[:, 1:2])
    row = lax.broadcasted_iota(jnp.int32, bat.shape, 0)
    c = lax.broadcasted_iota(jnp.int32, bat.shape, 1) % DN_CHUNK
    bwd_t = row >= 3 * H
    for d in (1, 2, 4, 8, 16, 32):
        f = jnp.where(c >= d, pltpu.roll(gt, d, 1), 0.0)
        b = jnp.where(c < DN_CHUNK - d, pltpu.roll(gt, ts - d, 1), 0.0)
        gt = gt + jnp.where(bwd_t, b, f)
    gt_ref[0] = gt


def _dn_prep(dn_raw, ba, bat, conv_w, a_log, dt_bias, *, ts=256):
    B, S, C = dn_raw.shape
    ts = min(ts, S)
    nt = S // ts
    n_ba = 4 * DN_HEADS
    hb = ts // 8
    zeros = jnp.zeros((2 * DN_HEADS,), F32)
    al = jnp.concatenate([zeros, a_log.reshape(-1).astype(F32)])
    db = jnp.concatenate([zeros, dt_bias.reshape(-1).astype(F32)])
    ab = jnp.pad(jnp.stack([al, db]), ((0, 0), (0, BA_PAD - n_ba)))
    abt = jnp.stack([al, db], axis=1)
    row = lambda n: pl.BlockSpec((1, ts, n), lambda b, i: (b, i, 0))
    return pl.pallas_call(
        functools.partial(_dnprep_body, ts=ts, n_tiles=nt),
        grid=(B, nt),
        in_specs=[row(C),
                  pl.BlockSpec((1, 8, C), lambda b, i: (b, jnp.maximum(i * hb - 1, 0), 0)),
                  pl.BlockSpec((1, 8, C), lambda b, i: (b, jnp.minimum((i + 1) * hb, S // 8 - 1), 0)),
                  _resident((CONV_W, C)), row(BA_PAD),
                  pl.BlockSpec((1, n_ba, ts), lambda b, i: (b, 0, i)),
                  _resident((2, BA_PAD)), _resident((n_ba, 2))],
        out_specs=[row(DN_WIDTH), row(DN_WIDTH), row(DN_WIDTH), row(BA_PAD),
                   pl.BlockSpec((1, n_ba, ts), lambda b, i: (b, 0, i))],
        out_shape=[jax.ShapeDtypeStruct((B, S, DN_WIDTH), F32)] * 3
        + [jax.ShapeDtypeStruct((B, S, BA_PAD), F32), jax.ShapeDtypeStruct((B, n_ba, S), F32)],
        compiler_params=pltpu.CompilerParams(dimension_semantics=("parallel", "parallel")),
        name="dn_prep",
    )(dn_raw, dn_raw, dn_raw, conv_w.astype(F32), ba, bat, ab, abt)


def _bd_rows(y, left):
    return jnp.concatenate([jnp.where(left, y, 0.0), jnp.where(left, 0.0, y)], axis=0).astype(BF16)


def _dnscan_body(qf_ref, kf_ref, vf_ref, bgf_ref, gtf_ref, qb_ref, kb_ref, vb_ref, bgb_ref, gtb_ref,
                 of_ref, ob_ref, st_ref, *, G):
    C = DN_CHUNK
    H = DN_HEADS
    P = H // 2

    @pl.when(pl.program_id(1) == 0)
    def _():
        st_ref[...] = jnp.zeros(st_ref.shape, F32)

    lane1 = lax.broadcasted_iota(jnp.int32, (C, LANES), 1)
    row1 = lax.broadcasted_iota(jnp.int32, (C, LANES), 0)
    left1 = lane1 < C
    s_idx = lane1 % C
    eye = (row1 == s_idx).astype(F32)
    lane2 = lax.broadcasted_iota(jnp.int32, (C, 2 * LANES), 1)
    left2 = (lane2 % LANES) < C
    rr = lax.broadcasted_iota(jnp.int32, (LANES, LANES), 0)
    cc = lax.broadcasted_iota(jnp.int32, (LANES, LANES), 1)
    top = rr < C
    bdmask = (rr < C) == (cc < C)
    top_c = lax.broadcasted_iota(jnp.int32, (LANES, C), 0) < C

    dirs = ((0, qf_ref, kf_ref, vf_ref, bgf_ref, gtf_ref, of_ref),
            (1, qb_ref, kb_ref, vb_ref, bgb_ref, gtb_ref, ob_ref))
    chains = []
    for d, q_ref, k_ref, v_ref, bg_ref, gt_ref, _ in dirs:
        tri = (row1 >= s_idx) if d == 0 else (row1 <= s_idx)
        strict = (row1 > s_idx) if d == 0 else (row1 < s_idx)
        last = C - 1 if d == 0 else 0
        for p in range(P):
            sl = slice(p * LANES, (p + 1) * LANES)
            cb = d * H + 2 * p
            cg = 2 * H + d * H + 2 * p
            kT = k_ref[0, :, sl].T
            for c in range(G):
                rs = slice(c * C, (c + 1) * C)
                qc, kc, vc = q_ref[0, rs, sl], k_ref[0, rs, sl], v_ref[0, rs, sl]
                bexp = jnp.where(left1, bg_ref[0, rs, cb:cb + 1], bg_ref[0, rs, cb + 1:cb + 2])
                gcc = jnp.where(left1, bg_ref[0, rs, cg:cg + 1], bg_ref[0, rs, cg + 1:cg + 2])
                g0 = gt_ref[0, cg:cg + 1, rs]
                g1 = gt_ref[0, cg + 1:cg + 2, rs]
                grow = jnp.concatenate([g0, g1], axis=1)
                glast = gcc[last:last + 1, :]
                gl0 = glast[:, 0:1]
                gl1 = glast[:, C:C + 1]
                eg = jnp.exp(gcc)
                kb_ = kc * bexp
                fac = jnp.exp(jnp.minimum(jnp.where(top_c, gl0 - g0, gl1 - g1), 0.0))
                chains.append(dict(
                    d=d, p=p, c=c,
                    rhs=jnp.concatenate([vc * bexp, kb_ * eg], axis=1),
                    qg=(qc * eg).astype(BF16),
                    kq=jnp.concatenate([kb_, qc], axis=0).astype(BF16),
                    Z=_bd_rows(kc, left1),
                    dec=jnp.where(tri, jnp.exp(jnp.minimum(gcc - grow, 0.0)), 0.0),
                    strict=strict,
                    kg=(kT[:, rs] * fac).astype(BF16),
                    glm=jnp.exp(jnp.where(top, gl0, gl1))))
    for ch in chains:
        kkqk = _dot_nt(ch["kq"], ch["Z"])
        ch["N"] = jnp.where(ch["strict"], -kkqk[:C] * ch["dec"], 0.0)
        ch["qk"] = (kkqk[C:] * ch["dec"]).astype(BF16)
    for ch in chains:
        ch["Pm"] = eye + ch["N"]
    for j in range(6):
        for ch in chains:
            nb = _bd_rows(ch["N"], left1)
            if j == 0:
                ch["N"] = _dot(ch["N"].astype(BF16), nb)
            elif j < 5:
                both = _dot(jnp.concatenate([ch["Pm"], ch["N"]], axis=0).astype(BF16), nb)
                ch["Pm"] = ch["Pm"] + both[:C]
                ch["N"] = both[C:]
            else:
                ch["Pm"] = ch["Pm"] + _dot(ch["Pm"].astype(BF16), nb)
    for ch in chains:
        X = _dot(ch["Pm"].astype(BF16), _bd_rows(ch["rhs"], left2))
        ch["u"] = X[:, :LANES]
        ch["wq"] = jnp.concatenate([X[:, LANES:].astype(BF16), ch["qg"]], axis=0)
    by_key = {(ch["d"], ch["p"], ch["c"]): ch for ch in chains}

    for step in range(G):
        live = []
        for d, *_refs, o_ref in dirs:
            c = step if d == 0 else G - 1 - step
            for p in range(P):
                live.append((d * P + p, o_ref, slice(c * C, (c + 1) * C), p, by_key[(d, p, c)]))
        S = [st_ref[sidx] for sidx, *_ in live]
        t1 = [_dot(ch["wq"], S[i].astype(BF16)) for i, (*_, ch) in enumerate(live)]
        vn = [ch["u"] - t1[i][:C] for i, (*_, ch) in enumerate(live)]
        for i, (sidx, o_ref, rs, p, ch) in enumerate(live):
            o_ref[0, rs, p * LANES:(p + 1) * LANES] = t1[i][C:] + _dot(ch["qk"], _bd_rows(vn[i], left1))
        for i, (sidx, o_ref, rs, p, ch) in enumerate(live):
            upd = _dot(ch["kg"], vn[i].astype(BF16))
            st_ref[sidx] = S[i] * ch["glm"] + jnp.where(bdmask, upd, 0.0)


def _dn_scan(q, k, v, bg, gt, *, G=4):
    B, S, _ = q.shape
    G = min(G, S // DN_CHUNK)
    R = G * DN_CHUNK
    nb = S // R
    n_ba = 4 * DN_HEADS
    P = DN_HEADS // 2
    fw = lambda n: pl.BlockSpec((1, R, n), lambda b, i: (b, i, 0))
    bw = lambda n: pl.BlockSpec((1, R, n), lambda b, i: (b, nb - 1 - i, 0))
    fwt = pl.BlockSpec((1, n_ba, R), lambda b, i: (b, 0, i))
    bwt = pl.BlockSpec((1, n_ba, R), lambda b, i: (b, 0, nb - 1 - i))
    return pl.pallas_call(
        functools.partial(_dnscan_body, G=G),
        grid=(B, nb),
        in_specs=[fw(DN_WIDTH), fw(DN_WIDTH), fw(DN_WIDTH), fw(BA_PAD), fwt,
                  bw(DN_WIDTH), bw(DN_WIDTH), bw(DN_WIDTH), bw(BA_PAD), bwt],
        out_specs=[fw(DN_WIDTH), bw(DN_WIDTH)],
        out_shape=[jax.ShapeDtypeStruct((B, S, DN_WIDTH), F32)] * 2,
        scratch_shapes=[pltpu.VMEM((2 * P, LANES, LANES), F32)],
        compiler_params=pltpu.CompilerParams(dimension_semantics=("parallel", "arbitrary")),
        name="dn_scan",
    )(q, k, v, bg, gt, q, k, v, bg, gt)


def _outproj_body(x_ref, ow_ref, od_ref, of_ref, ob_ref, z_ref, g_ref, w_ref, o_ref, cat_ref):
    oc = of_ref[0] + ob_ref[0]
    on = oc * lax.rsqrt(_seg_sum64(oc * oc) * (1.0 / HEAD_DIM) + EPS) * g_ref[...] * _silu(z_ref[0])
    cat_ref[:, :WIN_WIDTH] = ow_ref[0]
    cat_ref[:, WIN_WIDTH:WIN_WIDTH + DIFF_WIDTH] = od_ref[0]
    cat_ref[:, WIN_WIDTH + DIFF_WIDTH:] = on.astype(BF16)
    o_ref[0] = x_ref[0] + _dot(cat_ref[...], w_ref[...])


def _outproj(x, o_win, o_diff, o_f, o_b, z, dn_g, w_out, *, tm=512):
    B, S, _ = x.shape
    tm = min(tm, S)
    row = lambda n: pl.BlockSpec((1, tm, n), lambda b, i: (b, i, 0))
    g = jnp.tile(dn_g.astype(F32), DN_HEADS).reshape(1, DN_WIDTH)
    return pl.pallas_call(
        _outproj_body,
        grid=(B, S // tm),
        in_specs=[row(D_MODEL), row(WIN_WIDTH), row(DIFF_WIDTH), row(DN_WIDTH), row(DN_WIDTH),
                  row(DN_WIDTH), _resident((1, DN_WIDTH)), _resident((MIX_WIDTH, D_MODEL))],
        out_specs=row(D_MODEL),
        out_shape=jax.ShapeDtypeStruct((B, S, D_MODEL), F32),
        scratch_shapes=[pltpu.VMEM((tm, MIX_WIDTH), BF16)],
        compiler_params=pltpu.CompilerParams(
            dimension_semantics=("parallel", "parallel"), vmem_limit_bytes=VMEM_LIMIT),
        name="outproj",
    )(x, o_win, o_diff, o_f, o_b, z, g, w_out.astype(BF16))


def _mixer(x, ln_mix, w_mix_in, conv_w, sink, diff_lam, diff_g, a_log, dt_bias, dn_g, w_mix_out, lam_init):
    wq, wk, wv, dq, dkt, dv, dn_raw, z, ba, bat = _inproj(x, ln_mix, w_mix_in)
    win_slopes = 2.0 ** (-8.0 * jnp.arange(1, WIN_Q_HEADS + 1, dtype=F32) / WIN_Q_HEADS)
    o_win = _win_attention(wq, wk, wv, sink, win_slopes)
    o_diff = _diff_attention(dq, dkt, dv, diff_lam, diff_g, lam_init)
    q, k, v, bg, gt = _dn_prep(dn_raw, ba, bat, conv_w, a_log, dt_bias)
    o_f, o_b = _dn_scan(q, k, v, bg, gt)
    return _outproj(x, o_win, o_diff, o_f, o_b, z, dn_g, w_mix_out)


def kernel(x, ln_ffn1, ffn1_w_in, ffn1_w_out, ln_mix, w_mix_in, conv_w, sink_logits, diff_lambda,
           diff_norm_g, dn_A_log, dn_dt_bias, dn_norm_g, w_mix_out, ln_ffn2, ffn2_w_in, ffn2_w_out,
           ln_final):
    B, S, D = x.shape
    depth = ln_ffn1.shape[0]
    for l in range(depth):
        lam_init = 0.8 - 0.6 * math.exp(-0.3 * l)
        x = _ffn(x.reshape(B * S, D), ln_ffn1[l], ffn1_w_in[l], ffn1_w_out[l]).reshape(B, S, D)
        x = _mixer(x, ln_mix[l], w_mix_in[l], conv_w[l], sink_logits[l], diff_lambda[l], diff_norm_g[l],
                   dn_A_log[l], dn_dt_bias[l], dn_norm_g[l], w_mix_out[l], lam_init)
        g_final = ln_final if l == depth - 1 else None
        x = _ffn(x.reshape(B * S, D), ln_ffn2[l], ffn2_w_in[l], ffn2_w_out[l], g_final).reshape(B, S, D)
    return x
```

```python
import functools
import math

import jax
import jax.numpy as jnp
from jax import lax
from jax.experimental import pallas as pl
from jax.experimental.pallas import tpu as pltpu

F32 = jnp.float32
BF16 = jnp.bfloat16

D_MODEL = 1024
HEAD_DIM = 64
EPS = 1e-6
WIN_Q_HEADS = 6
WIN_KV_HEADS = 2
WIN_GROUP = WIN_Q_HEADS // WIN_KV_HEADS
WINDOW = 128
DIFF_HEADS = 4
DIFF_QK_DIM = HEAD_DIM // 2
DN_HEADS = 6
DN_CHUNK = 64
CONV_W = 5
D_FF = 2752
WIN_WIDTH = WIN_Q_HEADS * HEAD_DIM
WIN_KV_WIDTH = WIN_KV_HEADS * HEAD_DIM
DIFF_WIDTH = DIFF_HEADS * HEAD_DIM
DN_WIDTH = DN_HEADS * HEAD_DIM
DN_QKV = 3 * DN_WIDTH
MIX_WIDTH = WIN_WIDTH + DIFF_WIDTH + DN_WIDTH
OFF_WQ = 0
OFF_WK = OFF_WQ + WIN_WIDTH
OFF_WV = OFF_WK + WIN_KV_WIDTH
OFF_DQ = OFF_WV + WIN_KV_WIDTH
OFF_DK = OFF_DQ + DIFF_WIDTH
OFF_DV = OFF_DK + DIFF_WIDTH
OFF_DN = OFF_DV + DIFF_WIDTH
OFF_Z = OFF_DN + DN_QKV
OFF_BA = OFF_Z + DN_WIDTH
MIX_IN = OFF_BA + 4 * DN_HEADS

LANES = 128
FF_CHUNK = 256
D_FF_PAD = -(-D_FF // FF_CHUNK) * FF_CHUNK
VMEM_LIMIT = 56 * 1024 * 1024
BF16_ROWS = 16
LOG2E = math.log2(math.e)
DIFF_TK = 512
DIFF_KROWS = 48
DIFF_BIAS_ROWS = 3
DIFF_KT_ROWS = 2 * DIFF_HEADS * DIFF_KROWS
DIFF_SKIP_LOG2 = 150.0
DIFF_SLOPES = tuple(2.0 ** (-8.0 * (i + 1) / DIFF_HEADS) for i in range(DIFF_HEADS))


def _rms(x, g):
    return x * lax.rsqrt(jnp.mean(x * x, axis=-1, keepdims=True) + EPS) * g


def _silu(x):
    h = 0.5 * x
    return h + h * jnp.tanh(h)


def _seg_sum64(x):
    R, width = x.shape
    n = width // LANES
    rows = jnp.concatenate([x[:, c * LANES:(c + 1) * LANES] for c in range(n)], axis=0)
    r = lax.broadcasted_iota(jnp.int32, (LANES, LANES), 0) < HEAD_DIM
    c = lax.broadcasted_iota(jnp.int32, (LANES, LANES), 1) < HEAD_DIM
    ones_bd = (r == c).astype(F32).astype(BF16)
    hi = rows.astype(BF16)
    r1 = rows - hi.astype(F32)
    mid = r1.astype(BF16)
    lo = (r1 - mid.astype(F32)).astype(BF16)
    ss = _dot(hi, ones_bd) + _dot(mid, ones_bd) + _dot(lo, ones_bd)
    return jnp.concatenate([ss[c * R:(c + 1) * R] for c in range(n)], axis=1)


def _dot(a, b):
    return jnp.dot(a, b, preferred_element_type=F32)


def _dot_nt(a, b):
    return lax.dot_general(a, b, (((1,), (1,)), ((), ())), preferred_element_type=F32)


def _resident(shape):
    nd = len(shape)
    return pl.BlockSpec(shape, lambda *_: (0,) * nd, pipeline_mode=pl.Buffered(1))


def _ffn_body(x_ref, g_ref, wgu_ref, wo_ref, *rest, n_chunks, final):
    if final:
        gf_ref, o_ref, acc_ref = rest
    else:
        o_ref, acc_ref = rest
    x = x_ref[...]
    h = _rms(x, g_ref[...]).astype(BF16)
    for c in range(n_chunks):
        gu = _dot(h, wgu_ref[:, c * 2 * FF_CHUNK:(c + 1) * 2 * FF_CHUNK])
        a = (_silu(gu[:, :FF_CHUNK]) * gu[:, FF_CHUNK:]).astype(BF16)
        part = _dot(a, wo_ref[c * FF_CHUNK:(c + 1) * FF_CHUNK, :])
        if c == 0:
            acc_ref[...] = part
        else:
            acc_ref[...] += part
    y = x + 0.5 * acc_ref[...]
    if final:
        y = _rms(y, gf_ref[...])
    o_ref[...] = y


def _prep_ffn_weights(w_in, w_out):
    n = D_FF_PAD // FF_CHUNK
    pad = D_FF_PAD - D_FF
    wg = jnp.pad(w_in[:, :D_FF], ((0, 0), (0, pad))).reshape(D_MODEL, n, FF_CHUNK)
    wu = jnp.pad(w_in[:, D_FF:], ((0, 0), (0, pad))).reshape(D_MODEL, n, FF_CHUNK)
    wgu = jnp.concatenate([wg, wu], axis=2).reshape(D_MODEL, n * 2 * FF_CHUNK).astype(BF16)
    wo = jnp.pad(w_out, ((0, pad), (0, 0))).astype(BF16)
    return wgu, wo


def _ffn(x2d, g, w_in, w_out, g_final=None, *, tm=512):
    T = x2d.shape[0]
    tm = min(tm, T)
    n_chunks = D_FF_PAD // FF_CHUNK
    wgu, wo = _prep_ffn_weights(w_in, w_out)
    final = g_final is not None
    in_specs = [
        pl.BlockSpec((tm, D_MODEL), lambda i: (i, 0)),
        _resident((1, D_MODEL)),
        _resident(wgu.shape),
        _resident(wo.shape),
    ]
    args = [x2d, g.reshape(1, D_MODEL), wgu, wo]
    if final:
        in_specs.append(_resident((1, D_MODEL)))
        args.append(g_final.reshape(1, D_MODEL))
    return pl.pallas_call(
        functools.partial(_ffn_body, n_chunks=n_chunks, final=final),
        grid=(T // tm,),
        in_specs=in_specs,
        out_specs=pl.BlockSpec((tm, D_MODEL), lambda i: (i, 0)),
        out_shape=jax.ShapeDtypeStruct((T, D_MODEL), F32),
        scratch_shapes=[pltpu.VMEM((tm, D_MODEL), F32)],
        compiler_params=pltpu.CompilerParams(
            dimension_semantics=("parallel",), vmem_limit_bytes=VMEM_LIMIT),
        name="ffn_final" if final else "ffn",
    )(*args)


BA_PAD = LANES


def _inproj_body(x_ref, g_ref, w_ref, wkt_ref, wbat_ref,
                 wq_ref, wk_ref, wv_ref, dq_ref, dkt_ref, dv_ref, dn_ref, z_ref, ba_ref, bat_ref):
    h = _rms(x_ref[0], g_ref[...]).astype(BF16)
    wq_ref[0] = (_dot(h, w_ref[:, OFF_WQ:OFF_WK]) * (HEAD_DIM ** -0.5 * LOG2E)).astype(BF16)
    wk_ref[0] = _dot(h, w_ref[:, OFF_WK:OFF_WV]).astype(BF16)
    wv_ref[0] = _dot(h, w_ref[:, OFF_WV:OFF_DQ]).astype(BF16)
    dq_ref[0] = (_dot(h, w_ref[:, OFF_DQ:OFF_DK]) * (DIFF_QK_DIM ** -0.5 * LOG2E)).astype(BF16)
    kt = _dot_nt(wkt_ref[...], h)
    tm = kt.shape[1]
    extra_rows = DIFF_KROWS - DIFF_QK_DIM
    pos = pl.program_id(1) * tm + lax.broadcasted_iota(jnp.int32, (extra_rows, tm), 1)
    kr = (pos % DIFF_TK).astype(F32)
    rowi = lax.broadcasted_iota(jnp.int32, (extra_rows, tm), 0)
    for hd in range(DIFF_HEADS):
        bias = (DIFF_SLOPES[hd] * LOG2E) * kr
        hi = bias.astype(BF16).astype(F32)
        mid = (bias - hi).astype(BF16).astype(F32)
        lo = bias - hi - mid
        extra = jnp.where(rowi == 0, hi, jnp.where(rowi == 1, mid, jnp.where(rowi == 2, lo, 0.0))).astype(BF16)
        for mp in range(2):
            hm = 2 * hd + mp
            dkt_ref[0, hm * DIFF_KROWS:hm * DIFF_KROWS + DIFF_QK_DIM, :] = (
                kt[hm * DIFF_QK_DIM:(hm + 1) * DIFF_QK_DIM, :].astype(BF16))
            dkt_ref[0, hm * DIFF_KROWS + DIFF_QK_DIM:(hm + 1) * DIFF_KROWS, :] = extra
    dv_ref[0] = _dot(h, w_ref[:, OFF_DV:OFF_DN]).astype(BF16)
    dn_ref[0] = _dot(h, w_ref[:, OFF_DN:OFF_Z])
    z_ref[0] = _dot(h, w_ref[:, OFF_Z:OFF_BA])
    ba_ref[0] = _dot(h, w_ref[:, OFF_BA:OFF_BA + BA_PAD])
    bat_ref[0] = _dot_nt(wbat_ref[...], h)


def _inproj(x, g, w_in, *, tm=512):
    B, S, _ = x.shape
    tm = min(tm, S)
    n_ba = 4 * DN_HEADS
    w = jnp.pad(w_in, ((0, 0), (0, OFF_BA + BA_PAD - MIX_IN))).astype(BF16)
    wkt = w_in[:, OFF_DK:OFF_DV].T.astype(BF16)
    wbat = w_in[:, OFF_BA:MIX_IN].T.astype(BF16)
    row = lambda n: pl.BlockSpec((1, tm, n), lambda b, i: (b, i, 0))
    col = lambda n: pl.BlockSpec((1, n, tm), lambda b, i: (b, 0, i))
    out_shapes = [
        jax.ShapeDtypeStruct((B, S, WIN_WIDTH), BF16),
        jax.ShapeDtypeStruct((B, S, WIN_KV_WIDTH), BF16),
        jax.ShapeDtypeStruct((B, S, WIN_KV_WIDTH), BF16),
        jax.ShapeDtypeStruct((B, S, DIFF_WIDTH), BF16),
        jax.ShapeDtypeStruct((B, DIFF_KT_ROWS, S), BF16),
        jax.ShapeDtypeStruct((B, S, DIFF_WIDTH), BF16),
        jax.ShapeDtypeStruct((B, S, DN_QKV), F32),
        jax.ShapeDtypeStruct((B, S, DN_WIDTH), F32),
        jax.ShapeDtypeStruct((B, S, BA_PAD), F32),
        jax.ShapeDtypeStruct((B, n_ba, S), F32),
    ]
    out_specs = [row(WIN_WIDTH), row(WIN_KV_WIDTH), row(WIN_KV_WIDTH), row(DIFF_WIDTH),
                 col(DIFF_KT_ROWS), row(DIFF_WIDTH), row(DN_QKV), row(DN_WIDTH), row(BA_PAD), col(n_ba)]
    return pl.pallas_call(
        _inproj_body,
        grid=(B, S // tm),
        in_specs=[row(D_MODEL), _resident((1, D_MODEL)), _resident(w.shape),
                  _resident(wkt.shape), _resident(wbat.shape)],
        out_specs=out_specs,
        out_shape=out_shapes,
        compiler_params=pltpu.CompilerParams(
            dimension_semantics=("parallel", "parallel"), vmem_limit_bytes=VMEM_LIMIT),
        name="inproj",
    )(x, g.reshape(1, D_MODEL), w, wkt, wbat)


def _win_body(q_ref, kp_ref, kc_ref, kn_ref, vp_ref, vc_ref, vn_ref, sink_ref, slope_ref, o_ref,
              *, n_steps, nq):
    i = pl.program_id(1)
    W = WINDOW
    KW = 3 * W
    kw = jnp.concatenate([kp_ref[0], kc_ref[0], kn_ref[0]], axis=0).astype(F32)
    vw = jnp.concatenate([vp_ref[0], vc_ref[0], vn_ref[0]], axis=0).astype(F32)
    kw_sw = pltpu.roll(kw, HEAD_DIM, 1)
    vw_sw = pltpu.roll(vw, HEAD_DIM, 1)
    left = lax.broadcasted_iota(jnp.int32, (KW, LANES), 1) < HEAD_DIM
    qi = lax.broadcasted_iota(jnp.int32, (W, KW), 0)
    ki = lax.broadcasted_iota(jnp.int32, (W, KW), 1)
    dist = jnp.abs(ki - W - qi)
    band_bias = [jnp.where(dist <= W, (-LOG2E) * slope_ref[:, hq:hq + 1] * dist.astype(F32), -1e30)
                 for hq in range(WIN_Q_HEADS)]
    edge_lo = jnp.where(jnp.logical_or(ki >= W, i > 0), 0.0, -1e30)
    edge_hi = jnp.where(jnp.logical_or(ki < 2 * W, i < n_steps - 1), 0.0, -1e30)
    n_pairs = WIN_Q_HEADS // 2
    blocks = []
    for a in range(nq):
        rows = slice(a * W, a * W + KW)

        def stack(first, second, rows=rows):
            return jnp.concatenate([jnp.where(left, first[rows], 0.0),
                                    jnp.where(left, 0.0, second[rows])], axis=0).astype(BF16)

        zk = (stack(kw, kw_sw), stack(kw, kw), stack(kw_sw, kw))
        zv = (stack(vw, vw_sw), stack(vw, vw), stack(vw_sw, vw))
        edge = None
        if a == 0:
            edge = edge_lo
        if a == nq - 1:
            edge = edge_hi if edge is None else edge + edge_hi
        q = q_ref[0, a * W:(a + 1) * W, :]
        s2 = [_dot_nt(q[:, p * LANES:(p + 1) * LANES], zk[p]) for p in range(n_pairs)]
        blocks.append((a, edge, zv, s2))
    probs = {}
    for a, edge, zv, s2 in blocks:
        for hq in range(WIN_Q_HEADS):
            p, t = divmod(hq, 2)
            s = s2[p][:, t * KW:(t + 1) * KW] + band_bias[hq]
            if edge is not None:
                s = s + edge
            sink = LOG2E * sink_ref[:, hq:hq + 1]
            m = jnp.maximum(jnp.max(s, axis=-1, keepdims=True), sink)
            e = jnp.exp2(s - m)
            denom = jnp.sum(e, axis=-1, keepdims=True) + jnp.exp2(sink - m)
            probs[(a, hq)] = (e * (1.0 / denom)).astype(BF16)
    for a, edge, zv, s2 in blocks:
        for p in range(n_pairs):
            pp = jnp.concatenate([probs[(a, 2 * p)], probs[(a, 2 * p + 1)]], axis=1)
            o_ref[0, a * W:(a + 1) * W, p * LANES:(p + 1) * LANES] = _dot(pp, zv[p]).astype(BF16)


def _win_attention(q, k, v, sink, slopes, *, nq=2):
    B, S, _ = q.shape
    W = WINDOW
    nq = min(nq, S // W)
    ns = S // (nq * W)
    nb = S // W
    qspec = pl.BlockSpec((1, nq * W, WIN_WIDTH), lambda b, i: (b, i, 0))
    prev = pl.BlockSpec((1, W, WIN_KV_WIDTH), lambda b, i: (b, jnp.maximum(i * nq - 1, 0), 0))
    cur = pl.BlockSpec((1, nq * W, WIN_KV_WIDTH), lambda b, i: (b, i, 0))
    nxt = pl.BlockSpec((1, W, WIN_KV_WIDTH), lambda b, i: (b, jnp.minimum((i + 1) * nq, nb - 1), 0))
    return pl.pallas_call(
        functools.partial(_win_body, n_steps=ns, nq=nq),
        grid=(B, ns),
        in_specs=[qspec, prev, cur, nxt, prev, cur, nxt,
                  _resident((1, WIN_Q_HEADS)), _resident((1, WIN_Q_HEADS))],
        out_specs=pl.BlockSpec((1, nq * W, WIN_WIDTH), lambda b, i: (b, i, 0)),
        out_shape=jax.ShapeDtypeStruct((B, S, WIN_WIDTH), BF16),
        compiler_params=pltpu.CompilerParams(dimension_semantics=("parallel", "parallel")),
        name="win_attn",
    )(q, k, k, k, v, v, v, sink.reshape(1, WIN_Q_HEADS).astype(F32), slopes.reshape(1, WIN_Q_HEADS))


def _diff_ranges(q, kt, tq, tk):
    B, S, _ = q.shape
    nq, nk, n_maps = S // tq, S // tk, 2 * DIFF_HEADS
    qf = q.astype(F32).reshape(B, nq, tq, n_maps, DIFF_QK_DIM)
    qn = jnp.sqrt(jnp.max(jnp.sum(qf * qf, axis=-1), axis=2))
    kf = kt.astype(F32).reshape(B, n_maps, DIFF_KROWS, S)[:, :, :DIFF_QK_DIM]
    kn_pos = jnp.sqrt(jnp.sum(kf * kf, axis=2))
    kn = jnp.max(kn_pos.reshape(B, n_maps, nk, tk), axis=-1)
    kown = jnp.max(kn_pos.reshape(B, n_maps, nq, tq), axis=-1)
    reach = jnp.swapaxes(qn, 1, 2)[..., None] * (kn[:, :, None, :] + kown[..., None]) * (1.0 + 1e-3)
    reach = jnp.max(reach.reshape(B, DIFF_HEADS, 2, nq, nk), axis=2)
    q0 = jnp.arange(nq, dtype=jnp.int32) * tq
    k0 = jnp.arange(nk, dtype=jnp.int32) * tk
    dmin = jnp.maximum(jnp.maximum(k0[None, :] - (q0[:, None] + tq - 1), q0[:, None] - (k0[None, :] + tk - 1)), 0)
    c = jnp.asarray(DIFF_SLOPES, F32) * LOG2E
    need = reach - c[None, :, None, None] * dmin.astype(F32)[None, None] > -DIFF_SKIP_LOG2
    idx = jnp.arange(nk, dtype=jnp.int32)
    need = jnp.logical_or(need, (idx[None, :] == (q0 // tk)[:, None])[None, None])
    lo = jnp.min(jnp.where(need, idx, nk), axis=-1)
    hi = jnp.max(jnp.where(need, idx, -1), axis=-1)
    return jnp.stack([lo, hi], axis=-1).transpose(0, 2, 1, 3).reshape(-1).astype(jnp.int32)


def _diff_body(rng_ref, q_ref, kt_ref, v_ref, lam_ref, g_ref, o_ref, qv_ref, m_ref, l_ref, acc_ref,
               s0_ref, st0_ref, s1_ref, st1_ref, *, tq, tk, lam_init):
    n_maps = 2 * DIFF_HEADS
    n_groups = tk // LANES
    q0 = pl.program_id(1) * tq
    jd = q0 // tk
    q = q_ref[0]
    lane_x = lax.broadcasted_iota(jnp.int32, (tq, DIFF_KROWS - DIFF_QK_DIM), 1)
    for var, sign in enumerate((1.0, -1.0, 0.0)):
        extras = jnp.where(lane_x < DIFF_BIAS_ROWS, -sign, 0.0).astype(BF16)
        for hm in range(n_maps):
            qv_ref[var, hm, :, 0:DIFF_QK_DIM] = q[:, hm * DIFF_QK_DIM:(hm + 1) * DIFF_QK_DIM]
            qv_ref[var, hm, :, DIFF_QK_DIM:DIFF_KROWS] = extras
    m_ref[...] = jnp.full(m_ref.shape, -jnp.inf, F32)
    l_ref[...] = jnp.zeros(l_ref.shape, F32)
    acc_ref[...] = jnp.zeros(acc_ref.shape, F32)
    qpos = (q0 + lax.broadcasted_iota(jnp.int32, (tq, LANES), 0)).astype(F32)
    bufs = ((s0_ref, st0_ref), (s1_ref, st1_ref))
    rng_base = (pl.program_id(0) * pl.num_programs(1) + pl.program_id(1)) * (2 * DIFF_HEADS)

    los = [rng_ref[rng_base + 2 * h] for h in range(DIFF_HEADS)]
    n_offs = [rng_ref[rng_base + 2 * h + 1] - los[h] for h in range(DIFF_HEADS)]
    starts = [0]
    for h in range(DIFF_HEADS):
        starts.append(starts[-1] + n_offs[h])
    n_total = starts[DIFF_HEADS]

    def pick(h, vals):
        out = vals[-1]
        for i in range(len(vals) - 2, -1, -1):
            out = jnp.where(h == i, vals[i], out)
        return out

    def locate(u):
        h = sum((u >= starts[i]).astype(jnp.int32) for i in range(1, DIFF_HEADS))
        t = pick(h, los) + u - pick(h, starts[:DIFF_HEADS])
        return h, t + (t >= jd).astype(jnp.int32)

    def scores(h, j, buf, diag_dist=None):
        s_ref, st_ref = bufs[buf]
        if diag_dist is not None:
            var = 2
            bias = (DIFF_SLOPES[h] * LOG2E) * diag_dist
        else:
            c = pick(h, [sl * LOG2E for sl in DIFF_SLOPES])
            after = j > jd
            var = jnp.where(after, 0, 1)
            rowoff = jnp.where(after, c, -c) * ((j * tk).astype(F32) - qpos)
        k0 = pl.multiple_of(j * tk, tk)
        for mp in range(2):
            hm = 2 * h + mp
            r0 = hm * DIFF_KROWS if diag_dist is not None else pl.multiple_of(hm * DIFF_KROWS, BF16_ROWS)
            kt = kt_ref[0, pl.ds(r0, DIFF_KROWS), pl.ds(k0, tk)]
            s = _dot(qv_ref[var, hm], kt)
            if diag_dist is not None:
                s = s - bias
            s_ref[mp] = s
            mx = s[:, 0:LANES]
            for g in range(1, n_groups):
                mx = jnp.maximum(mx, s[:, g * LANES:(g + 1) * LANES])
            mrow = jnp.broadcast_to(jnp.max(mx, axis=-1, keepdims=True), (tq, LANES))
            m_old = m_ref[hm]
            if diag_dist is not None:
                m_new = jnp.maximum(m_old, mrow)
                shift = m_new
            else:
                m_new = jnp.maximum(m_old, mrow - rowoff)
                shift = m_new + rowoff
            st_ref[mp, 0] = shift
            st_ref[mp, 1] = jnp.exp2(m_old - m_new)
            m_ref[hm] = m_new

    def values(h, j, buf, static=False):
        s_ref, st_ref = bufs[buf]
        k0 = pl.multiple_of(j * tk, tk)
        c0 = (h // 2) * LANES if static else pl.multiple_of((h // 2) * LANES, LANES)
        v2 = v_ref[0, pl.ds(k0, tk), pl.ds(c0, LANES)]
        for mp in range(2):
            hm = 2 * h + mp
            shift = st_ref[mp, 0]
            alpha = st_ref[mp, 1]
            e = jnp.exp2(s_ref[mp] - jnp.concatenate([shift] * n_groups, axis=1))
            lp = e[:, 0:LANES]
            for g in range(1, n_groups):
                lp = lp + e[:, g * LANES:(g + 1) * LANES]
            l_ref[hm] = alpha * l_ref[hm] + lp
            acc_ref[hm] = alpha * acc_ref[hm] + _dot(e.astype(BF16), v2)

    kpos = (jd * tk + lax.broadcasted_iota(jnp.int32, (tq, tk), 1)).astype(F32)
    qp = (q0 + lax.broadcasted_iota(jnp.int32, (tq, tk), 0)).astype(F32)
    diag_dist = jnp.abs(qp - kpos)
    scores(0, jd, 0, diag_dist)
    for h in range(DIFF_HEADS):
        if h + 1 < DIFF_HEADS:
            scores(h + 1, jd, (h + 1) % 2, diag_dist)
        values(h, jd, h % 2, static=True)

    def s_at(u, buf):
        h, j = locate(u)
        scores(h, j, buf)

    def v_at(u, buf):
        h, j = locate(u)
        values(h, j, buf)

    @pl.when(n_total > 0)
    def _():
        s_at(0, 0)

    def pair(p, carry):
        u = 2 * p
        s_at(u + 1, 1)
        v_at(u, 0)
        s_at(u + 2, 0)
        v_at(u + 1, 1)
        return carry

    n_pairs = jnp.maximum(n_total - 1, 0) // 2
    lax.fori_loop(0, n_pairs, pair, 0)
    u_last = 2 * n_pairs
    left_over = n_total - u_last

    @pl.when(left_over == 2)
    def _():
        s_at(u_last + 1, 1)
        v_at(u_last, 0)
        v_at(u_last + 1, 1)

    @pl.when(left_over == 1)
    def _():
        v_at(u_last, 0)

    lp = lam_ref[...]
    lam = (jnp.exp(jnp.sum(lp[0:1] * lp[1:2], axis=-1, keepdims=True))
           - jnp.exp(jnp.sum(lp[2:3] * lp[3:4], axis=-1, keepdims=True)) + lam_init)
    lane = lax.broadcasted_iota(jnp.int32, (tq, LANES), 1)
    left = lane < HEAD_DIM
    ys = []
    for h in range(DIFF_HEADS):
        o1 = acc_ref[2 * h] / jnp.sum(l_ref[2 * h], axis=-1, keepdims=True)
        o2 = acc_ref[2 * h + 1] / jnp.sum(l_ref[2 * h + 1], axis=-1, keepdims=True)
        od = o1 - lam * o2
        mine = left if h % 2 == 0 else jnp.logical_not(left)
        ms = jnp.sum(jnp.where(mine, od * od, 0.0), axis=-1, keepdims=True) * (1.0 / HEAD_DIM)
        ys.append(od * lax.rsqrt(ms + EPS) * g_ref[...] * (1.0 - lam_init))
    for p in range(DIFF_HEADS // 2):
        o_ref[0, :, p * LANES:(p + 1) * LANES] = jnp.where(left, ys[2 * p], ys[2 * p + 1]).astype(BF16)


def _diff_attention(q, kt, v, diff_lambda, diff_g, lam_init, *, tq=512):
    B, S, _ = q.shape
    tk = DIFF_TK
    tq = min(tq, S)
    assert S % tk == 0 and tk % tq == 0
    n_maps = 2 * DIFF_HEADS
    g2 = jnp.concatenate([diff_g, diff_g]).reshape(1, LANES).astype(F32)
    grid_spec = pltpu.PrefetchScalarGridSpec(
        num_scalar_prefetch=1,
        grid=(B, S // tq),
        in_specs=[pl.BlockSpec((1, tq, DIFF_WIDTH), lambda b, i, r: (b, i, 0)),
                  pl.BlockSpec((1, DIFF_KT_ROWS, S), lambda b, i, r: (b, 0, 0)),
                  pl.BlockSpec((1, S, DIFF_WIDTH), lambda b, i, r: (b, 0, 0)),
                  pl.BlockSpec((4, DIFF_QK_DIM), lambda b, i, r: (0, 0)),
                  pl.BlockSpec((1, LANES), lambda b, i, r: (0, 0))],
        out_specs=pl.BlockSpec((1, tq, DIFF_WIDTH), lambda b, i, r: (b, i, 0)),
        scratch_shapes=[pltpu.VMEM((3, n_maps, tq, DIFF_KROWS), BF16),
                        pltpu.VMEM((n_maps, tq, LANES), F32),
                        pltpu.VMEM((n_maps, tq, LANES), F32),
                        pltpu.VMEM((n_maps, tq, LANES), F32),
                        pltpu.VMEM((2, tq, tk), F32), pltpu.VMEM((2, 2, tq, LANES), F32),
                        pltpu.VMEM((2, tq, tk), F32), pltpu.VMEM((2, 2, tq, LANES), F32)])
    return pl.pallas_call(
        functools.partial(_diff_body, tq=tq, tk=tk, lam_init=lam_init),
        grid_spec=grid_spec,
        out_shape=jax.ShapeDtypeStruct((B, S, DIFF_WIDTH), BF16),
        compiler_params=pltpu.CompilerParams(
            dimension_semantics=("parallel", "arbitrary"), vmem_limit_bytes=VMEM_LIMIT),
        name="diff_attn",
    )(_diff_ranges(q, kt, tq, tk), q, kt, v, diff_lambda.astype(F32), g2)


def _l2n64(x):
    return x * lax.rsqrt(_seg_sum64(x * x) + EPS)


def _dnprep_body(x_ref, xp_ref, xn_ref, w_ref, ba_ref, bat_ref, ab_ref, abt_ref,
                 q_ref, k_ref, v_ref, bg_ref, gt_ref, *, ts, n_tiles):
    i = pl.program_id(1)
    H = DN_HEADS
    x = x_ref[0]
    xp = jnp.where(i > 0, xp_ref[0], 0.0)
    xn = jnp.where(i < n_tiles - 1, xn_ref[0], 0.0)
    xx = jnp.concatenate([xp, x, xn], axis=0)
    half = CONV_W // 2
    y = jnp.zeros_like(x)
    for j in range(CONV_W):
        y = y + w_ref[j:j + 1, :] * xx[8 - half + j:8 - half + j + ts, :]
    y = _silu(y)
    q_ref[0] = _l2n64(y[:, :DN_WIDTH]) * (HEAD_DIM ** -0.5)
    k_ref[0] = _l2n64(y[:, DN_WIDTH:2 * DN_WIDTH])
    v_ref[0] = y[:, 2 * DN_WIDTH:]

    def gates(raw, a_log, dt_bias):
        beta = 1.0 / (1.0 + jnp.exp(-raw))
        z = raw + dt_bias
        softplus = jnp.maximum(z, 0.0) + jnp.log(1.0 + jnp.exp(-jnp.abs(z)))
        return beta, -jnp.exp(a_log) * softplus

    ba = ba_ref[0]
    beta, g = gates(ba, ab_ref[0:1, :], ab_ref[1:2, :])
    lane = lax.broadcasted_iota(jnp.int32, ba.shape, 1)
    r = lax.broadcasted_iota(jnp.int32, ba.shape, 0) % DN_CHUNK
    bwd = jnp.logical_and(lane >= 3 * H, lane < 4 * H)
    gc = g
    for d in (1, 2, 4, 8, 16, 32):
        f = jnp.where(r >= d, pltpu.roll(gc, d, 0), 0.0)
        b = jnp.where(r < DN_CHUNK - d, pltpu.roll(gc, ts - d, 0), 0.0)
        gc = gc + jnp.where(bwd, b, f)
    bg_ref[0] = jnp.where(lane < 2 * H, beta, gc)
    bat = bat_ref[0]
    _, gt = gates(bat, abt_ref[:, 0:1], abt_ref[:, 1:2])
    row = lax.broadcasted_iota(jnp.int32, bat.shape, 0)
    c = lax.broadcasted_iota(jnp.int32, bat.shape, 1) % DN_CHUNK
    bwd_t = row >= 3 * H
    for d in (1, 2, 4, 8, 16, 32):
        f = jnp.where(c >= d, pltpu.roll(gt, d, 1), 0.0)
        b = jnp.where(c < DN_CHUNK - d, pltpu.roll(gt, ts - d, 1), 0.0)
        gt = gt + jnp.where(bwd_t, b, f)
    gt_ref[0] = gt


def _dn_prep(dn_raw, ba, bat, conv_w, a_log, dt_bias, *, ts=256):
    B, S, C = dn_raw.shape
    ts = min(ts, S)
    nt = S // ts
    n_ba = 4 * DN_HEADS
    hb = ts // 8
    zeros = jnp.zeros((2 * DN_HEADS,), F32)
    al = jnp.concatenate([zeros, a_log.reshape(-1).astype(F32)])
    db = jnp.concatenate([zeros, dt_bias.reshape(-1).astype(F32)])
    ab = jnp.pad(jnp.stack([al, db]), ((0, 0), (0, BA_PAD - n_ba)))
    abt = jnp.stack([al, db], axis=1)
    row = lambda n: pl.BlockSpec((1, ts, n), lambda b, i: (b, i, 0))
    return pl.pallas_call(
        functools.partial(_dnprep_body, ts=ts, n_tiles=nt),
        grid=(B, nt),
        in_specs=[row(C),
                  pl.BlockSpec((1, 8, C), lambda b, i: (b, jnp.maximum(i * hb - 1, 0), 0)),
                  pl.BlockSpec((1, 8, C), lambda b, i: (b, jnp.minimum((i + 1) * hb, S // 8 - 1), 0)),
                  _resident((CONV_W, C)), row(BA_PAD),
                  pl.BlockSpec((1, n_ba, ts), lambda b, i: (b, 0, i)),
                  _resident((2, BA_PAD)), _resident((n_ba, 2))],
        out_specs=[row(DN_WIDTH), row(DN_WIDTH), row(DN_WIDTH), row(BA_PAD),
                   pl.BlockSpec((1, n_ba, ts), lambda b, i: (b, 0, i))],
        out_shape=[jax.ShapeDtypeStruct((B, S, DN_WIDTH), F32)] * 3
        + [jax.ShapeDtypeStruct((B, S, BA_PAD), F32), jax.ShapeDtypeStruct((B, n_ba, S), F32)],
        compiler_params=pltpu.CompilerParams(dimension_semantics=("parallel", "parallel")),
        name="dn_prep",
    )(dn_raw, dn_raw, dn_raw, conv_w.astype(F32), ba, bat, ab, abt)


def _bd_rows(y, left):
    return jnp.concatenate([jnp.where(left, y, 0.0), jnp.where(left, 0.0, y)], axis=0).astype(BF16)


def _dnscan_body(qf_ref, kf_ref, vf_ref, bgf_ref, gtf_ref, qb_ref, kb_ref, vb_ref, bgb_ref, gtb_ref,
                 of_ref, ob_ref, st_ref, *, G):
    C = DN_CHUNK
    H = DN_HEADS
    P = H // 2

    @pl.when(pl.program_id(1) == 0)
    def _():
        st_ref[...] = jnp.zeros(st_ref.shape, F32)

    lane1 = lax.broadcasted_iota(jnp.int32, (C, LANES), 1)
    row1 = lax.broadcasted_iota(jnp.int32, (C, LANES), 0)
    left1 = lane1 < C
    s_idx = lane1 % C
    eye = (row1 == s_idx).astype(F32)
    lane2 = lax.broadcasted_iota(jnp.int32, (C, 2 * LANES), 1)
    left2 = (lane2 % LANES) < C
    rr = lax.broadcasted_iota(jnp.int32, (LANES, LANES), 0)
    cc = lax.broadcasted_iota(jnp.int32, (LANES, LANES), 1)
    top = rr < C
    bdmask = (rr < C) == (cc < C)
    top_c = lax.broadcasted_iota(jnp.int32, (LANES, C), 0) < C

    dirs = ((0, qf_ref, kf_ref, vf_ref, bgf_ref, gtf_ref, of_ref),
            (1, qb_ref, kb_ref, vb_ref, bgb_ref, gtb_ref, ob_ref))
    kT_cache = {}

    def prepare(group):
        chains = []
        for d, p, c in group:
            _, q_ref, k_ref, v_ref, bg_ref, gt_ref, _ = dirs[d]
            tri = (row1 >= s_idx) if d == 0 else (row1 <= s_idx)
            strict = (row1 > s_idx) if d == 0 else (row1 < s_idx)
            last = C - 1 if d == 0 else 0
            sl = slice(p * LANES, (p + 1) * LANES)
            cb = d * H + 2 * p
            cg = 2 * H + d * H + 2 * p
            if (d, p) not in kT_cache:
                kT_cache[(d, p)] = k_ref[0, :, sl].T
            rs = slice(c * C, (c + 1) * C)
            qc, kc, vc = q_ref[0, rs, sl], k_ref[0, rs, sl], v_ref[0, rs, sl]
            bexp = jnp.where(left1, bg_ref[0, rs, cb:cb + 1], bg_ref[0, rs, cb + 1:cb + 2])
            gcc = jnp.where(left1, bg_ref[0, rs, cg:cg + 1], bg_ref[0, rs, cg + 1:cg + 2])
            g0 = gt_ref[0, cg:cg + 1, rs]
            g1 = gt_ref[0, cg + 1:cg + 2, rs]
            grow = jnp.concatenate([g0, g1], axis=1)
            glast = gcc[last:last + 1, :]
            gl0 = glast[:, 0:1]
            gl1 = glast[:, C:C + 1]
            eg = jnp.exp(gcc)
            kb_ = kc * bexp
            fac = jnp.exp(jnp.minimum(jnp.where(top_c, gl0 - g0, gl1 - g1), 0.0))
            chains.append(dict(
                key=(d, p, c),
                rhs=jnp.concatenate([vc * bexp, kb_ * eg], axis=1),
                qg=(qc * eg).astype(BF16),
                kq=jnp.concatenate([kb_, qc], axis=0).astype(BF16),
                Z=_bd_rows(kc, left1),
                dec=jnp.where(tri, jnp.exp(jnp.minimum(gcc - grow, 0.0)), 0.0),
                strict=strict,
                kg=(kT_cache[(d, p)][:, rs] * fac).astype(BF16),
                egl=jnp.exp(glast)))
        for ch in chains:
            kkqk = _dot_nt(ch["kq"], ch["Z"])
            ch["N"] = jnp.where(ch["strict"], -kkqk[:C] * ch["dec"], 0.0)
            ch["qk"] = (kkqk[C:] * ch["dec"]).astype(BF16)
        for ch in chains:
            ch["Pm"] = eye + ch["N"]
        for j in range(6):
            for ch in chains:
                nb = _bd_rows(ch["N"], left1)
                if j == 0:
                    ch["N"] = _dot(ch["N"].astype(BF16), nb)
                elif j < 5:
                    both = _dot(jnp.concatenate([ch["Pm"], ch["N"]], axis=0).astype(BF16), nb)
                    ch["Pm"] = ch["Pm"] + both[:C]
                    ch["N"] = both[C:]
                else:
                    ch["Pm"] = ch["Pm"] + _dot(ch["Pm"].astype(BF16), nb)
        out = {}
        for ch in chains:
            X = _dot(ch["Pm"].astype(BF16), _bd_rows(ch["rhs"], left2))
            out[ch["key"]] = dict(
                u=X[:, :LANES], qk=ch["qk"], kg=ch["kg"], egl=ch["egl"],
                wq=jnp.concatenate([X[:, LANES:].astype(BF16), ch["qg"]], axis=0))
        return out

    by_key = {}
    group_chunks = 4
    for d in range(2):
        for c0 in range(0, G, group_chunks):
            by_key.update(prepare([(d, p, c) for p in range(P) for c in range(c0, min(c0 + group_chunks, G))]))

    for step in range(G):
        live = []
        for d, *_refs, o_ref in dirs:
            c = step if d == 0 else G - 1 - step
            for p in range(P):
                live.append((d * P + p, o_ref, slice(c * C, (c + 1) * C), p, by_key[(d, p, c)]))
        S = [st_ref[sidx] for sidx, *_ in live]
        t1 = [_dot(ch["wq"], S[i].astype(BF16)) for i, (*_, ch) in enumerate(live)]
        vn = [ch["u"] - t1[i][:C] for i, (*_, ch) in enumerate(live)]
        for i, (sidx, o_ref, rs, p, ch) in enumerate(live):
            o_ref[0, rs, p * LANES:(p + 1) * LANES] = t1[i][C:] + _dot(ch["qk"], _bd_rows(vn[i], left1))
        for i, (sidx, o_ref, rs, p, ch) in enumerate(live):
            upd = _dot(ch["kg"], vn[i].astype(BF16))
            decay = jnp.where(top, ch["egl"][:, 0:1], ch["egl"][:, C:C + 1])
            st_ref[sidx] = S[i] * decay + jnp.where(bdmask, upd, 0.0)


def _dn_scan(q, k, v, bg, gt, *, G=8):
    B, S, _ = q.shape
    G = min(G, S // DN_CHUNK)
    R = G * DN_CHUNK
    nb = S // R
    n_ba = 4 * DN_HEADS
    P = DN_HEADS // 2
    fw = lambda n: pl.BlockSpec((1, R, n), lambda b, i: (b, i, 0))
    bw = lambda n: pl.BlockSpec((1, R, n), lambda b, i: (b, nb - 1 - i, 0))
    fwt = pl.BlockSpec((1, n_ba, R), lambda b, i: (b, 0, i))
    bwt = pl.BlockSpec((1, n_ba, R), lambda b, i: (b, 0, nb - 1 - i))
    return pl.pallas_call(
        functools.partial(_dnscan_body, G=G),
        grid=(B, nb),
        in_specs=[fw(DN_WIDTH), fw(DN_WIDTH), fw(DN_WIDTH), fw(BA_PAD), fwt,
                  bw(DN_WIDTH), bw(DN_WIDTH), bw(DN_WIDTH), bw(BA_PAD), bwt],
        out_specs=[fw(DN_WIDTH), bw(DN_WIDTH)],
        out_shape=[jax.ShapeDtypeStruct((B, S, DN_WIDTH), F32)] * 2,
        scratch_shapes=[pltpu.VMEM((2 * P, LANES, LANES), F32)],
        compiler_params=pltpu.CompilerParams(dimension_semantics=("parallel", "arbitrary")),
        name="dn_scan",
    )(q, k, v, bg, gt, q, k, v, bg, gt)


def _outproj_body(x_ref, ow_ref, od_ref, of_ref, ob_ref, z_ref, g_ref, w_ref, o_ref, cat_ref):
    oc = of_ref[0] + ob_ref[0]
    on = oc * lax.rsqrt(_seg_sum64(oc * oc) * (1.0 / HEAD_DIM) + EPS) * g_ref[...] * _silu(z_ref[0])
    cat_ref[:, :WIN_WIDTH] = ow_ref[0]
    cat_ref[:, WIN_WIDTH:WIN_WIDTH + DIFF_WIDTH] = od_ref[0]
    cat_ref[:, WIN_WIDTH + DIFF_WIDTH:] = on.astype(BF16)
    o_ref[0] = x_ref[0] + _dot(cat_ref[...], w_ref[...])


def _outproj(x, o_win, o_diff, o_f, o_b, z, dn_g, w_out, *, tm=512):
    B, S, _ = x.shape
    tm = min(tm, S)
    row = lambda n: pl.BlockSpec((1, tm, n), lambda b, i: (b, i, 0))
    g = jnp.tile(dn_g.astype(F32), DN_HEADS).reshape(1, DN_WIDTH)
    return pl.pallas_call(
        _outproj_body,
        grid=(B, S // tm),
        in_specs=[row(D_MODEL), row(WIN_WIDTH), row(DIFF_WIDTH), row(DN_WIDTH), row(DN_WIDTH),
                  row(DN_WIDTH), _resident((1, DN_WIDTH)), _resident((MIX_WIDTH, D_MODEL))],
        out_specs=row(D_MODEL),
        out_shape=jax.ShapeDtypeStruct((B, S, D_MODEL), F32),
        scratch_shapes=[pltpu.VMEM((tm, MIX_WIDTH), BF16)],
        compiler_params=pltpu.CompilerParams(
            dimension_semantics=("parallel", "parallel"), vmem_limit_bytes=VMEM_LIMIT),
        name="outproj",
    )(x, o_win, o_diff, o_f, o_b, z, g, w_out.astype(BF16))


def _mixer(x, ln_mix, w_mix_in, conv_w, sink, diff_lam, diff_g, a_log, dt_bias, dn_g, w_mix_out, lam_init):
    wq, wk, wv, dq, dkt, dv, dn_raw, z, ba, bat = _inproj(x, ln_mix, w_mix_in)
    win_slopes = 2.0 ** (-8.0 * jnp.arange(1, WIN_Q_HEADS + 1, dtype=F32) / WIN_Q_HEADS)
    o_win = _win_attention(wq, wk, wv, sink, win_slopes)
    o_diff = _diff_attention(dq, dkt, dv, diff_lam, diff_g, lam_init)
    q, k, v, bg, gt = _dn_prep(dn_raw, ba, bat, conv_w, a_log, dt_bias)
    o_f, o_b = _dn_scan(q, k, v, bg, gt)
    return _outproj(x, o_win, o_diff, o_f, o_b, z, dn_g, w_mix_out)


def kernel(x, ln_ffn1, ffn1_w_in, ffn1_w_out, ln_mix, w_mix_in, conv_w, sink_logits, diff_lambda,
           diff_norm_g, dn_A_log, dn_dt_bias, dn_norm_g, w_mix_out, ln_ffn2, ffn2_w_in, ffn2_w_out,
           ln_final):
    B, S, D = x.shape
    depth = ln_ffn1.shape[0]
    for l in range(depth):
        lam_init = 0.8 - 0.6 * math.exp(-0.3 * l)
        x = _ffn(x.reshape(B * S, D), ln_ffn1[l], ffn1_w_in[l], ffn1_w_out[l]).reshape(B, S, D)
        x = _mixer(x, ln_mix[l], w_mix_in[l], conv_w[l], sink_logits[l], diff_lambda[l], diff_norm_g[l],
                   dn_A_log[l], dn_dt_bias[l], dn_norm_g[l], w_mix_out[l], lam_init)
        g_final = ln_final if l == depth - 1 else None
        x = _ffn(x.reshape(B * S, D), ln_ffn2[l], ffn2_w_in[l], ffn2_w_out[l], g_final).reshape(B, S, D)
    return x
```

```python
import functools
import math

import jax
import jax.numpy as jnp
from jax import lax
from jax.experimental import pallas as pl
from jax.experimental.pallas import tpu as pltpu

F32 = jnp.float32
BF16 = jnp.bfloat16

D_MODEL = 1024
HEAD_DIM = 64
EPS = 1e-6
WIN_Q_HEADS = 6
WIN_KV_HEADS = 2
WIN_GROUP = WIN_Q_HEADS // WIN_KV_HEADS
WINDOW = 128
DIFF_HEADS = 4
DIFF_QK_DIM = HEAD_DIM // 2
DN_HEADS = 6
DN_CHUNK = 64
CONV_W = 5
D_FF = 2752
WIN_WIDTH = WIN_Q_HEADS * HEAD_DIM
WIN_KV_WIDTH = WIN_KV_HEADS * HEAD_DIM
DIFF_WIDTH = DIFF_HEADS * HEAD_DIM
DN_WIDTH = DN_HEADS * HEAD_DIM
DN_QKV = 3 * DN_WIDTH
MIX_WIDTH = WIN_WIDTH + DIFF_WIDTH + DN_WIDTH
OFF_WQ = 0
OFF_WK = OFF_WQ + WIN_WIDTH
OFF_WV = OFF_WK + WIN_KV_WIDTH
OFF_DQ = OFF_WV + WIN_KV_WIDTH
OFF_DK = OFF_DQ + DIFF_WIDTH
OFF_DV = OFF_DK + DIFF_WIDTH
OFF_DN = OFF_DV + DIFF_WIDTH
OFF_Z = OFF_DN + DN_QKV
OFF_BA = OFF_Z + DN_WIDTH
MIX_IN = OFF_BA + 4 * DN_HEADS

LANES = 128
FF_CHUNK = 256
D_FF_PAD = -(-D_FF // FF_CHUNK) * FF_CHUNK
VMEM_LIMIT = 56 * 1024 * 1024
BF16_ROWS = 16
LOG2E = math.log2(math.e)
DIFF_TK = 512
DIFF_KROWS = 48
DIFF_BIAS_ROWS = 3
DIFF_KT_ROWS = 2 * DIFF_HEADS * DIFF_KROWS
DIFF_V_WIDTH = DIFF_HEADS * LANES
DIFF_SKIP_LOG2 = 150.0
DIFF_SLOPES = tuple(2.0 ** (-8.0 * (i + 1) / DIFF_HEADS) for i in range(DIFF_HEADS))


def _rms(x, g):
    return x * lax.rsqrt(jnp.mean(x * x, axis=-1, keepdims=True) + EPS) * g


def _silu(x):
    h = 0.5 * x
    return h + h * jnp.tanh(h)


def _seg_sum64(x):
    R, width = x.shape
    n = width // LANES
    rows = jnp.concatenate([x[:, c * LANES:(c + 1) * LANES] for c in range(n)], axis=0)
    r = lax.broadcasted_iota(jnp.int32, (LANES, LANES), 0) < HEAD_DIM
    c = lax.broadcasted_iota(jnp.int32, (LANES, LANES), 1) < HEAD_DIM
    ones_bd = (r == c).astype(F32).astype(BF16)
    hi = rows.astype(BF16)
    r1 = rows - hi.astype(F32)
    mid = r1.astype(BF16)
    lo = (r1 - mid.astype(F32)).astype(BF16)
    ss = _dot(hi, ones_bd) + _dot(mid, ones_bd) + _dot(lo, ones_bd)
    return jnp.concatenate([ss[c * R:(c + 1) * R] for c in range(n)], axis=1)


def _dot(a, b):
    return jnp.dot(a, b, preferred_element_type=F32)


def _dot_nt(a, b):
    return lax.dot_general(a, b, (((1,), (1,)), ((), ())), preferred_element_type=F32)


def _resident(shape):
    nd = len(shape)
    return pl.BlockSpec(shape, lambda *_: (0,) * nd, pipeline_mode=pl.Buffered(1))


def _ffn_body(x_ref, g_ref, wgu_ref, wo_ref, *rest, n_chunks, final):
    if final:
        gf_ref, o_ref, acc_ref = rest
    else:
        o_ref, acc_ref = rest
    x = x_ref[...]
    h = _rms(x, g_ref[...]).astype(BF16)
    for c in range(n_chunks):
        gu = _dot(h, wgu_ref[:, c * 2 * FF_CHUNK:(c + 1) * 2 * FF_CHUNK])
        a = (_silu(gu[:, :FF_CHUNK]) * gu[:, FF_CHUNK:]).astype(BF16)
        part = _dot(a, wo_ref[c * FF_CHUNK:(c + 1) * FF_CHUNK, :])
        if c == 0:
            acc_ref[...] = part
        else:
            acc_ref[...] += part
    y = x + 0.5 * acc_ref[...]
    if final:
        y = _rms(y, gf_ref[...])
    o_ref[...] = y


def _prep_ffn_weights(w_in, w_out):
    n = D_FF_PAD // FF_CHUNK
    pad = D_FF_PAD - D_FF
    wg = jnp.pad(w_in[:, :D_FF], ((0, 0), (0, pad))).reshape(D_MODEL, n, FF_CHUNK)
    wu = jnp.pad(w_in[:, D_FF:], ((0, 0), (0, pad))).reshape(D_MODEL, n, FF_CHUNK)
    wgu = jnp.concatenate([wg, wu], axis=2).reshape(D_MODEL, n * 2 * FF_CHUNK).astype(BF16)
    wo = jnp.pad(w_out, ((0, pad), (0, 0))).astype(BF16)
    return wgu, wo


def _ffn(x2d, g, w_in, w_out, g_final=None, *, tm=512):
    T = x2d.shape[0]
    tm = min(tm, T)
    n_chunks = D_FF_PAD // FF_CHUNK
    wgu, wo = _prep_ffn_weights(w_in, w_out)
    final = g_final is not None
    in_specs = [
        pl.BlockSpec((tm, D_MODEL), lambda i: (i, 0)),
        _resident((1, D_MODEL)),
        _resident(wgu.shape),
        _resident(wo.shape),
    ]
    args = [x2d, g.reshape(1, D_MODEL), wgu, wo]
    if final:
        in_specs.append(_resident((1, D_MODEL)))
        args.append(g_final.reshape(1, D_MODEL))
    return pl.pallas_call(
        functools.partial(_ffn_body, n_chunks=n_chunks, final=final),
        grid=(T // tm,),
        in_specs=in_specs,
        out_specs=pl.BlockSpec((tm, D_MODEL), lambda i: (i, 0)),
        out_shape=jax.ShapeDtypeStruct((T, D_MODEL), F32),
        scratch_shapes=[pltpu.VMEM((tm, D_MODEL), F32)],
        compiler_params=pltpu.CompilerParams(
            dimension_semantics=("parallel",), vmem_limit_bytes=VMEM_LIMIT),
        name="ffn_final" if final else "ffn",
    )(*args)


BA_PAD = LANES


def _inproj_body(x_ref, g_ref, w_ref, wkt_ref, wbat_ref, wdv_ref,
                 wq_ref, wk_ref, wv_ref, dq_ref, dkt_ref, dv_ref, dn_ref, z_ref, ba_ref, bat_ref):
    h = _rms(x_ref[0], g_ref[...]).astype(BF16)
    wq_ref[0] = (_dot(h, w_ref[:, OFF_WQ:OFF_WK]) * (HEAD_DIM ** -0.5 * LOG2E)).astype(BF16)
    wk_ref[0] = _dot(h, w_ref[:, OFF_WK:OFF_WV]).astype(BF16)
    wv_ref[0] = _dot(h, w_ref[:, OFF_WV:OFF_DQ]).astype(BF16)
    dq_ref[0] = (_dot(h, w_ref[:, OFF_DQ:OFF_DK]) * (DIFF_QK_DIM ** -0.5 * LOG2E)).astype(BF16)
    kt = _dot_nt(wkt_ref[...], h)
    tm = kt.shape[1]
    extra_rows = DIFF_KROWS - DIFF_QK_DIM
    pos = pl.program_id(1) * tm + lax.broadcasted_iota(jnp.int32, (extra_rows, tm), 1)
    kr = (pos % DIFF_TK).astype(F32)
    rowi = lax.broadcasted_iota(jnp.int32, (extra_rows, tm), 0)
    for hd in range(DIFF_HEADS):
        bias = (DIFF_SLOPES[hd] * LOG2E) * kr
        hi = bias.astype(BF16).astype(F32)
        mid = (bias - hi).astype(BF16).astype(F32)
        lo = bias - hi - mid
        extra = jnp.where(rowi == 0, hi, jnp.where(rowi == 1, mid, jnp.where(rowi == 2, lo, 0.0))).astype(BF16)
        for mp in range(2):
            hm = 2 * hd + mp
            dkt_ref[0, hm * DIFF_KROWS:hm * DIFF_KROWS + DIFF_QK_DIM, :] = (
                kt[hm * DIFF_QK_DIM:(hm + 1) * DIFF_QK_DIM, :].astype(BF16))
            dkt_ref[0, hm * DIFF_KROWS + DIFF_QK_DIM:(hm + 1) * DIFF_KROWS, :] = extra
    dv = _dot(h, wdv_ref[...])
    ones_half = (lax.broadcasted_iota(jnp.int32, dv.shape, 1) % LANES) >= HEAD_DIM
    dv_ref[0] = jnp.where(ones_half, 1.0, dv).astype(BF16)
    dn_ref[0] = _dot(h, w_ref[:, OFF_DN:OFF_Z])
    z_ref[0] = _dot(h, w_ref[:, OFF_Z:OFF_BA])
    ba_ref[0] = _dot(h, w_ref[:, OFF_BA:OFF_BA + BA_PAD])
    bat_ref[0] = _dot_nt(wbat_ref[...], h)


def _inproj(x, g, w_in, *, tm=512):
    B, S, _ = x.shape
    tm = min(tm, S)
    n_ba = 4 * DN_HEADS
    w = jnp.pad(w_in, ((0, 0), (0, OFF_BA + BA_PAD - MIX_IN))).astype(BF16)
    wkt = w_in[:, OFF_DK:OFF_DV].T.astype(BF16)
    wbat = w_in[:, OFF_BA:MIX_IN].T.astype(BF16)
    wdv = jnp.pad(w_in[:, OFF_DV:OFF_DN].reshape(D_MODEL, DIFF_HEADS, HEAD_DIM),
                  ((0, 0), (0, 0), (0, LANES - HEAD_DIM))).reshape(D_MODEL, DIFF_V_WIDTH).astype(BF16)
    row = lambda n: pl.BlockSpec((1, tm, n), lambda b, i: (b, i, 0))
    col = lambda n: pl.BlockSpec((1, n, tm), lambda b, i: (b, 0, i))
    out_shapes = [
        jax.ShapeDtypeStruct((B, S, WIN_WIDTH), BF16),
        jax.ShapeDtypeStruct((B, S, WIN_KV_WIDTH), BF16),
        jax.ShapeDtypeStruct((B, S, WIN_KV_WIDTH), BF16),
        jax.ShapeDtypeStruct((B, S, DIFF_WIDTH), BF16),
        jax.ShapeDtypeStruct((B, DIFF_KT_ROWS, S), BF16),
        jax.ShapeDtypeStruct((B, S, DIFF_V_WIDTH), BF16),
        jax.ShapeDtypeStruct((B, S, DN_QKV), F32),
        jax.ShapeDtypeStruct((B, S, DN_WIDTH), F32),
        jax.ShapeDtypeStruct((B, S, BA_PAD), F32),
        jax.ShapeDtypeStruct((B, n_ba, S), F32),
    ]
    out_specs = [row(WIN_WIDTH), row(WIN_KV_WIDTH), row(WIN_KV_WIDTH), row(DIFF_WIDTH),
                 col(DIFF_KT_ROWS), row(DIFF_V_WIDTH), row(DN_QKV), row(DN_WIDTH), row(BA_PAD), col(n_ba)]
    return pl.pallas_call(
        _inproj_body,
        grid=(B, S // tm),
        in_specs=[row(D_MODEL), _resident((1, D_MODEL)), _resident(w.shape),
                  _resident(wkt.shape), _resident(wbat.shape), _resident(wdv.shape)],
        out_specs=out_specs,
        out_shape=out_shapes,
        compiler_params=pltpu.CompilerParams(
            dimension_semantics=("parallel", "parallel"), vmem_limit_bytes=VMEM_LIMIT),
        name="inproj",
    )(x, g.reshape(1, D_MODEL), w, wkt, wbat, wdv)


def _win_body(q_ref, kp_ref, kc_ref, kn_ref, vp_ref, vc_ref, vn_ref, sink_ref, slope_ref, o_ref,
              *, n_steps, nq):
    i = pl.program_id(1)
    W = WINDOW
    KW = 3 * W
    kw = jnp.concatenate([kp_ref[0], kc_ref[0], kn_ref[0]], axis=0).astype(F32)
    vw = jnp.concatenate([vp_ref[0], vc_ref[0], vn_ref[0]], axis=0).astype(F32)
    kw_sw = pltpu.roll(kw, HEAD_DIM, 1)
    vw_sw = pltpu.roll(vw, HEAD_DIM, 1)
    left = lax.broadcasted_iota(jnp.int32, (KW, LANES), 1) < HEAD_DIM
    qi = lax.broadcasted_iota(jnp.int32, (W, KW), 0)
    ki = lax.broadcasted_iota(jnp.int32, (W, KW), 1)
    dist = jnp.abs(ki - W - qi)
    band_bias = [jnp.where(dist <= W, (-LOG2E) * slope_ref[:, hq:hq + 1] * dist.astype(F32), -1e30)
                 for hq in range(WIN_Q_HEADS)]
    edge_lo = jnp.where(jnp.logical_or(ki >= W, i > 0), 0.0, -1e30)
    edge_hi = jnp.where(jnp.logical_or(ki < 2 * W, i < n_steps - 1), 0.0, -1e30)
    n_pairs = WIN_Q_HEADS // 2
    blocks = []
    for a in range(nq):
        rows = slice(a * W, a * W + KW)

        def stack(first, second, rows=rows):
            return jnp.concatenate([jnp.where(left, first[rows], 0.0),
                                    jnp.where(left, 0.0, second[rows])], axis=0).astype(BF16)

        zk = (stack(kw, kw_sw), stack(kw, kw), stack(kw_sw, kw))
        zv = (stack(vw, vw_sw), stack(vw, vw), stack(vw_sw, vw))
        edge = None
        if a == 0:
            edge = edge_lo
        if a == nq - 1:
            edge = edge_hi if edge is None else edge + edge_hi
        q = q_ref[0, a * W:(a + 1) * W, :]
        s2 = [_dot_nt(q[:, p * LANES:(p + 1) * LANES], zk[p]) for p in range(n_pairs)]
        blocks.append((a, edge, zv, s2))
    probs = {}
    for a, edge, zv, s2 in blocks:
        for hq in range(WIN_Q_HEADS):
            p, t = divmod(hq, 2)
            s = s2[p][:, t * KW:(t + 1) * KW] + band_bias[hq]
            if edge is not None:
                s = s + edge
            sink = LOG2E * sink_ref[:, hq:hq + 1]
            m = jnp.maximum(jnp.max(s, axis=-1, keepdims=True), sink)
            e = jnp.exp2(s - m)
            denom = jnp.sum(e, axis=-1, keepdims=True) + jnp.exp2(sink - m)
            probs[(a, hq)] = (e * (1.0 / denom)).astype(BF16)
    for a, edge, zv, s2 in blocks:
        for p in range(n_pairs):
            pp = jnp.concatenate([probs[(a, 2 * p)], probs[(a, 2 * p + 1)]], axis=1)
            o_ref[0, a * W:(a + 1) * W, p * LANES:(p + 1) * LANES] = _dot(pp, zv[p]).astype(BF16)


def _win_attention(q, k, v, sink, slopes, *, nq=2):
    B, S, _ = q.shape
    W = WINDOW
    nq = min(nq, S // W)
    ns = S // (nq * W)
    nb = S // W
    qspec = pl.BlockSpec((1, nq * W, WIN_WIDTH), lambda b, i: (b, i, 0))
    prev = pl.BlockSpec((1, W, WIN_KV_WIDTH), lambda b, i: (b, jnp.maximum(i * nq - 1, 0), 0))
    cur = pl.BlockSpec((1, nq * W, WIN_KV_WIDTH), lambda b, i: (b, i, 0))
    nxt = pl.BlockSpec((1, W, WIN_KV_WIDTH), lambda b, i: (b, jnp.minimum((i + 1) * nq, nb - 1), 0))
    return pl.pallas_call(
        functools.partial(_win_body, n_steps=ns, nq=nq),
        grid=(B, ns),
        in_specs=[qspec, prev, cur, nxt, prev, cur, nxt,
                  _resident((1, WIN_Q_HEADS)), _resident((1, WIN_Q_HEADS))],
        out_specs=pl.BlockSpec((1, nq * W, WIN_WIDTH), lambda b, i: (b, i, 0)),
        out_shape=jax.ShapeDtypeStruct((B, S, WIN_WIDTH), BF16),
        compiler_params=pltpu.CompilerParams(dimension_semantics=("parallel", "parallel")),
        name="win_attn",
    )(q, k, k, k, v, v, v, sink.reshape(1, WIN_Q_HEADS).astype(F32), slopes.reshape(1, WIN_Q_HEADS))


def _diff_ranges(q, kt, tq, tk):
    B, S, _ = q.shape
    nq, nk, n_maps = S // tq, S // tk, 2 * DIFF_HEADS
    qf = q.astype(F32).reshape(B, nq, tq, n_maps, DIFF_QK_DIM)
    qn = jnp.sqrt(jnp.max(jnp.sum(qf * qf, axis=-1), axis=2))
    kf = kt.astype(F32).reshape(B, n_maps, DIFF_KROWS, S)[:, :, :DIFF_QK_DIM]
    kn_pos = jnp.sqrt(jnp.sum(kf * kf, axis=2))
    kn = jnp.max(kn_pos.reshape(B, n_maps, nk, tk), axis=-1)
    kown = jnp.max(kn_pos.reshape(B, n_maps, nq, tq), axis=-1)
    reach = jnp.swapaxes(qn, 1, 2)[..., None] * (kn[:, :, None, :] + kown[..., None]) * (1.0 + 1e-3)
    reach = jnp.max(reach.reshape(B, DIFF_HEADS, 2, nq, nk), axis=2)
    q0 = jnp.arange(nq, dtype=jnp.int32) * tq
    k0 = jnp.arange(nk, dtype=jnp.int32) * tk
    dmin = jnp.maximum(jnp.maximum(k0[None, :] - (q0[:, None] + tq - 1), q0[:, None] - (k0[None, :] + tk - 1)), 0)
    c = jnp.asarray(DIFF_SLOPES, F32) * LOG2E
    need = reach - c[None, :, None, None] * dmin.astype(F32)[None, None] > -DIFF_SKIP_LOG2
    idx = jnp.arange(nk, dtype=jnp.int32)
    need = jnp.logical_or(need, (idx[None, :] == (q0 // tk)[:, None])[None, None])
    lo = jnp.min(jnp.where(need, idx, nk), axis=-1)
    hi = jnp.max(jnp.where(need, idx, -1), axis=-1)
    return jnp.stack([lo, hi], axis=-1).transpose(0, 2, 1, 3).reshape(-1).astype(jnp.int32)


def _diff_body(rng_ref, q_ref, kt_ref, v_ref, lam_ref, g_ref, o_ref, qv_ref, m_ref, acc_ref,
               s0_ref, st0_ref, s1_ref, st1_ref, *, tq, tk, lam_init):
    n_maps = 2 * DIFF_HEADS
    n_groups = tk // LANES
    q0 = pl.program_id(1) * tq
    jd = q0 // tk
    q = q_ref[0]
    lane_x = lax.broadcasted_iota(jnp.int32, (tq, DIFF_KROWS - DIFF_QK_DIM), 1)
    for var, sign in enumerate((1.0, -1.0, 0.0)):
        extras = jnp.where(lane_x < DIFF_BIAS_ROWS, -sign, 0.0).astype(BF16)
        for hm in range(n_maps):
            qv_ref[var, hm, :, 0:DIFF_QK_DIM] = q[:, hm * DIFF_QK_DIM:(hm + 1) * DIFF_QK_DIM]
            qv_ref[var, hm, :, DIFF_QK_DIM:DIFF_KROWS] = extras
    m_ref[...] = jnp.full(m_ref.shape, -jnp.inf, F32)
    acc_ref[...] = jnp.zeros(acc_ref.shape, F32)
    qpos = (q0 + lax.broadcasted_iota(jnp.int32, (tq, LANES), 0)).astype(F32)
    bufs = ((s0_ref, st0_ref), (s1_ref, st1_ref))
    rng_base = (pl.program_id(0) * pl.num_programs(1) + pl.program_id(1)) * (2 * DIFF_HEADS)

    los = [rng_ref[rng_base + 2 * h] for h in range(DIFF_HEADS)]
    n_offs = [rng_ref[rng_base + 2 * h + 1] - los[h] for h in range(DIFF_HEADS)]
    starts = [0]
    for h in range(DIFF_HEADS):
        starts.append(starts[-1] + n_offs[h])
    n_total = starts[DIFF_HEADS]

    def pick(h, vals):
        out = vals[-1]
        for i in range(len(vals) - 2, -1, -1):
            out = jnp.where(h == i, vals[i], out)
        return out

    def locate(u):
        h = sum((u >= starts[i]).astype(jnp.int32) for i in range(1, DIFF_HEADS))
        t = pick(h, los) + u - pick(h, starts[:DIFF_HEADS])
        return h, t + (t >= jd).astype(jnp.int32)

    def scores(h, j, buf, diag_dist=None):
        s_ref, st_ref = bufs[buf]
        if diag_dist is not None:
            var = 2
            bias = (DIFF_SLOPES[h] * LOG2E) * diag_dist
        else:
            c = pick(h, [sl * LOG2E for sl in DIFF_SLOPES])
            after = j > jd
            var = jnp.where(after, 0, 1)
            rowoff = jnp.where(after, c, -c) * ((j * tk).astype(F32) - qpos)
        k0 = pl.multiple_of(j * tk, tk)
        for mp in range(2):
            hm = 2 * h + mp
            r0 = hm * DIFF_KROWS if diag_dist is not None else pl.multiple_of(hm * DIFF_KROWS, BF16_ROWS)
            kt = kt_ref[0, pl.ds(r0, DIFF_KROWS), pl.ds(k0, tk)]
            s = _dot(qv_ref[var, hm], kt)
            if diag_dist is not None:
                s = s - bias
            s_ref[mp] = s
            mx = s[:, 0:LANES]
            for g in range(1, n_groups):
                mx = jnp.maximum(mx, s[:, g * LANES:(g + 1) * LANES])
            mrow = jnp.broadcast_to(jnp.max(mx, axis=-1, keepdims=True), (tq, LANES))
            m_old = m_ref[hm]
            if diag_dist is not None:
                m_new = jnp.maximum(m_old, mrow)
                shift = m_new
            else:
                m_new = jnp.maximum(m_old, mrow - rowoff)
                shift = m_new + rowoff
            st_ref[mp, 0] = shift
            st_ref[mp, 1] = jnp.exp2(m_old - m_new)
            m_ref[hm] = m_new

    def values(h, j, buf, static=False):
        s_ref, st_ref = bufs[buf]
        k0 = pl.multiple_of(j * tk, tk)
        c0 = h * LANES if static else pl.multiple_of(h * LANES, LANES)
        v2 = v_ref[0, pl.ds(k0, tk), pl.ds(c0, LANES)]
        for mp in range(2):
            hm = 2 * h + mp
            shift = st_ref[mp, 0]
            alpha = st_ref[mp, 1]
            e = jnp.exp2(s_ref[mp] - jnp.concatenate([shift] * n_groups, axis=1))
            acc_ref[hm] = alpha * acc_ref[hm] + _dot(e.astype(BF16), v2)

    kpos = (jd * tk + lax.broadcasted_iota(jnp.int32, (tq, tk), 1)).astype(F32)
    qp = (q0 + lax.broadcasted_iota(jnp.int32, (tq, tk), 0)).astype(F32)
    diag_dist = jnp.abs(qp - kpos)
    scores(0, jd, 0, diag_dist)
    for h in range(DIFF_HEADS):
        if h + 1 < DIFF_HEADS:
            scores(h + 1, jd, (h + 1) % 2, diag_dist)
        values(h, jd, h % 2, static=True)

    def s_at(u, buf):
        h, j = locate(u)
        scores(h, j, buf)

    def v_at(u, buf):
        h, j = locate(u)
        values(h, j, buf)

    @pl.when(n_total > 0)
    def _():
        s_at(0, 0)

    def pair(p, carry):
        u = 2 * p
        s_at(u + 1, 1)
        v_at(u, 0)
        s_at(u + 2, 0)
        v_at(u + 1, 1)
        return carry

    n_pairs = jnp.maximum(n_total - 1, 0) // 2
    lax.fori_loop(0, n_pairs, pair, 0)
    u_last = 2 * n_pairs
    left_over = n_total - u_last

    @pl.when(left_over == 2)
    def _():
        s_at(u_last + 1, 1)
        v_at(u_last, 0)
        v_at(u_last + 1, 1)

    @pl.when(left_over == 1)
    def _():
        v_at(u_last, 0)

    lp = lam_ref[...]
    lam = (jnp.exp(jnp.sum(lp[0:1] * lp[1:2], axis=-1, keepdims=True))
           - jnp.exp(jnp.sum(lp[2:3] * lp[3:4], axis=-1, keepdims=True)) + lam_init)
    lane = lax.broadcasted_iota(jnp.int32, (tq, LANES), 1)
    left = lane < HEAD_DIM
    ys = []
    for h in range(DIFF_HEADS):
        a1, a2 = acc_ref[2 * h], acc_ref[2 * h + 1]
        od = a1 / pltpu.roll(a1, HEAD_DIM, 1) - lam * (a2 / pltpu.roll(a2, HEAD_DIM, 1))
        ms = jnp.sum(jnp.where(left, od * od, 0.0), axis=-1, keepdims=True) * (1.0 / HEAD_DIM)
        ys.append(od * lax.rsqrt(ms + EPS) * g_ref[...] * (1.0 - lam_init))
    for p in range(DIFF_HEADS // 2):
        o_ref[0, :, p * LANES:(p + 1) * LANES] = jnp.where(
            left, ys[2 * p], pltpu.roll(ys[2 * p + 1], HEAD_DIM, 1)).astype(BF16)


def _diff_attention(q, kt, v, diff_lambda, diff_g, lam_init, *, tq=512):
    B, S, _ = q.shape
    tk = DIFF_TK
    tq = min(tq, S)
    assert S % tk == 0 and tk % tq == 0
    n_maps = 2 * DIFF_HEADS
    g2 = jnp.concatenate([diff_g, diff_g]).reshape(1, LANES).astype(F32)
    grid_spec = pltpu.PrefetchScalarGridSpec(
        num_scalar_prefetch=1,
        grid=(B, S // tq),
        in_specs=[pl.BlockSpec((1, tq, DIFF_WIDTH), lambda b, i, r: (b, i, 0)),
                  pl.BlockSpec((1, DIFF_KT_ROWS, S), lambda b, i, r: (b, 0, 0)),
                  pl.BlockSpec((1, S, DIFF_V_WIDTH), lambda b, i, r: (b, 0, 0)),
                  pl.BlockSpec((4, DIFF_QK_DIM), lambda b, i, r: (0, 0)),
                  pl.BlockSpec((1, LANES), lambda b, i, r: (0, 0))],
        out_specs=pl.BlockSpec((1, tq, DIFF_WIDTH), lambda b, i, r: (b, i, 0)),
        scratch_shapes=[pltpu.VMEM((3, n_maps, tq, DIFF_KROWS), BF16),
                        pltpu.VMEM((n_maps, tq, LANES), F32),
                        pltpu.VMEM((n_maps, tq, LANES), F32),
                        pltpu.VMEM((2, tq, tk), F32), pltpu.VMEM((2, 2, tq, LANES), F32),
                        pltpu.VMEM((2, tq, tk), F32), pltpu.VMEM((2, 2, tq, LANES), F32)])
    return pl.pallas_call(
        functools.partial(_diff_body, tq=tq, tk=tk, lam_init=lam_init),
        grid_spec=grid_spec,
        out_shape=jax.ShapeDtypeStruct((B, S, DIFF_WIDTH), BF16),
        compiler_params=pltpu.CompilerParams(
            dimension_semantics=("parallel", "arbitrary"), vmem_limit_bytes=VMEM_LIMIT),
        name="diff_attn",
    )(_diff_ranges(q, kt, tq, tk), q, kt, v, diff_lambda.astype(F32), g2)


def _l2n64(x):
    return x * lax.rsqrt(_seg_sum64(x * x) + EPS)


def _dnprep_body(x_ref, xp_ref, xn_ref, w_ref, ba_ref, bat_ref, ab_ref, abt_ref,
                 q_ref, k_ref, v_ref, bg_ref, gt_ref, *, ts, n_tiles):
    i = pl.program_id(1)
    H = DN_HEADS
    x = x_ref[0]
    xp = jnp.where(i > 0, xp_ref[0], 0.0)
    xn = jnp.where(i < n_tiles - 1, xn_ref[0], 0.0)
    xx = jnp.concatenate([xp, x, xn], axis=0)
    half = CONV_W // 2
    y = jnp.zeros_like(x)
    for j in range(CONV_W):
        y = y + w_ref[j:j + 1, :] * xx[8 - half + j:8 - half + j + ts, :]
    y = _silu(y)
    q_ref[0] = _l2n64(y[:, :DN_WIDTH]) * (HEAD_DIM ** -0.5)
    k_ref[0] = _l2n64(y[:, DN_WIDTH:2 * DN_WIDTH])
    v_ref[0] = y[:, 2 * DN_WIDTH:]

    def gates(raw, a_log, dt_bias):
        beta = 1.0 / (1.0 + jnp.exp(-raw))
        z = raw + dt_bias
        softplus = jnp.maximum(z, 0.0) + jnp.log(1.0 + jnp.exp(-jnp.abs(z)))
        return beta, -jnp.exp(a_log) * softplus

    ba = ba_ref[0]
    beta, g = gates(ba, ab_ref[0:1, :], ab_ref[1:2, :])
    lane = lax.broadcasted_iota(jnp.int32, ba.shape, 1)
    r = lax.broadcasted_iota(jnp.int32, ba.shape, 0) % DN_CHUNK
    bwd = jnp.logical_and(lane >= 3 * H, lane < 4 * H)
    gc = g
    for d in (1, 2, 4, 8, 16, 32):
        f = jnp.where(r >= d, pltpu.roll(gc, d, 0), 0.0)
        b = jnp.where(r < DN_CHUNK - d, pltpu.roll(gc, ts - d, 0), 0.0)
        gc = gc + jnp.where(bwd, b, f)
    bg_ref[0] = jnp.where(lane < 2 * H, beta, gc)
    bat = bat_ref[0]
    _, gt = gates(bat, abt_ref[:, 0:1], abt_ref[:, 1:2])
    row = lax.broadcasted_iota(jnp.int32, bat.shape, 0)
    c = lax.broadcasted_iota(jnp.int32, bat.shape, 1) % DN_CHUNK
    bwd_t = row >= 3 * H
    for d in (1, 2, 4, 8, 16, 32):
        f = jnp.where(c >= d, pltpu.roll(gt, d, 1), 0.0)
        b = jnp.where(c < DN_CHUNK - d, pltpu.roll(gt, ts - d, 1), 0.0)
        gt = gt + jnp.where(bwd_t, b, f)
    gt_ref[0] = gt


def _dn_prep(dn_raw, ba, bat, conv_w, a_log, dt_bias, *, ts=256):
    B, S, C = dn_raw.shape
    ts = min(ts, S)
    nt = S // ts
    n_ba = 4 * DN_HEADS
    hb = ts // 8
    zeros = jnp.zeros((2 * DN_HEADS,), F32)
    al = jnp.concatenate([zeros, a_log.reshape(-1).astype(F32)])
    db = jnp.concatenate([zeros, dt_bias.reshape(-1).astype(F32)])
    ab = jnp.pad(jnp.stack([al, db]), ((0, 0), (0, BA_PAD - n_ba)))
    abt = jnp.stack([al, db], axis=1)
    row = lambda n: pl.BlockSpec((1, ts, n), lambda b, i: (b, i, 0))
    return pl.pallas_call(
        functools.partial(_dnprep_body, ts=ts, n_tiles=nt),
        grid=(B, nt),
        in_specs=[row(C),
                  pl.BlockSpec((1, 8, C), lambda b, i: (b, jnp.maximum(i * hb - 1, 0), 0)),
                  pl.BlockSpec((1, 8, C), lambda b, i: (b, jnp.minimum((i + 1) * hb, S // 8 - 1), 0)),
                  _resident((CONV_W, C)), row(BA_PAD),
                  pl.BlockSpec((1, n_ba, ts), lambda b, i: (b, 0, i)),
                  _resident((2, BA_PAD)), _resident((n_ba, 2))],
        out_specs=[row(DN_WIDTH), row(DN_WIDTH), row(DN_WIDTH), row(BA_PAD),
                   pl.BlockSpec((1, n_ba, ts), lambda b, i: (b, 0, i))],
        out_shape=[jax.ShapeDtypeStruct((B, S, DN_WIDTH), F32)] * 3
        + [jax.ShapeDtypeStruct((B, S, BA_PAD), F32), jax.ShapeDtypeStruct((B, n_ba, S), F32)],
        compiler_params=pltpu.CompilerParams(dimension_semantics=("parallel", "parallel")),
        name="dn_prep",
    )(dn_raw, dn_raw, dn_raw, conv_w.astype(F32), ba, bat, ab, abt)


def _bd_rows(y, left):
    return jnp.concatenate([jnp.where(left, y, 0.0), jnp.where(left, 0.0, y)], axis=0).astype(BF16)


def _dnscan_body(qf_ref, kf_ref, vf_ref, bgf_ref, gtf_ref, qb_ref, kb_ref, vb_ref, bgb_ref, gtb_ref,
                 of_ref, ob_ref, st_ref, *, G):
    C = DN_CHUNK
    H = DN_HEADS
    P = H // 2

    @pl.when(pl.program_id(1) == 0)
    def _():
        st_ref[...] = jnp.zeros(st_ref.shape, F32)

    lane1 = lax.broadcasted_iota(jnp.int32, (C, LANES), 1)
    row1 = lax.broadcasted_iota(jnp.int32, (C, LANES), 0)
    left1 = lane1 < C
    s_idx = lane1 % C
    eye = (row1 == s_idx).astype(F32)
    lane2 = lax.broadcasted_iota(jnp.int32, (C, 2 * LANES), 1)
    left2 = (lane2 % LANES) < C
    rr = lax.broadcasted_iota(jnp.int32, (LANES, LANES), 0)
    cc = lax.broadcasted_iota(jnp.int32, (LANES, LANES), 1)
    top = rr < C
    bdmask = (rr < C) == (cc < C)
    top_c = lax.broadcasted_iota(jnp.int32, (LANES, C), 0) < C

    dirs = ((0, qf_ref, kf_ref, vf_ref, bgf_ref, gtf_ref, of_ref),
            (1, qb_ref, kb_ref, vb_ref, bgb_ref, gtb_ref, ob_ref))
    kT_cache = {}

    def prepare(group):
        chains = []
        for d, p, c in group:
            _, q_ref, k_ref, v_ref, bg_ref, gt_ref, _ = dirs[d]
            tri = (row1 >= s_idx) if d == 0 else (row1 <= s_idx)
            strict = (row1 > s_idx) if d == 0 else (row1 < s_idx)
            last = C - 1 if d == 0 else 0
            sl = slice(p * LANES, (p + 1) * LANES)
            cb = d * H + 2 * p
            cg = 2 * H + d * H + 2 * p
            if (d, p) not in kT_cache:
                kT_cache[(d, p)] = k_ref[0, :, sl].T
            rs = slice(c * C, (c + 1) * C)
            qc, kc, vc = q_ref[0, rs, sl], k_ref[0, rs, sl], v_ref[0, rs, sl]
            bexp = jnp.where(left1, bg_ref[0, rs, cb:cb + 1], bg_ref[0, rs, cb + 1:cb + 2])
            gcc = jnp.where(left1, bg_ref[0, rs, cg:cg + 1], bg_ref[0, rs, cg + 1:cg + 2])
            g0 = gt_ref[0, cg:cg + 1, rs]
            g1 = gt_ref[0, cg + 1:cg + 2, rs]
            grow = jnp.concatenate([g0, g1], axis=1)
            glast = gcc[last:last + 1, :]
            gl0 = glast[:, 0:1]
            gl1 = glast[:, C:C + 1]
            eg = jnp.exp(gcc)
            kb_ = kc * bexp
            fac = jnp.exp(jnp.minimum(jnp.where(top_c, gl0 - g0, gl1 - g1), 0.0))
            chains.append(dict(
                key=(d, p, c),
                rhs=jnp.concatenate([vc * bexp, kb_ * eg], axis=1),
                qg=(qc * eg).astype(BF16),
                kq=jnp.concatenate([kb_, qc], axis=0).astype(BF16),
                Z=_bd_rows(kc, left1),
                dec=jnp.where(tri, jnp.exp(jnp.minimum(gcc - grow, 0.0)), 0.0),
                strict=strict,
                kg=(kT_cache[(d, p)][:, rs] * fac).astype(BF16),
                egl=jnp.exp(glast)))
        for ch in chains:
            kkqk = _dot_nt(ch["kq"], ch["Z"])
            ch["N"] = jnp.where(ch["strict"], -kkqk[:C] * ch["dec"], 0.0)
            ch["qk"] = (kkqk[C:] * ch["dec"]).astype(BF16)
        for ch in chains:
            ch["Pm"] = eye + ch["N"]
        for j in range(6):
            for ch in chains:
                nb = _bd_rows(ch["N"], left1)
                if j == 0:
                    ch["N"] = _dot(ch["N"].astype(BF16), nb)
                elif j < 5:
                    both = _dot(jnp.concatenate([ch["Pm"], ch["N"]], axis=0).astype(BF16), nb)
                    ch["Pm"] = ch["Pm"] + both[:C]
                    ch["N"] = both[C:]
                else:
                    ch["Pm"] = ch["Pm"] + _dot(ch["Pm"].astype(BF16), nb)
        out = {}
        for ch in chains:
            X = _dot(ch["Pm"].astype(BF16), _bd_rows(ch["rhs"], left2))
            out[ch["key"]] = dict(
                u=X[:, :LANES], qk=ch["qk"], kg=ch["kg"], egl=ch["egl"],
                wq=jnp.concatenate([X[:, LANES:].astype(BF16), ch["qg"]], axis=0))
        return out

    by_key = {}
    group_chunks = 4
    for d in range(2):
        for c0 in range(0, G, group_chunks):
            by_key.update(prepare([(d, p, c) for p in range(P) for c in range(c0, min(c0 + group_chunks, G))]))

    for step in range(G):
        live = []
        for d, *_refs, o_ref in dirs:
            c = step if d == 0 else G - 1 - step
            for p in range(P):
                live.append((d * P + p, o_ref, slice(c * C, (c + 1) * C), p, by_key[(d, p, c)]))
        S = [st_ref[sidx] for sidx, *_ in live]
        t1 = [_dot(ch["wq"], S[i].astype(BF16)) for i, (*_, ch) in enumerate(live)]
        vn = [ch["u"] - t1[i][:C] for i, (*_, ch) in enumerate(live)]
        for i, (sidx, o_ref, rs, p, ch) in enumerate(live):
            o_ref[0, rs, p * LANES:(p + 1) * LANES] = t1[i][C:] + _dot(ch["qk"], _bd_rows(vn[i], left1))
        for i, (sidx, o_ref, rs, p, ch) in enumerate(live):
            upd = _dot(ch["kg"], vn[i].astype(BF16))
            decay = jnp.where(top, ch["egl"][:, 0:1], ch["egl"][:, C:C + 1])
            st_ref[sidx] = S[i] * decay + jnp.where(bdmask, upd, 0.0)


def _dn_scan(q, k, v, bg, gt, *, G=8):
    B, S, _ = q.shape
    G = min(G, S // DN_CHUNK)
    R = G * DN_CHUNK
    nb = S // R
    n_ba = 4 * DN_HEADS
    P = DN_HEADS // 2
    fw = lambda n: pl.BlockSpec((1, R, n), lambda b, i: (b, i, 0))
    bw = lambda n: pl.BlockSpec((1, R, n), lambda b, i: (b, nb - 1 - i, 0))
    fwt = pl.BlockSpec((1, n_ba, R), lambda b, i: (b, 0, i))
    bwt = pl.BlockSpec((1, n_ba, R), lambda b, i: (b, 0, nb - 1 - i))
    return pl.pallas_call(
        functools.partial(_dnscan_body, G=G),
        grid=(B, nb),
        in_specs=[fw(DN_WIDTH), fw(DN_WIDTH), fw(DN_WIDTH), fw(BA_PAD), fwt,
                  bw(DN_WIDTH), bw(DN_WIDTH), bw(DN_WIDTH), bw(BA_PAD), bwt],
        out_specs=[fw(DN_WIDTH), bw(DN_WIDTH)],
        out_shape=[jax.ShapeDtypeStruct((B, S, DN_WIDTH), F32)] * 2,
        scratch_shapes=[pltpu.VMEM((2 * P, LANES, LANES), F32)],
        compiler_params=pltpu.CompilerParams(dimension_semantics=("parallel", "arbitrary")),
        name="dn_scan",
    )(q, k, v, bg, gt, q, k, v, bg, gt)


def _outproj_body(x_ref, ow_ref, od_ref, of_ref, ob_ref, z_ref, g_ref, w_ref, o_ref, cat_ref):
    oc = of_ref[0] + ob_ref[0]
    on = oc * lax.rsqrt(_seg_sum64(oc * oc) * (1.0 / HEAD_DIM) + EPS) * g_ref[...] * _silu(z_ref[0])
    cat_ref[:, :WIN_WIDTH] = ow_ref[0]
    cat_ref[:, WIN_WIDTH:WIN_WIDTH + DIFF_WIDTH] = od_ref[0]
    cat_ref[:, WIN_WIDTH + DIFF_WIDTH:] = on.astype(BF16)
    o_ref[0] = x_ref[0] + _dot(cat_ref[...], w_ref[...])


def _outproj(x, o_win, o_diff, o_f, o_b, z, dn_g, w_out, *, tm=512):
    B, S, _ = x.shape
    tm = min(tm, S)
    row = lambda n: pl.BlockSpec((1, tm, n), lambda b, i: (b, i, 0))
    g = jnp.tile(dn_g.astype(F32), DN_HEADS).reshape(1, DN_WIDTH)
    return pl.pallas_call(
        _outproj_body,
        grid=(B, S // tm),
        in_specs=[row(D_MODEL), row(WIN_WIDTH), row(DIFF_WIDTH), row(DN_WIDTH), row(DN_WIDTH),
                  row(DN_WIDTH), _resident((1, DN_WIDTH)), _resident((MIX_WIDTH, D_MODEL))],
        out_specs=row(D_MODEL),
        out_shape=jax.ShapeDtypeStruct((B, S, D_MODEL), F32),
        scratch_shapes=[pltpu.VMEM((tm, MIX_WIDTH), BF16)],
        compiler_params=pltpu.CompilerParams(
            dimension_semantics=("parallel", "parallel"), vmem_limit_bytes=VMEM_LIMIT),
        name="outproj",
    )(x, o_win, o_diff, o_f, o_b, z, g, w_out.astype(BF16))


def _mixer(x, ln_mix, w_mix_in, conv_w, sink, diff_lam, diff_g, a_log, dt_bias, dn_g, w_mix_out, lam_init):
    wq, wk, wv, dq, dkt, dv, dn_raw, z, ba, bat = _inproj(x, ln_mix, w_mix_in)
    win_slopes = 2.0 ** (-8.0 * jnp.arange(1, WIN_Q_HEADS + 1, dtype=F32) / WIN_Q_HEADS)
    o_win = _win_attention(wq, wk, wv, sink, win_slopes)
    o_diff = _diff_attention(dq, dkt, dv, diff_lam, diff_g, lam_init)
    q, k, v, bg, gt = _dn_prep(dn_raw, ba, bat, conv_w, a_log, dt_bias)
    o_f, o_b = _dn_scan(q, k, v, bg, gt)
    return _outproj(x, o_win, o_diff, o_f, o_b, z, dn_g, w_mix_out)


def kernel(x, ln_ffn1, ffn1_w_in, ffn1_w_out, ln_mix, w_mix_in, conv_w, sink_logits, diff_lambda,
           diff_norm_g, dn_A_log, dn_dt_bias, dn_norm_g, w_mix_out, ln_ffn2, ffn2_w_in, ffn2_w_out,
           ln_final):
    B, S, D = x.shape
    depth = ln_ffn1.shape[0]
    for l in range(depth):
        lam_init = 0.8 - 0.6 * math.exp(-0.3 * l)
        x = _ffn(x.reshape(B * S, D), ln_ffn1[l], ffn1_w_in[l], ffn1_w_out[l]).reshape(B, S, D)
        x = _mixer(x, ln_mix[l], w_mix_in[l], conv_w[l], sink_logits[l], diff_lambda[l], diff_norm_g[l],
                   dn_A_log[l], dn_dt_bias[l], dn_norm_g[l], w_mix_out[l], lam_init)
        g_final = ln_final if l == depth - 1 else None
        x = _ffn(x.reshape(B * S, D), ln_ffn2[l], ffn2_w_in[l], ffn2_w_out[l], g_final).reshape(B, S, D)
    return x
```

```python
import functools
import math

import jax
import jax.numpy as jnp
from jax import lax
from jax.experimental import pallas as pl
from jax.experimental.pallas import tpu as pltpu

F32 = jnp.float32
BF16 = jnp.bfloat16

D_MODEL = 1024
HEAD_DIM = 64
EPS = 1e-6
WIN_Q_HEADS = 6
WIN_KV_HEADS = 2
WIN_GROUP = WIN_Q_HEADS // WIN_KV_HEADS
WINDOW = 128
DIFF_HEADS = 4
DIFF_QK_DIM = HEAD_DIM // 2
DN_HEADS = 6
DN_CHUNK = 64
CONV_W = 5
D_FF = 2752
WIN_WIDTH = WIN_Q_HEADS * HEAD_DIM
WIN_KV_WIDTH = WIN_KV_HEADS * HEAD_DIM
DIFF_WIDTH = DIFF_HEADS * HEAD_DIM
DN_WIDTH = DN_HEADS * HEAD_DIM
DN_QKV = 3 * DN_WIDTH
MIX_WIDTH = WIN_WIDTH + DIFF_WIDTH + DN_WIDTH
OFF_WQ = 0
OFF_WK = OFF_WQ + WIN_WIDTH
OFF_WV = OFF_WK + WIN_KV_WIDTH
OFF_DQ = OFF_WV + WIN_KV_WIDTH
OFF_DK = OFF_DQ + DIFF_WIDTH
OFF_DV = OFF_DK + DIFF_WIDTH
OFF_DN = OFF_DV + DIFF_WIDTH
OFF_Z = OFF_DN + DN_QKV
OFF_BA = OFF_Z + DN_WIDTH
MIX_IN = OFF_BA + 4 * DN_HEADS

LANES = 128
FF_CHUNK = 256
D_FF_PAD = -(-D_FF // FF_CHUNK) * FF_CHUNK
VMEM_LIMIT = 56 * 1024 * 1024
BF16_ROWS = 16
LOG2E = math.log2(math.e)
DIFF_TK = 512
DIFF_KROWS = 48
DIFF_BIAS_ROWS = 3
DIFF_KT_ROWS = 2 * DIFF_HEADS * DIFF_KROWS
DN_SCAN_GROUPS = 3
DIFF_V_WIDTH = DIFF_HEADS * LANES
DIFF_SKIP_LOG2 = 150.0
DIFF_SLOPES = tuple(2.0 ** (-8.0 * (i + 1) / DIFF_HEADS) for i in range(DIFF_HEADS))


def _rms(x, g):
    return x * lax.rsqrt(jnp.mean(x * x, axis=-1, keepdims=True) + EPS) * g


def _silu(x):
    h = 0.5 * x
    return h + h * jnp.tanh(h)


def _seg_sum64(x):
    R, width = x.shape
    n = width // LANES
    rows = jnp.concatenate([x[:, c * LANES:(c + 1) * LANES] for c in range(n)], axis=0)
    r = lax.broadcasted_iota(jnp.int32, (LANES, LANES), 0) < HEAD_DIM
    c = lax.broadcasted_iota(jnp.int32, (LANES, LANES), 1) < HEAD_DIM
    ones_bd = (r == c).astype(F32).astype(BF16)
    hi = rows.astype(BF16)
    r1 = rows - hi.astype(F32)
    mid = r1.astype(BF16)
    lo = (r1 - mid.astype(F32)).astype(BF16)
    ss = _dot(hi, ones_bd) + _dot(mid, ones_bd) + _dot(lo, ones_bd)
    return jnp.concatenate([ss[c * R:(c + 1) * R] for c in range(n)], axis=1)


def _dot(a, b):
    return jnp.dot(a, b, preferred_element_type=F32)


def _dot_nt(a, b):
    return lax.dot_general(a, b, (((1,), (1,)), ((), ())), preferred_element_type=F32)


def _resident(shape):
    nd = len(shape)
    return pl.BlockSpec(shape, lambda *_: (0,) * nd, pipeline_mode=pl.Buffered(1))


def _ffn_body(x_ref, g_ref, wgu_ref, wo_ref, *rest, n_chunks, final):
    if final:
        gf_ref, o_ref, acc_ref = rest
    else:
        o_ref, acc_ref = rest
    x = x_ref[...]
    h = _rms(x, g_ref[...]).astype(BF16)
    for c in range(n_chunks):
        gu = _dot(h, wgu_ref[:, c * 2 * FF_CHUNK:(c + 1) * 2 * FF_CHUNK])
        a = (_silu(gu[:, :FF_CHUNK]) * gu[:, FF_CHUNK:]).astype(BF16)
        part = _dot(a, wo_ref[c * FF_CHUNK:(c + 1) * FF_CHUNK, :])
        if c == 0:
            acc_ref[...] = part
        else:
            acc_ref[...] += part
    y = x + 0.5 * acc_ref[...]
    if final:
        y = _rms(y, gf_ref[...])
    o_ref[...] = y


def _prep_ffn_weights(w_in, w_out):
    n = D_FF_PAD // FF_CHUNK
    pad = D_FF_PAD - D_FF
    w_in = w_in.astype(BF16)
    wg = jnp.pad(w_in[:, :D_FF], ((0, 0), (0, pad))).reshape(D_MODEL, n, FF_CHUNK)
    wu = jnp.pad(w_in[:, D_FF:], ((0, 0), (0, pad))).reshape(D_MODEL, n, FF_CHUNK)
    wgu = jnp.concatenate([wg, wu], axis=2).reshape(D_MODEL, n * 2 * FF_CHUNK)
    wo = jnp.pad(w_out.astype(BF16), ((0, pad), (0, 0)))
    return wgu, wo


def _ffn(x2d, g, w_in, w_out, g_final=None, *, tm=512):
    T = x2d.shape[0]
    tm = min(tm, T)
    n_chunks = D_FF_PAD // FF_CHUNK
    wgu, wo = _prep_ffn_weights(w_in, w_out)
    final = g_final is not None
    in_specs = [
        pl.BlockSpec((tm, D_MODEL), lambda i: (i, 0)),
        _resident((1, D_MODEL)),
        _resident(wgu.shape),
        _resident(wo.shape),
    ]
    args = [x2d, g.reshape(1, D_MODEL), wgu, wo]
    if final:
        in_specs.append(_resident((1, D_MODEL)))
        args.append(g_final.reshape(1, D_MODEL))
    return pl.pallas_call(
        functools.partial(_ffn_body, n_chunks=n_chunks, final=final),
        grid=(T // tm,),
        in_specs=in_specs,
        out_specs=pl.BlockSpec((tm, D_MODEL), lambda i: (i, 0)),
        out_shape=jax.ShapeDtypeStruct((T, D_MODEL), F32),
        scratch_shapes=[pltpu.VMEM((tm, D_MODEL), F32)],
        compiler_params=pltpu.CompilerParams(
            dimension_semantics=("parallel",), vmem_limit_bytes=VMEM_LIMIT),
        name="ffn_final" if final else "ffn",
    )(*args)


BA_PAD = LANES


def _inproj_body(x_ref, g_ref, w_ref, wkt_ref, wbat_ref, wdv_ref,
                 wq_ref, wk_ref, wv_ref, dq_ref, dkt_ref, dv_ref, dn_ref, z_ref, ba_ref, bat_ref):
    h = _rms(x_ref[0], g_ref[...]).astype(BF16)
    wq_ref[0] = (_dot(h, w_ref[:, OFF_WQ:OFF_WK]) * (HEAD_DIM ** -0.5 * LOG2E)).astype(BF16)
    wk_ref[0] = _dot(h, w_ref[:, OFF_WK:OFF_WV]).astype(BF16)
    wv_ref[0] = _dot(h, w_ref[:, OFF_WV:OFF_DQ]).astype(BF16)
    dq_ref[0] = (_dot(h, w_ref[:, OFF_DQ:OFF_DK]) * (DIFF_QK_DIM ** -0.5 * LOG2E)).astype(BF16)
    kt = _dot_nt(wkt_ref[...], h)
    tm = kt.shape[1]
    extra_rows = DIFF_KROWS - DIFF_QK_DIM
    pos = pl.program_id(1) * tm + lax.broadcasted_iota(jnp.int32, (extra_rows, tm), 1)
    kr = (pos % DIFF_TK).astype(F32)
    rowi = lax.broadcasted_iota(jnp.int32, (extra_rows, tm), 0)
    for hd in range(DIFF_HEADS):
        bias = (DIFF_SLOPES[hd] * LOG2E) * kr
        hi = bias.astype(BF16).astype(F32)
        mid = (bias - hi).astype(BF16).astype(F32)
        lo = bias - hi - mid
        extra = jnp.where(rowi == 0, hi, jnp.where(rowi == 1, mid, jnp.where(rowi == 2, lo, 0.0))).astype(BF16)
        for mp in range(2):
            hm = 2 * hd + mp
            dkt_ref[0, hm * DIFF_KROWS:hm * DIFF_KROWS + DIFF_QK_DIM, :] = (
                kt[hm * DIFF_QK_DIM:(hm + 1) * DIFF_QK_DIM, :].astype(BF16))
            dkt_ref[0, hm * DIFF_KROWS + DIFF_QK_DIM:(hm + 1) * DIFF_KROWS, :] = extra
    dv = _dot(h, wdv_ref[...])
    ones_half = (lax.broadcasted_iota(jnp.int32, dv.shape, 1) % LANES) >= HEAD_DIM
    dv_ref[0] = jnp.where(ones_half, 1.0, dv).astype(BF16)
    dn_ref[0] = _dot(h, w_ref[:, OFF_DN:OFF_Z])
    z_ref[0] = _dot(h, w_ref[:, OFF_Z:OFF_BA])
    ba_ref[0] = _dot(h, w_ref[:, OFF_BA:OFF_BA + BA_PAD])
    bat_ref[0] = _dot_nt(wbat_ref[...], h)


def _inproj(x, g, w_in, *, tm=512):
    B, S, _ = x.shape
    tm = min(tm, S)
    n_ba = 4 * DN_HEADS
    w_in = w_in.astype(BF16)
    w = jnp.pad(w_in, ((0, 0), (0, OFF_BA + BA_PAD - MIX_IN)))
    wkt = w_in[:, OFF_DK:OFF_DV].T
    wbat = w_in[:, OFF_BA:MIX_IN].T
    wdv = jnp.pad(w_in[:, OFF_DV:OFF_DN].reshape(D_MODEL, DIFF_HEADS, HEAD_DIM),
                  ((0, 0), (0, 0), (0, LANES - HEAD_DIM))).reshape(D_MODEL, DIFF_V_WIDTH)
    row = lambda n: pl.BlockSpec((1, tm, n), lambda b, i: (b, i, 0))
    col = lambda n: pl.BlockSpec((1, n, tm), lambda b, i: (b, 0, i))
    out_shapes = [
        jax.ShapeDtypeStruct((B, S, WIN_WIDTH), BF16),
        jax.ShapeDtypeStruct((B, S, WIN_KV_WIDTH), BF16),
        jax.ShapeDtypeStruct((B, S, WIN_KV_WIDTH), BF16),
        jax.ShapeDtypeStruct((B, S, DIFF_WIDTH), BF16),
        jax.ShapeDtypeStruct((B, DIFF_KT_ROWS, S), BF16),
        jax.ShapeDtypeStruct((B, S, DIFF_V_WIDTH), BF16),
        jax.ShapeDtypeStruct((B, S, DN_QKV), F32),
        jax.ShapeDtypeStruct((B, S, DN_WIDTH), F32),
        jax.ShapeDtypeStruct((B, S, BA_PAD), F32),
        jax.ShapeDtypeStruct((B, n_ba, S), F32),
    ]
    out_specs = [row(WIN_WIDTH), row(WIN_KV_WIDTH), row(WIN_KV_WIDTH), row(DIFF_WIDTH),
                 col(DIFF_KT_ROWS), row(DIFF_V_WIDTH), row(DN_QKV), row(DN_WIDTH), row(BA_PAD), col(n_ba)]
    return pl.pallas_call(
        _inproj_body,
        grid=(B, S // tm),
        in_specs=[row(D_MODEL), _resident((1, D_MODEL)), _resident(w.shape),
                  _resident(wkt.shape), _resident(wbat.shape), _resident(wdv.shape)],
        out_specs=out_specs,
        out_shape=out_shapes,
        compiler_params=pltpu.CompilerParams(
            dimension_semantics=("parallel", "parallel"), vmem_limit_bytes=VMEM_LIMIT),
        name="inproj",
    )(x, g.reshape(1, D_MODEL), w, wkt, wbat, wdv)


def _win_body(q_ref, kp_ref, kc_ref, kn_ref, vp_ref, vc_ref, vn_ref, sink_ref, slope_ref, o_ref,
              *, n_steps, nq):
    i = pl.program_id(1)
    W = WINDOW
    KW = 3 * W
    kw = jnp.concatenate([kp_ref[0], kc_ref[0], kn_ref[0]], axis=0).astype(F32)
    vw = jnp.concatenate([vp_ref[0], vc_ref[0], vn_ref[0]], axis=0).astype(F32)
    kw_sw = pltpu.roll(kw, HEAD_DIM, 1)
    vw_sw = pltpu.roll(vw, HEAD_DIM, 1)
    left = lax.broadcasted_iota(jnp.int32, (KW, LANES), 1) < HEAD_DIM
    qi = lax.broadcasted_iota(jnp.int32, (W, KW), 0)
    ki = lax.broadcasted_iota(jnp.int32, (W, KW), 1)
    dist = jnp.abs(ki - W - qi)
    band_bias = [jnp.where(dist <= W, (-LOG2E) * slope_ref[:, hq:hq + 1] * dist.astype(F32), -1e30)
                 for hq in range(WIN_Q_HEADS)]
    edge_lo = jnp.where(jnp.logical_or(ki >= W, i > 0), 0.0, -1e30)
    edge_hi = jnp.where(jnp.logical_or(ki < 2 * W, i < n_steps - 1), 0.0, -1e30)
    n_pairs = WIN_Q_HEADS // 2
    ones_stack = jnp.concatenate([jnp.where(left, 1.0, 0.0), jnp.where(left, 0.0, 1.0)], axis=0).astype(BF16)
    blocks = []
    for a in range(nq):
        rows = slice(a * W, a * W + KW)

        def stack(first, second, rows=rows):
            return jnp.concatenate([jnp.where(left, first[rows], 0.0),
                                    jnp.where(left, 0.0, second[rows])], axis=0).astype(BF16)

        zk = (stack(kw, kw_sw), stack(kw, kw), stack(kw_sw, kw))
        zv = tuple(jnp.concatenate([z, ones_stack], axis=1)
                   for z in (stack(vw, vw_sw), stack(vw, vw), stack(vw_sw, vw)))
        edge = None
        if a == 0:
            edge = edge_lo
        if a == nq - 1:
            edge = edge_hi if edge is None else edge + edge_hi
        q = q_ref[0, a * W:(a + 1) * W, :]
        s2 = [_dot_nt(q[:, p * LANES:(p + 1) * LANES], zk[p]) for p in range(n_pairs)]
        blocks.append((a, edge, zv, s2))
    weights, sink_terms = {}, {}
    for a, edge, zv, s2 in blocks:
        for hq in range(WIN_Q_HEADS):
            p, t = divmod(hq, 2)
            s = s2[p][:, t * KW:(t + 1) * KW] + band_bias[hq]
            if edge is not None:
                s = s + edge
            sink = LOG2E * sink_ref[:, hq:hq + 1]
            m = jnp.maximum(jnp.max(s, axis=-1, keepdims=True), sink)
            weights[(a, hq)] = jnp.exp2(s - m).astype(BF16)
            sink_terms[(a, hq)] = jnp.exp2(sink - m)
    left_w = lax.broadcasted_iota(jnp.int32, (W, LANES), 1) < HEAD_DIM
    for a, edge, zv, s2 in blocks:
        for p in range(n_pairs):
            ee = jnp.concatenate([weights[(a, 2 * p)], weights[(a, 2 * p + 1)]], axis=1)
            acc = _dot(ee, zv[p])
            denom = acc[:, LANES:] + jnp.where(left_w, sink_terms[(a, 2 * p)], sink_terms[(a, 2 * p + 1)])
            o_ref[0, a * W:(a + 1) * W, p * LANES:(p + 1) * LANES] = (acc[:, :LANES] / denom).astype(BF16)


def _win_attention(q, k, v, sink, slopes, *, nq=8):
    B, S, _ = q.shape
    W = WINDOW
    nq = min(nq, S // W)
    ns = S // (nq * W)
    nb = S // W
    qspec = pl.BlockSpec((1, nq * W, WIN_WIDTH), lambda b, i: (b, i, 0))
    prev = pl.BlockSpec((1, W, WIN_KV_WIDTH), lambda b, i: (b, jnp.maximum(i * nq - 1, 0), 0))
    cur = pl.BlockSpec((1, nq * W, WIN_KV_WIDTH), lambda b, i: (b, i, 0))
    nxt = pl.BlockSpec((1, W, WIN_KV_WIDTH), lambda b, i: (b, jnp.minimum((i + 1) * nq, nb - 1), 0))
    return pl.pallas_call(
        functools.partial(_win_body, n_steps=ns, nq=nq),
        grid=(B, ns),
        in_specs=[qspec, prev, cur, nxt, prev, cur, nxt,
                  _resident((1, WIN_Q_HEADS)), _resident((1, WIN_Q_HEADS))],
        out_specs=pl.BlockSpec((1, nq * W, WIN_WIDTH), lambda b, i: (b, i, 0)),
        out_shape=jax.ShapeDtypeStruct((B, S, WIN_WIDTH), BF16),
        compiler_params=pltpu.CompilerParams(dimension_semantics=("parallel", "parallel")),
        name="win_attn",
    )(q, k, k, k, v, v, v, sink.reshape(1, WIN_Q_HEADS).astype(F32), slopes.reshape(1, WIN_Q_HEADS))


def _diff_ranges(q, kt, tq, tk):
    B, S, _ = q.shape
    nq, nk, n_maps = S // tq, S // tk, 2 * DIFF_HEADS
    qf = q.astype(F32).reshape(B, nq, tq, n_maps, DIFF_QK_DIM)
    qn = jnp.sqrt(jnp.max(jnp.sum(qf * qf, axis=-1), axis=2))
    kf = kt.astype(F32).reshape(B, n_maps, DIFF_KROWS, S)[:, :, :DIFF_QK_DIM]
    kn_pos = jnp.sqrt(jnp.sum(kf * kf, axis=2))
    kn = jnp.max(kn_pos.reshape(B, n_maps, nk, tk), axis=-1)
    kown = jnp.max(kn_pos.reshape(B, n_maps, nq, tq), axis=-1)
    reach = jnp.swapaxes(qn, 1, 2)[..., None] * (kn[:, :, None, :] + kown[..., None]) * (1.0 + 1e-3)
    reach = jnp.max(reach.reshape(B, DIFF_HEADS, 2, nq, nk), axis=2)
    q0 = jnp.arange(nq, dtype=jnp.int32) * tq
    k0 = jnp.arange(nk, dtype=jnp.int32) * tk
    dmin = jnp.maximum(jnp.maximum(k0[None, :] - (q0[:, None] + tq - 1), q0[:, None] - (k0[None, :] + tk - 1)), 0)
    c = jnp.asarray(DIFF_SLOPES, F32) * LOG2E
    need = reach - c[None, :, None, None] * dmin.astype(F32)[None, None] > -DIFF_SKIP_LOG2
    idx = jnp.arange(nk, dtype=jnp.int32)
    need = jnp.logical_or(need, (idx[None, :] == (q0 // tk)[:, None])[None, None])
    lo = jnp.min(jnp.where(need, idx, nk), axis=-1)
    hi = jnp.max(jnp.where(need, idx, -1), axis=-1)
    return jnp.stack([lo, hi], axis=-1).transpose(0, 2, 1, 3).reshape(-1).astype(jnp.int32)


def _diff_body(rng_ref, q_ref, kt_ref, v_ref, lam_ref, g_ref, o_ref, qv_ref, m_ref, acc_ref,
               s0_ref, st0_ref, s1_ref, st1_ref, *, tq, tk, lam_init):
    n_maps = 2 * DIFF_HEADS
    n_groups = tk // LANES
    q0 = pl.program_id(1) * tq
    jd = q0 // tk
    q = q_ref[0]
    lane_x = lax.broadcasted_iota(jnp.int32, (tq, DIFF_KROWS - DIFF_QK_DIM), 1)
    for var, sign in enumerate((1.0, -1.0, 0.0)):
        extras = jnp.where(lane_x < DIFF_BIAS_ROWS, -sign, 0.0).astype(BF16)
        for hm in range(n_maps):
            qv_ref[var, hm, :, 0:DIFF_QK_DIM] = q[:, hm * DIFF_QK_DIM:(hm + 1) * DIFF_QK_DIM]
            qv_ref[var, hm, :, DIFF_QK_DIM:DIFF_KROWS] = extras
    m_ref[...] = jnp.full(m_ref.shape, -jnp.inf, F32)
    acc_ref[...] = jnp.zeros(acc_ref.shape, F32)
    qpos = (q0 + lax.broadcasted_iota(jnp.int32, (tq, LANES), 0)).astype(F32)
    bufs = ((s0_ref, st0_ref), (s1_ref, st1_ref))
    rng_base = (pl.program_id(0) * pl.num_programs(1) + pl.program_id(1)) * (2 * DIFF_HEADS)

    los = [rng_ref[rng_base + 2 * h] for h in range(DIFF_HEADS)]
    n_offs = [rng_ref[rng_base + 2 * h + 1] - los[h] for h in range(DIFF_HEADS)]
    starts = [0]
    for h in range(DIFF_HEADS):
        starts.append(starts[-1] + n_offs[h])
    n_total = starts[DIFF_HEADS]

    def pick(h, vals):
        out = vals[-1]
        for i in range(len(vals) - 2, -1, -1):
            out = jnp.where(h == i, vals[i], out)
        return out

    def locate(u):
        h = sum((u >= starts[i]).astype(jnp.int32) for i in range(1, DIFF_HEADS))
        t = pick(h, los) + u - pick(h, starts[:DIFF_HEADS])
        return h, t + (t >= jd).astype(jnp.int32)

    def scores(h, j, buf, diag_dist=None):
        s_ref, st_ref = bufs[buf]
        if diag_dist is not None:
            var = 2
            bias = (DIFF_SLOPES[h] * LOG2E) * diag_dist
        else:
            c = pick(h, [sl * LOG2E for sl in DIFF_SLOPES])
            after = j > jd
            var = jnp.where(after, 0, 1)
            rowoff = jnp.where(after, c, -c) * ((j * tk).astype(F32) - qpos)
        k0 = pl.multiple_of(j * tk, tk)
        for mp in range(2):
            hm = 2 * h + mp
            r0 = hm * DIFF_KROWS if diag_dist is not None else pl.multiple_of(hm * DIFF_KROWS, BF16_ROWS)
            kt = kt_ref[0, pl.ds(r0, DIFF_KROWS), pl.ds(k0, tk)]
            s = _dot(qv_ref[var, hm], kt)
            if diag_dist is not None:
                s = s - bias
            s_ref[mp] = s
            mx = s[:, 0:LANES]
            for g in range(1, n_groups):
                mx = jnp.maximum(mx, s[:, g * LANES:(g + 1) * LANES])
            mrow = jnp.broadcast_to(jnp.max(mx, axis=-1, keepdims=True), (tq, LANES))
            m_old = m_ref[hm]
            if diag_dist is not None:
                m_new = jnp.maximum(m_old, mrow)
                shift = m_new
            else:
                m_new = jnp.maximum(m_old, mrow - rowoff)
                shift = m_new + rowoff
            st_ref[mp, 0] = shift
            st_ref[mp, 1] = jnp.exp2(m_old - m_new)
            m_ref[hm] = m_new

    def values(h, j, buf, static=False):
        s_ref, st_ref = bufs[buf]
        k0 = pl.multiple_of(j * tk, tk)
        c0 = h * LANES if static else pl.multiple_of(h * LANES, LANES)
        v2 = v_ref[0, pl.ds(k0, tk), pl.ds(c0, LANES)]
        for mp in range(2):
            hm = 2 * h + mp
            shift = st_ref[mp, 0]
            alpha = st_ref[mp, 1]
            e = jnp.exp2(s_ref[mp] - jnp.concatenate([shift] * n_groups, axis=1))
            acc_ref[hm] = alpha * acc_ref[hm] + _dot(e.astype(BF16), v2)

    kpos = (jd * tk + lax.broadcasted_iota(jnp.int32, (tq, tk), 1)).astype(F32)
    qp = (q0 + lax.broadcasted_iota(jnp.int32, (tq, tk), 0)).astype(F32)
    diag_dist = jnp.abs(qp - kpos)
    scores(0, jd, 0, diag_dist)
    for h in range(DIFF_HEADS):
        if h + 1 < DIFF_HEADS:
            scores(h + 1, jd, (h + 1) % 2, diag_dist)
        values(h, jd, h % 2, static=True)

    def s_at(u, buf):
        h, j = locate(u)
        scores(h, j, buf)

    def v_at(u, buf):
        h, j = locate(u)
        values(h, j, buf)

    @pl.when(n_total > 0)
    def _():
        s_at(0, 0)

    def pair(p, carry):
        u = 2 * p
        s_at(u + 1, 1)
        v_at(u, 0)
        s_at(u + 2, 0)
        v_at(u + 1, 1)
        return carry

    n_pairs = jnp.maximum(n_total - 1, 0) // 2
    lax.fori_loop(0, n_pairs, pair, 0)
    u_last = 2 * n_pairs
    left_over = n_total - u_last

    @pl.when(left_over == 2)
    def _():
        s_at(u_last + 1, 1)
        v_at(u_last, 0)
        v_at(u_last + 1, 1)

    @pl.when(left_over == 1)
    def _():
        v_at(u_last, 0)

    lp = lam_ref[...]
    lam = (jnp.exp(jnp.sum(lp[0:1] * lp[1:2], axis=-1, keepdims=True))
           - jnp.exp(jnp.sum(lp[2:3] * lp[3:4], axis=-1, keepdims=True)) + lam_init)
    lane = lax.broadcasted_iota(jnp.int32, (tq, LANES), 1)
    left = lane < HEAD_DIM
    ys = []
    for h in range(DIFF_HEADS):
        a1, a2 = acc_ref[2 * h], acc_ref[2 * h + 1]
        od = a1 / pltpu.roll(a1, HEAD_DIM, 1) - lam * (a2 / pltpu.roll(a2, HEAD_DIM, 1))
        ms = jnp.sum(jnp.where(left, od * od, 0.0), axis=-1, keepdims=True) * (1.0 / HEAD_DIM)
        ys.append(od * lax.rsqrt(ms + EPS) * g_ref[...] * (1.0 - lam_init))
    for p in range(DIFF_HEADS // 2):
        o_ref[0, :, p * LANES:(p + 1) * LANES] = jnp.where(
            left, ys[2 * p], pltpu.roll(ys[2 * p + 1], HEAD_DIM, 1)).astype(BF16)


def _diff_attention(q, kt, v, diff_lambda, diff_g, lam_init, *, tq=512):
    B, S, _ = q.shape
    tk = DIFF_TK
    tq = min(tq, S)
    assert S % tk == 0 and tk % tq == 0
    n_maps = 2 * DIFF_HEADS
    g2 = jnp.concatenate([diff_g, diff_g]).reshape(1, LANES).astype(F32)
    grid_spec = pltpu.PrefetchScalarGridSpec(
        num_scalar_prefetch=1,
        grid=(B, S // tq),
        in_specs=[pl.BlockSpec((1, tq, DIFF_WIDTH), lambda b, i, r: (b, i, 0)),
                  pl.BlockSpec((1, DIFF_KT_ROWS, S), lambda b, i, r: (b, 0, 0)),
                  pl.BlockSpec((1, S, DIFF_V_WIDTH), lambda b, i, r: (b, 0, 0)),
                  pl.BlockSpec((4, DIFF_QK_DIM), lambda b, i, r: (0, 0)),
                  pl.BlockSpec((1, LANES), lambda b, i, r: (0, 0))],
        out_specs=pl.BlockSpec((1, tq, DIFF_WIDTH), lambda b, i, r: (b, i, 0)),
        scratch_shapes=[pltpu.VMEM((3, n_maps, tq, DIFF_KROWS), BF16),
                        pltpu.VMEM((n_maps, tq, LANES), F32),
                        pltpu.VMEM((n_maps, tq, LANES), F32),
                        pltpu.VMEM((2, tq, tk), F32), pltpu.VMEM((2, 2, tq, LANES), F32),
                        pltpu.VMEM((2, tq, tk), F32), pltpu.VMEM((2, 2, tq, LANES), F32)])
    return pl.pallas_call(
        functools.partial(_diff_body, tq=tq, tk=tk, lam_init=lam_init),
        grid_spec=grid_spec,
        out_shape=jax.ShapeDtypeStruct((B, S, DIFF_WIDTH), BF16),
        compiler_params=pltpu.CompilerParams(
            dimension_semantics=("parallel", "arbitrary"), vmem_limit_bytes=VMEM_LIMIT),
        name="diff_attn",
    )(_diff_ranges(q, kt, tq, tk), q, kt, v, diff_lambda.astype(F32), g2)


def _l2n64(x):
    return x * lax.rsqrt(_seg_sum64(x * x) + EPS)


def _dnprep_body(x_ref, xp_ref, xn_ref, w_ref, ba_ref, bat_ref, ab_ref, abt_ref,
                 q_ref, k_ref, v_ref, bg_ref, gt_ref, *, ts, n_tiles):
    i = pl.program_id(1)
    H = DN_HEADS
    x = x_ref[0]
    xp = jnp.where(i > 0, xp_ref[0], 0.0)
    xn = jnp.where(i < n_tiles - 1, xn_ref[0], 0.0)
    xx = jnp.concatenate([xp, x, xn], axis=0)
    half = CONV_W // 2
    y = jnp.zeros_like(x)
    for j in range(CONV_W):
        y = y + w_ref[j:j + 1, :] * xx[8 - half + j:8 - half + j + ts, :]
    y = _silu(y)
    q_ref[0] = _l2n64(y[:, :DN_WIDTH]) * (HEAD_DIM ** -0.5)
    k_ref[0] = _l2n64(y[:, DN_WIDTH:2 * DN_WIDTH])
    v_ref[0] = y[:, 2 * DN_WIDTH:]

    def gates(raw, a_log, dt_bias):
        beta = 1.0 / (1.0 + jnp.exp(-raw))
        z = raw + dt_bias
        softplus = jnp.maximum(z, 0.0) + jnp.log(1.0 + jnp.exp(-jnp.abs(z)))
        return beta, -jnp.exp(a_log) * softplus

    ba = ba_ref[0]
    beta, g = gates(ba, ab_ref[0:1, :], ab_ref[1:2, :])
    lane = lax.broadcasted_iota(jnp.int32, ba.shape, 1)
    r = lax.broadcasted_iota(jnp.int32, ba.shape, 0) % DN_CHUNK
    bwd = jnp.logical_and(lane >= 3 * H, lane < 4 * H)
    gc = g
    for d in (1, 2, 4, 8, 16, 32):
        f = jnp.where(r >= d, pltpu.roll(gc, d, 0), 0.0)
        b = jnp.where(r < DN_CHUNK - d, pltpu.roll(gc, ts - d, 0), 0.0)
        gc = gc + jnp.where(bwd, b, f)
    bg_ref[0] = jnp.where(lane < 2 * H, beta, gc)
    bat = bat_ref[0]
    _, gt = gates(bat, abt_ref[:, 0:1], abt_ref[:, 1:2])
    row = lax.broadcasted_iota(jnp.int32, bat.shape, 0)
    c = lax.broadcasted_iota(jnp.int32, bat.shape, 1) % DN_CHUNK
    bwd_t = row >= 3 * H
    for d in (1, 2, 4, 8, 16, 32):
        f = jnp.where(c >= d, pltpu.roll(gt, d, 1), 0.0)
        b = jnp.where(c < DN_CHUNK - d, pltpu.roll(gt, ts - d, 1), 0.0)
        gt = gt + jnp.where(bwd_t, b, f)
    gt_ref[0] = gt


def _dn_prep(dn_raw, ba, bat, conv_w, a_log, dt_bias, *, ts=256):
    B, S, C = dn_raw.shape
    ts = min(ts, S)
    nt = S // ts
    n_ba = 4 * DN_HEADS
    hb = ts // 8
    zeros = jnp.zeros((2 * DN_HEADS,), F32)
    al = jnp.concatenate([zeros, a_log.reshape(-1).astype(F32)])
    db = jnp.concatenate([zeros, dt_bias.reshape(-1).astype(F32)])
    ab = jnp.pad(jnp.stack([al, db]), ((0, 0), (0, BA_PAD - n_ba)))
    abt = jnp.stack([al, db], axis=1)
    row = lambda n: pl.BlockSpec((1, ts, n), lambda b, i: (b, i, 0))
    return pl.pallas_call(
        functools.partial(_dnprep_body, ts=ts, n_tiles=nt),
        grid=(B, nt),
        in_specs=[row(C),
                  pl.BlockSpec((1, 8, C), lambda b, i: (b, jnp.maximum(i * hb - 1, 0), 0)),
                  pl.BlockSpec((1, 8, C), lambda b, i: (b, jnp.minimum((i + 1) * hb, S // 8 - 1), 0)),
                  _resident((CONV_W, C)), row(BA_PAD),
                  pl.BlockSpec((1, n_ba, ts), lambda b, i: (b, 0, i)),
                  _resident((2, BA_PAD)), _resident((n_ba, 2))],
        out_specs=[row(DN_WIDTH), row(DN_WIDTH), row(DN_WIDTH), row(BA_PAD),
                   pl.BlockSpec((1, n_ba, ts), lambda b, i: (b, 0, i))],
        out_shape=[jax.ShapeDtypeStruct((B, S, DN_WIDTH), F32)] * 3
        + [jax.ShapeDtypeStruct((B, S, BA_PAD), F32), jax.ShapeDtypeStruct((B, n_ba, S), F32)],
        compiler_params=pltpu.CompilerParams(dimension_semantics=("parallel", "parallel")),
        name="dn_prep",
    )(dn_raw, dn_raw, dn_raw, conv_w.astype(F32), ba, bat, ab, abt)


def _bd_rows(y, left):
    return jnp.concatenate([jnp.where(left, y, 0.0), jnp.where(left, 0.0, y)], axis=0).astype(BF16)


def _dnscan_body(qf_ref, kf_ref, vf_ref, bgf_ref, gtf_ref, qb_ref, kb_ref, vb_ref, bgb_ref, gtb_ref,
                 of_ref, ob_ref, st_ref, *, G):
    C = DN_CHUNK
    H = DN_HEADS
    P = H // 2

    @pl.when(pl.program_id(1) == 0)
    def _():
        st_ref[...] = jnp.zeros(st_ref.shape, F32)

    lane1 = lax.broadcasted_iota(jnp.int32, (C, LANES), 1)
    row1 = lax.broadcasted_iota(jnp.int32, (C, LANES), 0)
    left1 = lane1 < C
    s_idx = lane1 % C
    eye = (row1 == s_idx).astype(F32)
    lane2 = lax.broadcasted_iota(jnp.int32, (C, 2 * LANES), 1)
    left2 = (lane2 % LANES) < C
    rr = lax.broadcasted_iota(jnp.int32, (LANES, LANES), 0)
    cc = lax.broadcasted_iota(jnp.int32, (LANES, LANES), 1)
    top = rr < C
    bdmask = (rr < C) == (cc < C)
    top_c = lax.broadcasted_iota(jnp.int32, (LANES, C), 0) < C

    dirs = ((0, qf_ref, kf_ref, vf_ref, bgf_ref, gtf_ref, of_ref),
            (1, qb_ref, kb_ref, vb_ref, bgb_ref, gtb_ref, ob_ref))
    kT_cache = {}
    by_key = {}

    def prepare(group):
        chains = []
        for d, p, c in group:
            _, q_ref, k_ref, v_ref, bg_ref, gt_ref, _ = dirs[d]
            tri = (row1 >= s_idx) if d == 0 else (row1 <= s_idx)
            strict = (row1 > s_idx) if d == 0 else (row1 < s_idx)
            last = C - 1 if d == 0 else 0
            sl = slice(p * LANES, (p + 1) * LANES)
            cb = d * H + 2 * p
            cg = 2 * H + d * H + 2 * p
            if (d, p) not in kT_cache:
                kT_cache[(d, p)] = k_ref[0, :, sl].T
            rs = slice(c * C, (c + 1) * C)
            qc, kc, vc = q_ref[0, rs, sl], k_ref[0, rs, sl], v_ref[0, rs, sl]
            bexp = jnp.where(left1, bg_ref[0, rs, cb:cb + 1], bg_ref[0, rs, cb + 1:cb + 2])
            gcc = jnp.where(left1, bg_ref[0, rs, cg:cg + 1], bg_ref[0, rs, cg + 1:cg + 2])
            g0 = gt_ref[0, cg:cg + 1, rs]
            g1 = gt_ref[0, cg + 1:cg + 2, rs]
            grow = jnp.concatenate([g0, g1], axis=1)
            glast = gcc[last:last + 1, :]
            gl0 = glast[:, 0:1]
            gl1 = glast[:, C:C + 1]
            eg = jnp.exp(gcc)
            kb_ = kc * bexp
            fac = jnp.exp(jnp.minimum(jnp.where(top_c, gl0 - g0, gl1 - g1), 0.0))
            chains.append(dict(
                key=(d, p, c),
                rhs=jnp.concatenate([vc * bexp, kb_ * eg], axis=1),
                qg=(qc * eg).astype(BF16),
                kq=jnp.concatenate([kb_, qc], axis=0).astype(BF16),
                Z=_bd_rows(kc, left1),
                dec=jnp.where(tri, jnp.exp(jnp.minimum(gcc - grow, 0.0)), 0.0),
                strict=strict,
                kg=(kT_cache[(d, p)][:, rs] * fac).astype(BF16),
                egl=jnp.exp(glast)))
        yield
        for ch in chains:
            kkqk = _dot_nt(ch["kq"], ch["Z"])
            ch["N"] = jnp.where(ch["strict"], -kkqk[:C] * ch["dec"], 0.0)
            ch["qk"] = (kkqk[C:] * ch["dec"]).astype(BF16)
        for ch in chains:
            ch["Pm"] = eye + ch["N"]
        for j in range(6):
            yield
            for ch in chains:
                nb = _bd_rows(ch["N"], left1)
                if j == 0:
                    ch["N"] = _dot(ch["N"].astype(BF16), nb)
                elif j < 5:
                    both = _dot(jnp.concatenate([ch["Pm"], ch["N"]], axis=0).astype(BF16), nb)
                    ch["Pm"] = ch["Pm"] + both[:C]
                    ch["N"] = both[C:]
                else:
                    ch["Pm"] = ch["Pm"] + _dot(ch["Pm"].astype(BF16), nb)
        yield
        for ch in chains:
            X = _dot(ch["Pm"].astype(BF16), _bd_rows(ch["rhs"], left2))
            by_key[ch["key"]] = dict(
                u=X[:, :LANES], qk=ch["qk"], kg=ch["kg"], egl=ch["egl"],
                wq=jnp.concatenate([X[:, LANES:].astype(BF16), ch["qg"]], axis=0))

    def scan_step(step):
        live = []
        for d, *_refs, o_ref in dirs:
            c = step if d == 0 else G - 1 - step
            for p in range(P):
                live.append((d * P + p, o_ref, slice(c * C, (c + 1) * C), p, by_key[(d, p, c)]))
        S = [st_ref[sidx] for sidx, *_ in live]
        t1 = [_dot(ch["wq"], S[i].astype(BF16)) for i, (*_, ch) in enumerate(live)]
        vn = [ch["u"] - t1[i][:C] for i, (*_, ch) in enumerate(live)]
        for i, (sidx, o_ref, rs, p, ch) in enumerate(live):
            o_ref[0, rs, p * LANES:(p + 1) * LANES] = t1[i][C:] + _dot(ch["qk"], _bd_rows(vn[i], left1))
        for i, (sidx, o_ref, rs, p, ch) in enumerate(live):
            upd = _dot(ch["kg"], vn[i].astype(BF16))
            decay = jnp.where(top, ch["egl"][:, 0:1], ch["egl"][:, C:C + 1])
            st_ref[sidx] = S[i] * decay + jnp.where(bdmask, upd, 0.0)

    bounds = sorted({0, G} | {G - (G * i) // DN_SCAN_GROUPS for i in range(1, DN_SCAN_GROUPS)})
    groups = [range(a, b) for a, b in zip(bounds[:-1], bounds[1:])]
    pending = iter(())
    for steps in groups:
        needed = [(0, p, c) for p in range(P) for c in steps]
        needed += [(1, p, G - 1 - c) for p in range(P) for c in steps]
        for _ in prepare(needed):
            step = next(pending, None)
            if step is not None:
                scan_step(step)
        for step in pending:
            scan_step(step)
        pending = iter(steps)
    for step in pending:
        scan_step(step)


def _dn_scan(q, k, v, bg, gt, *, G=8):
    B, S, _ = q.shape
    G = min(G, S // DN_CHUNK)
    R = G * DN_CHUNK
    nb = S // R
    n_ba = 4 * DN_HEADS
    P = DN_HEADS // 2
    fw = lambda n: pl.BlockSpec((1, R, n), lambda b, i: (b, i, 0))
    bw = lambda n: pl.BlockSpec((1, R, n), lambda b, i: (b, nb - 1 - i, 0))
    fwt = pl.BlockSpec((1, n_ba, R), lambda b, i: (b, 0, i))
    bwt = pl.BlockSpec((1, n_ba, R), lambda b, i: (b, 0, nb - 1 - i))
    return pl.pallas_call(
        functools.partial(_dnscan_body, G=G),
        grid=(B, nb),
        in_specs=[fw(DN_WIDTH), fw(DN_WIDTH), fw(DN_WIDTH), fw(BA_PAD), fwt,
                  bw(DN_WIDTH), bw(DN_WIDTH), bw(DN_WIDTH), bw(BA_PAD), bwt],
        out_specs=[fw(DN_WIDTH), bw(DN_WIDTH)],
        out_shape=[jax.ShapeDtypeStruct((B, S, DN_WIDTH), F32)] * 2,
        scratch_shapes=[pltpu.VMEM((2 * P, LANES, LANES), F32)],
        compiler_params=pltpu.CompilerParams(dimension_semantics=("parallel", "arbitrary")),
        name="dn_scan",
    )(q, k, v, bg, gt, q, k, v, bg, gt)


def _outproj_body(x_ref, ow_ref, od_ref, of_ref, ob_ref, z_ref, g_ref, w_ref, o_ref, cat_ref):
    oc = of_ref[0] + ob_ref[0]
    on = oc * lax.rsqrt(_seg_sum64(oc * oc) * (1.0 / HEAD_DIM) + EPS) * g_ref[...] * _silu(z_ref[0])
    cat_ref[:, :WIN_WIDTH] = ow_ref[0]
    cat_ref[:, WIN_WIDTH:WIN_WIDTH + DIFF_WIDTH] = od_ref[0]
    cat_ref[:, WIN_WIDTH + DIFF_WIDTH:] = on.astype(BF16)
    o_ref[0] = x_ref[0] + _dot(cat_ref[...], w_ref[...])


def _outproj(x, o_win, o_diff, o_f, o_b, z, dn_g, w_out, *, tm=512):
    B, S, _ = x.shape
    tm = min(tm, S)
    row = lambda n: pl.BlockSpec((1, tm, n), lambda b, i: (b, i, 0))
    g = jnp.tile(dn_g.astype(F32), DN_HEADS).reshape(1, DN_WIDTH)
    return pl.pallas_call(
        _outproj_body,
        grid=(B, S // tm),
        in_specs=[row(D_MODEL), row(WIN_WIDTH), row(DIFF_WIDTH), row(DN_WIDTH), row(DN_WIDTH),
                  row(DN_WIDTH), _resident((1, DN_WIDTH)), _resident((MIX_WIDTH, D_MODEL))],
        out_specs=row(D_MODEL),
        out_shape=jax.ShapeDtypeStruct((B, S, D_MODEL), F32),
        scratch_shapes=[pltpu.VMEM((tm, MIX_WIDTH), BF16)],
        compiler_params=pltpu.CompilerParams(
            dimension_semantics=("parallel", "parallel"), vmem_limit_bytes=VMEM_LIMIT),
        name="outproj",
    )(x, o_win, o_diff, o_f, o_b, z, g, w_out.astype(BF16))


def _mixer(x, ln_mix, w_mix_in, conv_w, sink, diff_lam, diff_g, a_log, dt_bias, dn_g, w_mix_out, lam_init):
    wq, wk, wv, dq, dkt, dv, dn_raw, z, ba, bat = _inproj(x, ln_mix, w_mix_in)
    win_slopes = 2.0 ** (-8.0 * jnp.arange(1, WIN_Q_HEADS + 1, dtype=F32) / WIN_Q_HEADS)
    o_win = _win_attention(wq, wk, wv, sink, win_slopes)
    o_diff = _diff_attention(dq, dkt, dv, diff_lam, diff_g, lam_init)
    q, k, v, bg, gt = _dn_prep(dn_raw, ba, bat, conv_w, a_log, dt_bias)
    o_f, o_b = _dn_scan(q, k, v, bg, gt)
    return _outproj(x, o_win, o_diff, o_f, o_b, z, dn_g, w_mix_out)


def kernel(x, ln_ffn1, ffn1_w_in, ffn1_w_out, ln_mix, w_mix_in, conv_w, sink_logits, diff_lambda,
           diff_norm_g, dn_A_log, dn_dt_bias, dn_norm_g, w_mix_out, ln_ffn2, ffn2_w_in, ffn2_w_out,
           ln_final):
    B, S, D = x.shape
    depth = ln_ffn1.shape[0]
    for l in range(depth):
        lam_init = 0.8 - 0.6 * math.exp(-0.3 * l)
        x = _ffn(x.reshape(B * S, D), ln_ffn1[l], ffn1_w_in[l], ffn1_w_out[l]).reshape(B, S, D)
        x = _mixer(x, ln_mix[l], w_mix_in[l], conv_w[l], sink_logits[l], diff_lambda[l], diff_norm_g[l],
                   dn_A_log[l], dn_dt_bias[l], dn_norm_g[l], w_mix_out[l], lam_init)
        g_final = ln_final if l == depth - 1 else None
        x = _ffn(x.reshape(B * S, D), ln_ffn2[l], ffn2_w_in[l], ffn2_w_out[l], g_final).reshape(B, S, D)
    return x
```

```python
import functools
import math

import jax
import jax.numpy as jnp
from jax import lax
from jax.experimental import pallas as pl
from jax.experimental.pallas import tpu as pltpu

F32 = jnp.float32
BF16 = jnp.bfloat16

D_MODEL = 1024
HEAD_DIM = 64
EPS = 1e-6
WIN_Q_HEADS = 6
WIN_KV_HEADS = 2
WIN_GROUP = WIN_Q_HEADS // WIN_KV_HEADS
WINDOW = 128
DIFF_HEADS = 4
DIFF_QK_DIM = HEAD_DIM // 2
DN_HEADS = 6
DN_CHUNK = 64
CONV_W = 5
D_FF = 2752
WIN_WIDTH = WIN_Q_HEADS * HEAD_DIM
WIN_KV_WIDTH = WIN_KV_HEADS * HEAD_DIM
DIFF_WIDTH = DIFF_HEADS * HEAD_DIM
DN_WIDTH = DN_HEADS * HEAD_DIM
DN_QKV = 3 * DN_WIDTH
MIX_WIDTH = WIN_WIDTH + DIFF_WIDTH + DN_WIDTH
OFF_WQ = 0
OFF_WK = OFF_WQ + WIN_WIDTH
OFF_WV = OFF_WK + WIN_KV_WIDTH
OFF_DQ = OFF_WV + WIN_KV_WIDTH
OFF_DK = OFF_DQ + DIFF_WIDTH
OFF_DV = OFF_DK + DIFF_WIDTH
OFF_DN = OFF_DV + DIFF_WIDTH
OFF_Z = OFF_DN + DN_QKV
OFF_BA = OFF_Z + DN_WIDTH
MIX_IN = OFF_BA + 4 * DN_HEADS

LANES = 128
FF_CHUNK = 256
D_FF_PAD = -(-D_FF // FF_CHUNK) * FF_CHUNK
VMEM_LIMIT = 56 * 1024 * 1024
BF16_ROWS = 16
LOG2E = math.log2(math.e)
DIFF_TK = 512
DIFF_KROWS = 48
DIFF_BIAS_ROWS = 3
DIFF_KT_ROWS = 2 * DIFF_HEADS * DIFF_KROWS
DN_SCAN_GROUPS = 3
DIFF_V_WIDTH = DIFF_HEADS * LANES
DIFF_SKIP_LOG2 = 150.0
DIFF_SLOPES = tuple(2.0 ** (-8.0 * (i + 1) / DIFF_HEADS) for i in range(DIFF_HEADS))


def _rms(x, g):
    return x * lax.rsqrt(jnp.mean(x * x, axis=-1, keepdims=True) + EPS) * g


def _silu(x):
    h = 0.5 * x
    return h + h * jnp.tanh(h)


def _seg_sum64(x):
    R, width = x.shape
    n = width // LANES
    rows = jnp.concatenate([x[:, c * LANES:(c + 1) * LANES] for c in range(n)], axis=0)
    r = lax.broadcasted_iota(jnp.int32, (LANES, LANES), 0) < HEAD_DIM
    c = lax.broadcasted_iota(jnp.int32, (LANES, LANES), 1) < HEAD_DIM
    ones_bd = (r == c).astype(F32).astype(BF16)
    hi = rows.astype(BF16)
    r1 = rows - hi.astype(F32)
    mid = r1.astype(BF16)
    lo = (r1 - mid.astype(F32)).astype(BF16)
    ss = _dot(hi, ones_bd) + _dot(mid, ones_bd) + _dot(lo, ones_bd)
    return jnp.concatenate([ss[c * R:(c + 1) * R] for c in range(n)], axis=1)


def _dot(a, b):
    return jnp.dot(a, b, preferred_element_type=F32)


def _dot_nt(a, b):
    return lax.dot_general(a, b, (((1,), (1,)), ((), ())), preferred_element_type=F32)


def _resident(shape):
    nd = len(shape)
    return pl.BlockSpec(shape, lambda *_: (0,) * nd, pipeline_mode=pl.Buffered(1))


def _ffn_body(*refs, n_chunks, final, mixed):
    refs = iter(refs)
    x_ref = next(refs)
    mix_refs = [next(refs) for _ in range(7)] if mixed else None
    g_ref, wg_ref, wu_ref, wo_ref = (next(refs) for _ in range(4))
    gf_ref = next(refs) if final else None
    o_ref, acc_ref = next(refs), next(refs)
    x = x_ref[...]
    if mixed:
        ow_ref, od_ref, of_ref, ob_ref, z_ref, dng_ref, wmix_ref = mix_refs
        cat_ref = next(refs)
        oc = of_ref[...] + ob_ref[...]
        on = oc * lax.rsqrt(_seg_sum64(oc * oc) * (1.0 / HEAD_DIM) + EPS) * dng_ref[...] * _silu(z_ref[...])
        cat_ref[:, :WIN_WIDTH] = ow_ref[...]
        cat_ref[:, WIN_WIDTH:WIN_WIDTH + DIFF_WIDTH] = od_ref[...]
        cat_ref[:, WIN_WIDTH + DIFF_WIDTH:] = on.astype(BF16)
        x = x + _dot(cat_ref[...], wmix_ref[...])
    h = _rms(x, g_ref[...]).astype(BF16)
    for c in range(n_chunks):
        cols = slice(c * FF_CHUNK, (c + 1) * FF_CHUNK)
        a = (_silu(_dot(h, wg_ref[:, cols])) * _dot(h, wu_ref[:, cols])).astype(BF16)
        part = _dot(a, wo_ref[cols, :])
        if c == 0:
            acc_ref[...] = part
        else:
            acc_ref[...] += part
    y = x + 0.5 * acc_ref[...]
    if final:
        y = _rms(y, gf_ref[...])
    o_ref[...] = y


def _prep_ffn_weights(w_in, w_out):
    pad = D_FF_PAD - D_FF
    wg = jnp.pad(w_in[:, :D_FF].astype(BF16), ((0, 0), (0, pad)))
    wu = jnp.pad(w_in[:, D_FF:].astype(BF16), ((0, 0), (0, pad)))
    wo = jnp.pad(w_out.astype(BF16), ((0, pad), (0, 0)))
    return wg, wu, wo


def _ffn(x2d, g, w_in, w_out, g_final=None, mix=None, *, tm=512):
    T = x2d.shape[0]
    tm = min(tm, T)
    n_chunks = D_FF_PAD // FF_CHUNK
    wg, wu, wo = _prep_ffn_weights(w_in, w_out)
    final = g_final is not None
    mixed = mix is not None
    row = lambda n: pl.BlockSpec((tm, n), lambda i: (i, 0))
    in_specs = [row(D_MODEL)]
    args = [x2d]
    scratch = [pltpu.VMEM((tm, D_MODEL), F32)]
    if mixed:
        o_win, o_diff, o_f, o_b, z, dn_g, w_mix_out = mix
        in_specs += [row(WIN_WIDTH), row(DIFF_WIDTH), row(DN_WIDTH), row(DN_WIDTH), row(DN_WIDTH),
                     _resident((1, DN_WIDTH)), _resident((MIX_WIDTH, D_MODEL))]
        args += [o_win, o_diff, o_f, o_b, z, jnp.tile(dn_g.astype(F32), DN_HEADS).reshape(1, DN_WIDTH),
                 w_mix_out.astype(BF16)]
        scratch.append(pltpu.VMEM((tm, MIX_WIDTH), BF16))
    in_specs += [_resident((1, D_MODEL)), _resident(wg.shape), _resident(wu.shape), _resident(wo.shape)]
    args += [g.reshape(1, D_MODEL), wg, wu, wo]
    if final:
        in_specs.append(_resident((1, D_MODEL)))
        args.append(g_final.reshape(1, D_MODEL))
    return pl.pallas_call(
        functools.partial(_ffn_body, n_chunks=n_chunks, final=final, mixed=mixed),
        grid=(T // tm,),
        in_specs=in_specs,
        out_specs=row(D_MODEL),
        out_shape=jax.ShapeDtypeStruct((T, D_MODEL), F32),
        scratch_shapes=scratch,
        compiler_params=pltpu.CompilerParams(
            dimension_semantics=("parallel",), vmem_limit_bytes=VMEM_LIMIT),
        name="ffn" + ("_mix" if mixed else "") + ("_final" if final else ""),
    )(*args)


BA_PAD = LANES


def _inproj_body(x_ref, g_ref, w_ref, wt_ref, wdv_ref,
                 wq_ref, wk_ref, wv_ref, dq_ref, dkt_ref, dv_ref, dn_ref, z_ref, ba_ref, bat_ref):
    h = _rms(x_ref[0], g_ref[...]).astype(BF16)
    wq_ref[0] = (_dot(h, w_ref[:, OFF_WQ:OFF_WK]) * (HEAD_DIM ** -0.5 * LOG2E)).astype(BF16)
    wk_ref[0] = _dot(h, w_ref[:, OFF_WK:OFF_WV]).astype(BF16)
    wv_ref[0] = _dot(h, w_ref[:, OFF_WV:OFF_DQ]).astype(BF16)
    dq_ref[0] = (_dot(h, w_ref[:, OFF_DQ:OFF_DK]) * (DIFF_QK_DIM ** -0.5 * LOG2E)).astype(BF16)
    wt_out = _dot_nt(wt_ref[...], h)
    kt = wt_out[:DIFF_WIDTH]
    tm = kt.shape[1]
    extra_rows = DIFF_KROWS - DIFF_QK_DIM
    pos = pl.program_id(1) * tm + lax.broadcasted_iota(jnp.int32, (extra_rows, tm), 1)
    kr = (pos % DIFF_TK).astype(F32)
    rowi = lax.broadcasted_iota(jnp.int32, (extra_rows, tm), 0)
    for hd in range(DIFF_HEADS):
        bias = (DIFF_SLOPES[hd] * LOG2E) * kr
        hi = bias.astype(BF16).astype(F32)
        mid = (bias - hi).astype(BF16).astype(F32)
        lo = bias - hi - mid
        extra = jnp.where(rowi == 0, hi, jnp.where(rowi == 1, mid, jnp.where(rowi == 2, lo, 0.0))).astype(BF16)
        for mp in range(2):
            hm = 2 * hd + mp
            dkt_ref[0, hm * DIFF_KROWS:hm * DIFF_KROWS + DIFF_QK_DIM, :] = (
                kt[hm * DIFF_QK_DIM:(hm + 1) * DIFF_QK_DIM, :].astype(BF16))
            dkt_ref[0, hm * DIFF_KROWS + DIFF_QK_DIM:(hm + 1) * DIFF_KROWS, :] = extra
    dv = _dot(h, wdv_ref[...])
    ones_half = (lax.broadcasted_iota(jnp.int32, dv.shape, 1) % LANES) >= HEAD_DIM
    dv_ref[0] = jnp.where(ones_half, 1.0, dv).astype(BF16)
    dn_ref[0] = _dot(h, w_ref[:, OFF_DN:OFF_Z])
    z_ref[0] = _dot(h, w_ref[:, OFF_Z:OFF_BA])
    ba_ref[0] = _dot(h, w_ref[:, OFF_BA:OFF_BA + BA_PAD])
    bat_ref[0] = wt_out[DIFF_WIDTH:DIFF_WIDTH + bat_ref.shape[1]]


def _inproj(x, g, w_in, *, tm=512):
    B, S, _ = x.shape
    tm = min(tm, S)
    n_ba = 4 * DN_HEADS
    w_in = w_in.astype(BF16)
    w = jnp.pad(w_in, ((0, 0), (0, OFF_BA + BA_PAD - MIX_IN)))
    wt = jnp.pad(jnp.concatenate([w_in[:, OFF_DK:OFF_DV], w_in[:, OFF_BA:MIX_IN]], axis=1),
                 ((0, 0), (0, 2 * BF16_ROWS - n_ba % (2 * BF16_ROWS)))).T
    wdv = jnp.pad(w_in[:, OFF_DV:OFF_DN].reshape(D_MODEL, DIFF_HEADS, HEAD_DIM),
                  ((0, 0), (0, 0), (0, LANES - HEAD_DIM))).reshape(D_MODEL, DIFF_V_WIDTH)
    row = lambda n: pl.BlockSpec((1, tm, n), lambda b, i: (b, i, 0))
    col = lambda n: pl.BlockSpec((1, n, tm), lambda b, i: (b, 0, i))
    out_shapes = [
        jax.ShapeDtypeStruct((B, S, WIN_WIDTH), BF16),
        jax.ShapeDtypeStruct((B, S, WIN_KV_WIDTH), BF16),
        jax.ShapeDtypeStruct((B, S, WIN_KV_WIDTH), BF16),
        jax.ShapeDtypeStruct((B, S, DIFF_WIDTH), BF16),
        jax.ShapeDtypeStruct((B, DIFF_KT_ROWS, S), BF16),
        jax.ShapeDtypeStruct((B, S, DIFF_V_WIDTH), BF16),
        jax.ShapeDtypeStruct((B, S, DN_QKV), F32),
        jax.ShapeDtypeStruct((B, S, DN_WIDTH), F32),
        jax.ShapeDtypeStruct((B, S, BA_PAD), F32),
        jax.ShapeDtypeStruct((B, n_ba, S), F32),
    ]
    out_specs = [row(WIN_WIDTH), row(WIN_KV_WIDTH), row(WIN_KV_WIDTH), row(DIFF_WIDTH),
                 col(DIFF_KT_ROWS), row(DIFF_V_WIDTH), row(DN_QKV), row(DN_WIDTH), row(BA_PAD), col(n_ba)]
    return pl.pallas_call(
        _inproj_body,
        grid=(B, S // tm),
        in_specs=[row(D_MODEL), _resident((1, D_MODEL)), _resident(w.shape),
                  _resident(wt.shape), _resident(wdv.shape)],
        out_specs=out_specs,
        out_shape=out_shapes,
        compiler_params=pltpu.CompilerParams(
            dimension_semantics=("parallel", "parallel"), vmem_limit_bytes=VMEM_LIMIT),
        name="inproj",
    )(x, g.reshape(1, D_MODEL), w, wt, wdv)


def _win_body(q_ref, kp_ref, kc_ref, kn_ref, vp_ref, vc_ref, vn_ref, sink_ref, slope_ref, o_ref,
              *, n_steps, nq):
    i = pl.program_id(1)
    W = WINDOW
    KW = 3 * W
    kw = jnp.concatenate([kp_ref[0], kc_ref[0], kn_ref[0]], axis=0).astype(F32)
    vw = jnp.concatenate([vp_ref[0], vc_ref[0], vn_ref[0]], axis=0).astype(F32)
    kw_sw = pltpu.roll(kw, HEAD_DIM, 1)
    vw_sw = pltpu.roll(vw, HEAD_DIM, 1)
    left = lax.broadcasted_iota(jnp.int32, (KW, LANES), 1) < HEAD_DIM
    qi = lax.broadcasted_iota(jnp.int32, (W, KW), 0)
    ki = lax.broadcasted_iota(jnp.int32, (W, KW), 1)
    dist = jnp.abs(ki - W - qi)
    band_bias = [jnp.where(dist <= W, (-LOG2E) * slope_ref[:, hq:hq + 1] * dist.astype(F32), -1e30)
                 for hq in range(WIN_Q_HEADS)]
    edge_lo = jnp.where(jnp.logical_or(ki >= W, i > 0), 0.0, -1e30)
    edge_hi = jnp.where(jnp.logical_or(ki < 2 * W, i < n_steps - 1), 0.0, -1e30)
    n_pairs = WIN_Q_HEADS // 2
    ones_stack = jnp.concatenate([jnp.where(left, 1.0, 0.0), jnp.where(left, 0.0, 1.0)], axis=0).astype(BF16)
    blocks = []
    for a in range(nq):
        rows = slice(a * W, a * W + KW)

        def stack(first, second, rows=rows):
            return jnp.concatenate([jnp.where(left, first[rows], 0.0),
                                    jnp.where(left, 0.0, second[rows])], axis=0).astype(BF16)

        zk = (stack(kw, kw_sw), stack(kw, kw), stack(kw_sw, kw))
        zv = tuple(jnp.concatenate([z, ones_stack], axis=1)
                   for z in (stack(vw, vw_sw), stack(vw, vw), stack(vw_sw, vw)))
        edge = None
        if a == 0:
            edge = edge_lo
        if a == nq - 1:
            edge = edge_hi if edge is None else edge + edge_hi
        q = q_ref[0, a * W:(a + 1) * W, :]
        s2 = [_dot_nt(q[:, p * LANES:(p + 1) * LANES], zk[p]) for p in range(n_pairs)]
        blocks.append((a, edge, zv, s2))
    weights, sink_terms = {}, {}
    for a, edge, zv, s2 in blocks:
        for hq in range(WIN_Q_HEADS):
            p, t = divmod(hq, 2)
            s = s2[p][:, t * KW:(t + 1) * KW] + band_bias[hq]
            if edge is not None:
                s = s + edge
            sink = LOG2E * sink_ref[:, hq:hq + 1]
            m = jnp.maximum(jnp.max(s, axis=-1, keepdims=True), sink)
            weights[(a, hq)] = jnp.exp2(s - m).astype(BF16)
            sink_terms[(a, hq)] = jnp.exp2(sink - m)
    left_w = lax.broadcasted_iota(jnp.int32, (W, LANES), 1) < HEAD_DIM
    for a, edge, zv, s2 in blocks:
        for p in range(n_pairs):
            ee = jnp.concatenate([weights[(a, 2 * p)], weights[(a, 2 * p + 1)]], axis=1)
            acc = _dot(ee, zv[p])
            denom = acc[:, LANES:] + jnp.where(left_w, sink_terms[(a, 2 * p)], sink_terms[(a, 2 * p + 1)])
            o_ref[0, a * W:(a + 1) * W, p * LANES:(p + 1) * LANES] = (acc[:, :LANES] / denom).astype(BF16)


def _win_attention(q, k, v, sink, slopes, *, nq=8):
    B, S, _ = q.shape
    W = WINDOW
    nq = min(nq, S // W)
    ns = S // (nq * W)
    nb = S // W
    qspec = pl.BlockSpec((1, nq * W, WIN_WIDTH), lambda b, i: (b, i, 0))
    prev = pl.BlockSpec((1, W, WIN_KV_WIDTH), lambda b, i: (b, jnp.maximum(i * nq - 1, 0), 0))
    cur = pl.BlockSpec((1, nq * W, WIN_KV_WIDTH), lambda b, i: (b, i, 0))
    nxt = pl.BlockSpec((1, W, WIN_KV_WIDTH), lambda b, i: (b, jnp.minimum((i + 1) * nq, nb - 1), 0))
    return pl.pallas_call(
        functools.partial(_win_body, n_steps=ns, nq=nq),
        grid=(B, ns),
        in_specs=[qspec, prev, cur, nxt, prev, cur, nxt,
                  _resident((1, WIN_Q_HEADS)), _resident((1, WIN_Q_HEADS))],
        out_specs=pl.BlockSpec((1, nq * W, WIN_WIDTH), lambda b, i: (b, i, 0)),
        out_shape=jax.ShapeDtypeStruct((B, S, WIN_WIDTH), BF16),
        compiler_params=pltpu.CompilerParams(dimension_semantics=("parallel", "parallel")),
        name="win_attn",
    )(q, k, k, k, v, v, v, sink.reshape(1, WIN_Q_HEADS).astype(F32), slopes.reshape(1, WIN_Q_HEADS))


def _diff_ranges(q, kt, tq, tk):
    B, S, _ = q.shape
    nq, nk, n_maps = S // tq, S // tk, 2 * DIFF_HEADS
    qf = q.astype(F32).reshape(B, nq, tq, n_maps, DIFF_QK_DIM)
    qn = jnp.sqrt(jnp.max(jnp.sum(qf * qf, axis=-1), axis=2))
    kf = kt.astype(F32).reshape(B, n_maps, DIFF_KROWS, S)[:, :, :DIFF_QK_DIM]
    kn_pos = jnp.sqrt(jnp.sum(kf * kf, axis=2))
    kn = jnp.max(kn_pos.reshape(B, n_maps, nk, tk), axis=-1)
    kown = jnp.max(kn_pos.reshape(B, n_maps, nq, tq), axis=-1)
    reach = jnp.swapaxes(qn, 1, 2)[..., None] * (kn[:, :, None, :] + kown[..., None]) * (1.0 + 1e-3)
    reach = jnp.max(reach.reshape(B, DIFF_HEADS, 2, nq, nk), axis=2)
    q0 = jnp.arange(nq, dtype=jnp.int32) * tq
    k0 = jnp.arange(nk, dtype=jnp.int32) * tk
    dmin = jnp.maximum(jnp.maximum(k0[None, :] - (q0[:, None] + tq - 1), q0[:, None] - (k0[None, :] + tk - 1)), 0)
    c = jnp.asarray(DIFF_SLOPES, F32) * LOG2E
    need = reach - c[None, :, None, None] * dmin.astype(F32)[None, None] > -DIFF_SKIP_LOG2
    idx = jnp.arange(nk, dtype=jnp.int32)
    need = jnp.logical_or(need, (idx[None, :] == (q0 // tk)[:, None])[None, None])
    lo = jnp.min(jnp.where(need, idx, nk), axis=-1)
    hi = jnp.max(jnp.where(need, idx, -1), axis=-1)
    return jnp.stack([lo, hi], axis=-1).transpose(0, 2, 1, 3).reshape(-1).astype(jnp.int32)


def _diff_body(rng_ref, q_ref, kt_ref, v_ref, lam_ref, g_ref, o_ref, qv_ref, m_ref, acc_ref,
               s0_ref, st0_ref, s1_ref, st1_ref, *, tq, tk, lam_init):
    n_maps = 2 * DIFF_HEADS
    n_groups = tk // LANES
    q0 = pl.program_id(1) * tq
    jd = q0 // tk
    q = q_ref[0]
    lane_x = lax.broadcasted_iota(jnp.int32, (tq, DIFF_KROWS - DIFF_QK_DIM), 1)
    for var, sign in enumerate((1.0, -1.0, 0.0)):
        extras = jnp.where(lane_x < DIFF_BIAS_ROWS, -sign, 0.0).astype(BF16)
        for hm in range(n_maps):
            qv_ref[var, hm, :, 0:DIFF_QK_DIM] = q[:, hm * DIFF_QK_DIM:(hm + 1) * DIFF_QK_DIM]
            qv_ref[var, hm, :, DIFF_QK_DIM:DIFF_KROWS] = extras
    m_ref[...] = jnp.full(m_ref.shape, -jnp.inf, F32)
    acc_ref[...] = jnp.zeros(acc_ref.shape, F32)
    qpos = (q0 + lax.broadcasted_iota(jnp.int32, (tq, LANES), 0)).astype(F32)
    bufs = ((s0_ref, st0_ref), (s1_ref, st1_ref))
    rng_base = (pl.program_id(0) * pl.num_programs(1) + pl.program_id(1)) * (2 * DIFF_HEADS)

    los = [rng_ref[rng_base + 2 * h] for h in range(DIFF_HEADS)]
    n_offs = [rng_ref[rng_base + 2 * h + 1] - los[h] for h in range(DIFF_HEADS)]
    starts = [0]
    for h in range(DIFF_HEADS):
        starts.append(starts[-1] + n_offs[h])
    n_total = starts[DIFF_HEADS]

    def pick(h, vals):
        out = vals[-1]
        for i in range(len(vals) - 2, -1, -1):
            out = jnp.where(h == i, vals[i], out)
        return out

    def locate(u):
        h = sum((u >= starts[i]).astype(jnp.int32) for i in range(1, DIFF_HEADS))
        t = pick(h, los) + u - pick(h, starts[:DIFF_HEADS])
        return h, t + (t >= jd).astype(jnp.int32)

    def scores(h, j, buf, diag_dist=None):
        s_ref, st_ref = bufs[buf]
        if diag_dist is not None:
            var = 2
            bias = (DIFF_SLOPES[h] * LOG2E) * diag_dist
        else:
            c = pick(h, [sl * LOG2E for sl in DIFF_SLOPES])
            after = j > jd
            var = jnp.where(after, 0, 1)
            rowoff = jnp.where(after, c, -c) * ((j * tk).astype(F32) - qpos)
        k0 = pl.multiple_of(j * tk, tk)
        for mp in range(2):
            hm = 2 * h + mp
            r0 = hm * DIFF_KROWS if diag_dist is not None else pl.multiple_of(hm * DIFF_KROWS, BF16_ROWS)
            kt = kt_ref[0, pl.ds(r0, DIFF_KROWS), pl.ds(k0, tk)]
            s = _dot(qv_ref[var, hm], kt)
            if diag_dist is not None:
                s = s - bias
            s_ref[mp] = s
            mx = s[:, 0:LANES]
            for g in range(1, n_groups):
                mx = jnp.maximum(mx, s[:, g * LANES:(g + 1) * LANES])
            mrow = jnp.broadcast_to(jnp.max(mx, axis=-1, keepdims=True), (tq, LANES))
            m_old = m_ref[hm]
            if diag_dist is not None:
                m_new = jnp.maximum(m_old, mrow)
                shift = m_new
            else:
                m_new = jnp.maximum(m_old, mrow - rowoff)
                shift = m_new + rowoff
            st_ref[mp, 0] = shift
            st_ref[mp, 1] = jnp.exp2(m_old - m_new)
            m_ref[hm] = m_new

    def values(h, j, buf, static=False):
        s_ref, st_ref = bufs[buf]
        k0 = pl.multiple_of(j * tk, tk)
        c0 = h * LANES if static else pl.multiple_of(h * LANES, LANES)
        v2 = v_ref[0, pl.ds(k0, tk), pl.ds(c0, LANES)]
        for mp in range(2):
            hm = 2 * h + mp
            shift = st_ref[mp, 0]
            alpha = st_ref[mp, 1]
            e = jnp.exp2(s_ref[mp] - jnp.concatenate([shift] * n_groups, axis=1))
            acc_ref[hm] = alpha * acc_ref[hm] + _dot(e.astype(BF16), v2)

    kpos = (jd * tk + lax.broadcasted_iota(jnp.int32, (tq, tk), 1)).astype(F32)
    qp = (q0 + lax.broadcasted_iota(jnp.int32, (tq, tk), 0)).astype(F32)
    diag_dist = jnp.abs(qp - kpos)
    scores(0, jd, 0, diag_dist)
    for h in range(DIFF_HEADS):
        if h + 1 < DIFF_HEADS:
            scores(h + 1, jd, (h + 1) % 2, diag_dist)
        values(h, jd, h % 2, static=True)

    def s_at(u, buf):
        h, j = locate(u)
        scores(h, j, buf)

    def v_at(u, buf):
        h, j = locate(u)
        values(h, j, buf)

    @pl.when(n_total > 0)
    def _():
        s_at(0, 0)

    def pair(p, carry):
        u = 2 * p
        s_at(u + 1, 1)
        v_at(u, 0)
        s_at(u + 2, 0)
        v_at(u + 1, 1)
        return carry

    n_pairs = jnp.maximum(n_total - 1, 0) // 2
    lax.fori_loop(0, n_pairs, pair, 0)
    u_last = 2 * n_pairs
    left_over = n_total - u_last

    @pl.when(left_over == 2)
    def _():
        s_at(u_last + 1, 1)
        v_at(u_last, 0)
        v_at(u_last + 1, 1)

    @pl.when(left_over == 1)
    def _():
        v_at(u_last, 0)

    lp = lam_ref[...]
    lam = (jnp.exp(jnp.sum(lp[0:1] * lp[1:2], axis=-1, keepdims=True))
           - jnp.exp(jnp.sum(lp[2:3] * lp[3:4], axis=-1, keepdims=True)) + lam_init)
    lane = lax.broadcasted_iota(jnp.int32, (tq, LANES), 1)
    left = lane < HEAD_DIM
    ys = []
    for h in range(DIFF_HEADS):
        a1, a2 = acc_ref[2 * h], acc_ref[2 * h + 1]
        od = a1 / pltpu.roll(a1, HEAD_DIM, 1) - lam * (a2 / pltpu.roll(a2, HEAD_DIM, 1))
        ms = jnp.sum(jnp.where(left, od * od, 0.0), axis=-1, keepdims=True) * (1.0 / HEAD_DIM)
        ys.append(od * lax.rsqrt(ms + EPS) * g_ref[...] * (1.0 - lam_init))
    for p in range(DIFF_HEADS // 2):
        o_ref[0, :, p * LANES:(p + 1) * LANES] = jnp.where(
            left, ys[2 * p], pltpu.roll(ys[2 * p + 1], HEAD_DIM, 1)).astype(BF16)


def _diff_attention(q, kt, v, diff_lambda, diff_g, lam_init, *, tq=512):
    B, S, _ = q.shape
    tk = DIFF_TK
    tq = min(tq, S)
    assert S % tk == 0 and tk % tq == 0
    n_maps = 2 * DIFF_HEADS
    g2 = jnp.concatenate([diff_g, diff_g]).reshape(1, LANES).astype(F32)
    grid_spec = pltpu.PrefetchScalarGridSpec(
        num_scalar_prefetch=1,
        grid=(B, S // tq),
        in_specs=[pl.BlockSpec((1, tq, DIFF_WIDTH), lambda b, i, r: (b, i, 0)),
                  pl.BlockSpec((1, DIFF_KT_ROWS, S), lambda b, i, r: (b, 0, 0)),
                  pl.BlockSpec((1, S, DIFF_V_WIDTH), lambda b, i, r: (b, 0, 0)),
                  pl.BlockSpec((4, DIFF_QK_DIM), lambda b, i, r: (0, 0)),
                  pl.BlockSpec((1, LANES), lambda b, i, r: (0, 0))],
        out_specs=pl.BlockSpec((1, tq, DIFF_WIDTH), lambda b, i, r: (b, i, 0)),
        scratch_shapes=[pltpu.VMEM((3, n_maps, tq, DIFF_KROWS), BF16),
                        pltpu.VMEM((n_maps, tq, LANES), F32),
                        pltpu.VMEM((n_maps, tq, LANES), F32),
                        pltpu.VMEM((2, tq, tk), F32), pltpu.VMEM((2, 2, tq, LANES), F32),
                        pltpu.VMEM((2, tq, tk), F32), pltpu.VMEM((2, 2, tq, LANES), F32)])
    return pl.pallas_call(
        functools.partial(_diff_body, tq=tq, tk=tk, lam_init=lam_init),
        grid_spec=grid_spec,
        out_shape=jax.ShapeDtypeStruct((B, S, DIFF_WIDTH), BF16),
        compiler_params=pltpu.CompilerParams(
            dimension_semantics=("parallel", "arbitrary"), vmem_limit_bytes=VMEM_LIMIT),
        name="diff_attn",
    )(_diff_ranges(q, kt, tq, tk), q, kt, v, diff_lambda.astype(F32), g2)


def _l2n64(x):
    return x * lax.rsqrt(_seg_sum64(x * x) + EPS)


def _dnprep_body(x_ref, xp_ref, xn_ref, w_ref, ba_ref, bat_ref, ab_ref, abt_ref,
                 q_ref, k_ref, v_ref, bg_ref, gt_ref, *, ts, n_tiles):
    i = pl.program_id(1)
    H = DN_HEADS
    x = x_ref[0]
    xp = jnp.where(i > 0, xp_ref[0], 0.0)
    xn = jnp.where(i < n_tiles - 1, xn_ref[0], 0.0)
    xx = jnp.concatenate([xp, x, xn], axis=0)
    half = CONV_W // 2
    y = jnp.zeros_like(x)
    for j in range(CONV_W):
        y = y + w_ref[j:j + 1, :] * xx[8 - half + j:8 - half + j + ts, :]
    y = _silu(y)
    q_ref[0] = _l2n64(y[:, :DN_WIDTH]) * (HEAD_DIM ** -0.5)
    k_ref[0] = _l2n64(y[:, DN_WIDTH:2 * DN_WIDTH])
    v_ref[0] = y[:, 2 * DN_WIDTH:]

    def gates(raw, a_log, dt_bias):
        beta = 1.0 / (1.0 + jnp.exp(-raw))
        z = raw + dt_bias
        softplus = jnp.maximum(z, 0.0) + jnp.log(1.0 + jnp.exp(-jnp.abs(z)))
        return beta, -jnp.exp(a_log) * softplus

    ba = ba_ref[0]
    beta, g = gates(ba, ab_ref[0:1, :], ab_ref[1:2, :])
    lane = lax.broadcasted_iota(jnp.int32, ba.shape, 1)
    r = lax.broadcasted_iota(jnp.int32, ba.shape, 0) % DN_CHUNK
    bwd = jnp.logical_and(lane >= 3 * H, lane < 4 * H)
    gc = g
    for d in (1, 2, 4, 8, 16, 32):
        f = jnp.where(r >= d, pltpu.roll(gc, d, 0), 0.0)
        b = jnp.where(r < DN_CHUNK - d, pltpu.roll(gc, ts - d, 0), 0.0)
        gc = gc + jnp.where(bwd, b, f)
    bg_ref[0] = jnp.where(lane < 2 * H, beta, gc)
    bat = bat_ref[0]
    _, gt = gates(bat, abt_ref[:, 0:1], abt_ref[:, 1:2])
    row = lax.broadcasted_iota(jnp.int32, bat.shape, 0)
    c = lax.broadcasted_iota(jnp.int32, bat.shape, 1) % DN_CHUNK
    bwd_t = row >= 3 * H
    for d in (1, 2, 4, 8, 16, 32):
        f = jnp.where(c >= d, pltpu.roll(gt, d, 1), 0.0)
        b = jnp.where(c < DN_CHUNK - d, pltpu.roll(gt, ts - d, 1), 0.0)
        gt = gt + jnp.where(bwd_t, b, f)
    gt_ref[0] = gt


def _dn_prep(dn_raw, ba, bat, conv_w, a_log, dt_bias, *, ts=256):
    B, S, C = dn_raw.shape
    ts = min(ts, S)
    nt = S // ts
    n_ba = 4 * DN_HEADS
    hb = ts // 8
    zeros = jnp.zeros((2 * DN_HEADS,), F32)
    al = jnp.concatenate([zeros, a_log.reshape(-1).astype(F32)])
    db = jnp.concatenate([zeros, dt_bias.reshape(-1).astype(F32)])
    ab = jnp.pad(jnp.stack([al, db]), ((0, 0), (0, BA_PAD - n_ba)))
    abt = jnp.stack([al, db], axis=1)
    row = lambda n: pl.BlockSpec((1, ts, n), lambda b, i: (b, i, 0))
    return pl.pallas_call(
        functools.partial(_dnprep_body, ts=ts, n_tiles=nt),
        grid=(B, nt),
        in_specs=[row(C),
                  pl.BlockSpec((1, 8, C), lambda b, i: (b, jnp.maximum(i * hb - 1, 0), 0)),
                  pl.BlockSpec((1, 8, C), lambda b, i: (b, jnp.minimum((i + 1) * hb, S // 8 - 1), 0)),
                  _resident((CONV_W, C)), row(BA_PAD),
                  pl.BlockSpec((1, n_ba, ts), lambda b, i: (b, 0, i)),
                  _resident((2, BA_PAD)), _resident((n_ba, 2))],
        out_specs=[row(DN_WIDTH), row(DN_WIDTH), row(DN_WIDTH), row(BA_PAD),
                   pl.BlockSpec((1, n_ba, ts), lambda b, i: (b, 0, i))],
        out_shape=[jax.ShapeDtypeStruct((B, S, DN_WIDTH), F32)] * 3
        + [jax.ShapeDtypeStruct((B, S, BA_PAD), F32), jax.ShapeDtypeStruct((B, n_ba, S), F32)],
        compiler_params=pltpu.CompilerParams(dimension_semantics=("parallel", "parallel")),
        name="dn_prep",
    )(dn_raw, dn_raw, dn_raw, conv_w.astype(F32), ba, bat, ab, abt)


def _bd_rows(y, left):
    return jnp.concatenate([jnp.where(left, y, 0.0), jnp.where(left, 0.0, y)], axis=0).astype(BF16)


def _dnscan_body(qf_ref, kf_ref, vf_ref, bgf_ref, gtf_ref, qb_ref, kb_ref, vb_ref, bgb_ref, gtb_ref,
                 of_ref, ob_ref, st_ref, *, G):
    C = DN_CHUNK
    H = DN_HEADS
    P = H // 2

    @pl.when(pl.program_id(1) == 0)
    def _():
        st_ref[...] = jnp.zeros(st_ref.shape, F32)

    lane1 = lax.broadcasted_iota(jnp.int32, (C, LANES), 1)
    row1 = lax.broadcasted_iota(jnp.int32, (C, LANES), 0)
    left1 = lane1 < C
    s_idx = lane1 % C
    eye = (row1 == s_idx).astype(F32)
    lane2 = lax.broadcasted_iota(jnp.int32, (C, 2 * LANES), 1)
    left2 = (lane2 % LANES) < C
    rr = lax.broadcasted_iota(jnp.int32, (LANES, LANES), 0)
    cc = lax.broadcasted_iota(jnp.int32, (LANES, LANES), 1)
    top = rr < C
    bdmask = (rr < C) == (cc < C)
    top_c = lax.broadcasted_iota(jnp.int32, (LANES, C), 0) < C

    dirs = ((0, qf_ref, kf_ref, vf_ref, bgf_ref, gtf_ref, of_ref),
            (1, qb_ref, kb_ref, vb_ref, bgb_ref, gtb_ref, ob_ref))
    kT_cache = {}
    by_key = {}

    def prepare(group):
        chains = []
        for d, p, c in group:
            _, q_ref, k_ref, v_ref, bg_ref, gt_ref, _ = dirs[d]
            tri = (row1 >= s_idx) if d == 0 else (row1 <= s_idx)
            strict = (row1 > s_idx) if d == 0 else (row1 < s_idx)
            last = C - 1 if d == 0 else 0
            sl = slice(p * LANES, (p + 1) * LANES)
            cb = d * H + 2 * p
            cg = 2 * H + d * H + 2 * p
            if (d, p) not in kT_cache:
                kT_cache[(d, p)] = k_ref[0, :, sl].T
            rs = slice(c * C, (c + 1) * C)
            qc, kc, vc = q_ref[0, rs, sl], k_ref[0, rs, sl], v_ref[0, rs, sl]
            bexp = jnp.where(left1, bg_ref[0, rs, cb:cb + 1], bg_ref[0, rs, cb + 1:cb + 2])
            gcc = jnp.where(left1, bg_ref[0, rs, cg:cg + 1], bg_ref[0, rs, cg + 1:cg + 2])
            g0 = gt_ref[0, cg:cg + 1, rs]
            g1 = gt_ref[0, cg + 1:cg + 2, rs]
            grow = jnp.concatenate([g0, g1], axis=1)
            glast = gcc[last:last + 1, :]
            gl0 = glast[:, 0:1]
            gl1 = glast[:, C:C + 1]
            eg = jnp.exp(gcc)
            kb_ = kc * bexp
            fac = jnp.exp(jnp.minimum(jnp.where(top_c, gl0 - g0, gl1 - g1), 0.0))
            chains.append(dict(
                key=(d, p, c),
                rhs=jnp.concatenate([vc * bexp, kb_ * eg], axis=1),
                qg=(qc * eg).astype(BF16),
                kq=jnp.concatenate([kb_, qc], axis=0).astype(BF16),
                Z=_bd_rows(kc, left1),
                dec=jnp.where(tri, jnp.exp(jnp.minimum(gcc - grow, 0.0)), 0.0),
                strict=strict,
                kg=(kT_cache[(d, p)][:, rs] * fac).astype(BF16),
                egl=jnp.exp(glast)))
        yield
        for ch in chains:
            kkqk = _dot_nt(ch["kq"], ch["Z"])
            ch["N"] = jnp.where(ch["strict"], -kkqk[:C] * ch["dec"], 0.0)
            ch["qk"] = (kkqk[C:] * ch["dec"]).astype(BF16)
        for ch in chains:
            ch["Pm"] = eye + ch["N"]
        for j in range(6):
            yield
            for ch in chains:
                nb = _bd_rows(ch["N"], left1)
                if j == 0:
                    ch["N"] = _dot(ch["N"].astype(BF16), nb)
                elif j < 5:
                    both = _dot(jnp.concatenate([ch["Pm"], ch["N"]], axis=0).astype(BF16), nb)
                    ch["Pm"] = ch["Pm"] + both[:C]
                    ch["N"] = both[C:]
                else:
                    ch["Pm"] = ch["Pm"] + _dot(ch["Pm"].astype(BF16), nb)
        yield
        for ch in chains:
            X = _dot(ch["Pm"].astype(BF16), _bd_rows(ch["rhs"], left2))
            by_key[ch["key"]] = dict(
                u=X[:, :LANES], qk=ch["qk"], kg=ch["kg"], egl=ch["egl"],
                wq=jnp.concatenate([X[:, LANES:].astype(BF16), ch["qg"]], axis=0))

    def scan_step(step):
        live = []
        for d, *_refs, o_ref in dirs:
            c = step if d == 0 else G - 1 - step
            for p in range(P):
                live.append((d * P + p, o_ref, slice(c * C, (c + 1) * C), p, by_key[(d, p, c)]))
        S = [st_ref[sidx] for sidx, *_ in live]
        t1 = [_dot(ch["wq"], S[i].astype(BF16)) for i, (*_, ch) in enumerate(live)]
        vn = [ch["u"] - t1[i][:C] for i, (*_, ch) in enumerate(live)]
        for i, (sidx, o_ref, rs, p, ch) in enumerate(live):
            o_ref[0, rs, p * LANES:(p + 1) * LANES] = t1[i][C:] + _dot(ch["qk"], _bd_rows(vn[i], left1))
        for i, (sidx, o_ref, rs, p, ch) in enumerate(live):
            upd = _dot(ch["kg"], vn[i].astype(BF16))
            decay = jnp.where(top, ch["egl"][:, 0:1], ch["egl"][:, C:C + 1])
            st_ref[sidx] = S[i] * decay + jnp.where(bdmask, upd, 0.0)

    bounds = sorted({0, G} | {G - (G * i) // DN_SCAN_GROUPS for i in range(1, DN_SCAN_GROUPS)})
    groups = [range(a, b) for a, b in zip(bounds[:-1], bounds[1:])]
    pending = iter(())
    for steps in groups:
        needed = [(0, p, c) for p in range(P) for c in steps]
        needed += [(1, p, G - 1 - c) for p in range(P) for c in steps]
        for _ in prepare(needed):
            step = next(pending, None)
            if step is not None:
                scan_step(step)
        for step in pending:
            scan_step(step)
        pending = iter(steps)
    for step in pending:
        scan_step(step)


def _dn_scan(q, k, v, bg, gt, *, G=8):
    B, S, _ = q.shape
    G = min(G, S // DN_CHUNK)
    R = G * DN_CHUNK
    nb = S // R
    n_ba = 4 * DN_HEADS
    P = DN_HEADS // 2
    fw = lambda n: pl.BlockSpec((1, R, n), lambda b, i: (b, i, 0))
    bw = lambda n: pl.BlockSpec((1, R, n), lambda b, i: (b, nb - 1 - i, 0))
    fwt = pl.BlockSpec((1, n_ba, R), lambda b, i: (b, 0, i))
    bwt = pl.BlockSpec((1, n_ba, R), lambda b, i: (b, 0, nb - 1 - i))
    return pl.pallas_call(
        functools.partial(_dnscan_body, G=G),
        grid=(B, nb),
        in_specs=[fw(DN_WIDTH), fw(DN_WIDTH), fw(DN_WIDTH), fw(BA_PAD), fwt,
                  bw(DN_WIDTH), bw(DN_WIDTH), bw(DN_WIDTH), bw(BA_PAD), bwt],
        out_specs=[fw(DN_WIDTH), bw(DN_WIDTH)],
        out_shape=[jax.ShapeDtypeStruct((B, S, DN_WIDTH), F32)] * 2,
        scratch_shapes=[pltpu.VMEM((2 * P, LANES, LANES), F32)],
        compiler_params=pltpu.CompilerParams(dimension_semantics=("parallel", "arbitrary")),
        name="dn_scan",
    )(q, k, v, bg, gt, q, k, v, bg, gt)


def _mixers(x, ln_mix, w_mix_in, conv_w, sink, diff_lam, diff_g, a_log, dt_bias, lam_init):
    B, S, _ = x.shape
    wq, wk, wv, dq, dkt, dv, dn_raw, z, ba, bat = _inproj(x, ln_mix, w_mix_in)
    win_slopes = 2.0 ** (-8.0 * jnp.arange(1, WIN_Q_HEADS + 1, dtype=F32) / WIN_Q_HEADS)
    o_win = _win_attention(wq, wk, wv, sink, win_slopes)
    o_diff = _diff_attention(dq, dkt, dv, diff_lam, diff_g, lam_init)
    q, k, v, bg, gt = _dn_prep(dn_raw, ba, bat, conv_w, a_log, dt_bias)
    o_f, o_b = _dn_scan(q, k, v, bg, gt)
    return tuple(t.reshape(B * S, t.shape[-1]) for t in (o_win, o_diff, o_f, o_b, z))


def kernel(x, ln_ffn1, ffn1_w_in, ffn1_w_out, ln_mix, w_mix_in, conv_w, sink_logits, diff_lambda,
           diff_norm_g, dn_A_log, dn_dt_bias, dn_norm_g, w_mix_out, ln_ffn2, ffn2_w_in, ffn2_w_out,
           ln_final):
    B, S, D = x.shape
    depth = ln_ffn1.shape[0]
    for l in range(depth):
        lam_init = 0.8 - 0.6 * math.exp(-0.3 * l)
        x = _ffn(x.reshape(B * S, D), ln_ffn1[l], ffn1_w_in[l], ffn1_w_out[l]).reshape(B, S, D)
        mixed = _mixers(x, ln_mix[l], w_mix_in[l], conv_w[l], sink_logits[l], diff_lambda[l], diff_norm_g[l],
                        dn_A_log[l], dn_dt_bias[l], lam_init)
        g_final = ln_final if l == depth - 1 else None
        x = _ffn(x.reshape(B * S, D), ln_ffn2[l], ffn2_w_in[l], ffn2_w_out[l], g_final,
                 mix=mixed + (dn_norm_g[l], w_mix_out[l])).reshape(B, S, D)
    return x
```

```python
import functools
import math

import jax
import jax.numpy as jnp
from jax import lax
from jax.experimental import pallas as pl
from jax.experimental.pallas import tpu as pltpu

F32 = jnp.float32
BF16 = jnp.bfloat16

D_MODEL = 1024
HEAD_DIM = 64
EPS = 1e-6
WIN_Q_HEADS = 6
WIN_KV_HEADS = 2
WIN_GROUP = WIN_Q_HEADS // WIN_KV_HEADS
WINDOW = 128
DIFF_HEADS = 4
DIFF_QK_DIM = HEAD_DIM // 2
DN_HEADS = 6
DN_CHUNK = 64
CONV_W = 5
D_FF = 2752
WIN_WIDTH = WIN_Q_HEADS * HEAD_DIM
WIN_KV_WIDTH = WIN_KV_HEADS * HEAD_DIM
DIFF_WIDTH = DIFF_HEADS * HEAD_DIM
DN_WIDTH = DN_HEADS * HEAD_DIM
DN_QKV = 3 * DN_WIDTH
MIX_WIDTH = WIN_WIDTH + DIFF_WIDTH + DN_WIDTH
OFF_WQ = 0
OFF_WK = OFF_WQ + WIN_WIDTH
OFF_WV = OFF_WK + WIN_KV_WIDTH
OFF_DQ = OFF_WV + WIN_KV_WIDTH
OFF_DK = OFF_DQ + DIFF_WIDTH
OFF_DV = OFF_DK + DIFF_WIDTH
OFF_DN = OFF_DV + DIFF_WIDTH
OFF_Z = OFF_DN + DN_QKV
OFF_BA = OFF_Z + DN_WIDTH
MIX_IN = OFF_BA + 4 * DN_HEADS

LANES = 128
FF_CHUNK = 256
D_FF_PAD = -(-D_FF // FF_CHUNK) * FF_CHUNK
VMEM_LIMIT = 56 * 1024 * 1024
BF16_ROWS = 16
LOG2E = math.log2(math.e)
DIFF_TK = 512
DIFF_KROWS = 48
DIFF_BIAS_ROWS = 3
DIFF_KT_ROWS = 2 * DIFF_HEADS * DIFF_KROWS
DN_SCAN_GROUPS = 5
DIFF_V_WIDTH = DIFF_HEADS * LANES
DIFF_SKIP_LOG2 = 150.0
DIFF_SLOPES = tuple(2.0 ** (-8.0 * (i + 1) / DIFF_HEADS) for i in range(DIFF_HEADS))


def _rms(x, g):
    return x * lax.rsqrt(jnp.mean(x * x, axis=-1, keepdims=True) + EPS) * g


def _silu(x):
    h = 0.5 * x
    return h + h * jnp.tanh(h)


def _seg_sum64(x):
    R, width = x.shape
    n = width // LANES
    rows = jnp.concatenate([x[:, c * LANES:(c + 1) * LANES] for c in range(n)], axis=0)
    r = lax.broadcasted_iota(jnp.int32, (LANES, LANES), 0) < HEAD_DIM
    c = lax.broadcasted_iota(jnp.int32, (LANES, LANES), 1) < HEAD_DIM
    ones_bd = (r == c).astype(F32).astype(BF16)
    hi = rows.astype(BF16)
    r1 = rows - hi.astype(F32)
    mid = r1.astype(BF16)
    lo = (r1 - mid.astype(F32)).astype(BF16)
    ss = _dot(hi, ones_bd) + _dot(mid, ones_bd) + _dot(lo, ones_bd)
    return jnp.concatenate([ss[c * R:(c + 1) * R] for c in range(n)], axis=1)


def _dot(a, b):
    return jnp.dot(a, b, preferred_element_type=F32)


def _dot_nt(a, b):
    return lax.dot_general(a, b, (((1,), (1,)), ((), ())), preferred_element_type=F32)


def _resident(shape):
    nd = len(shape)
    return pl.BlockSpec(shape, lambda *_: (0,) * nd, pipeline_mode=pl.Buffered(1))


def _ffn_body(*refs, n_chunks, final, mixed):
    refs = iter(refs)
    x_ref = next(refs)
    mix_refs = [next(refs) for _ in range(7)] if mixed else None
    g_ref, wg_ref, wu_ref, wo_ref = (next(refs) for _ in range(4))
    gf_ref = next(refs) if final else None
    o_ref, acc_ref = next(refs), next(refs)
    x = x_ref[...]
    if mixed:
        ow_ref, od_ref, of_ref, ob_ref, z_ref, dng_ref, wmix_ref = mix_refs
        cat_ref = next(refs)
        oc = of_ref[...] + ob_ref[...]
        on = oc * lax.rsqrt(_seg_sum64(oc * oc) * (1.0 / HEAD_DIM) + EPS) * dng_ref[...] * _silu(z_ref[...])
        cat_ref[:, :WIN_WIDTH] = ow_ref[...]
        cat_ref[:, WIN_WIDTH:WIN_WIDTH + DIFF_WIDTH] = od_ref[...]
        cat_ref[:, WIN_WIDTH + DIFF_WIDTH:] = on.astype(BF16)
        x = x + _dot(cat_ref[...], wmix_ref[...])
    h = _rms(x, g_ref[...]).astype(BF16)
    for c in range(n_chunks):
        cols = slice(c * FF_CHUNK, (c + 1) * FF_CHUNK)
        a = (_silu(_dot(h, wg_ref[:, cols])) * _dot(h, wu_ref[:, cols])).astype(BF16)
        part = _dot(a, wo_ref[cols, :])
        if c == 0:
            acc_ref[...] = part
        else:
            acc_ref[...] += part
    y = x + 0.5 * acc_ref[...]
    if final:
        y = _rms(y, gf_ref[...])
    o_ref[...] = y


def _prep_ffn_weights(w_in, w_out):
    pad = D_FF_PAD - D_FF
    wg = jnp.pad(w_in[:, :D_FF].astype(BF16), ((0, 0), (0, pad)))
    wu = jnp.pad(w_in[:, D_FF:].astype(BF16), ((0, 0), (0, pad)))
    wo = jnp.pad(w_out.astype(BF16), ((0, pad), (0, 0)))
    return wg, wu, wo


def _ffn(x2d, g, w_in, w_out, g_final=None, mix=None, *, tm=512):
    T = x2d.shape[0]
    tm = min(tm, T)
    n_chunks = D_FF_PAD // FF_CHUNK
    wg, wu, wo = _prep_ffn_weights(w_in, w_out)
    final = g_final is not None
    mixed = mix is not None
    row = lambda n: pl.BlockSpec((tm, n), lambda i: (i, 0))
    in_specs = [row(D_MODEL)]
    args = [x2d]
    scratch = [pltpu.VMEM((tm, D_MODEL), F32)]
    if mixed:
        o_win, o_diff, o_f, o_b, z, dn_g, w_mix_out = mix
        in_specs += [row(WIN_WIDTH), row(DIFF_WIDTH), row(DN_WIDTH), row(DN_WIDTH), row(DN_WIDTH),
                     _resident((1, DN_WIDTH)), _resident((MIX_WIDTH, D_MODEL))]
        args += [o_win, o_diff, o_f, o_b, z, jnp.tile(dn_g.astype(F32), DN_HEADS).reshape(1, DN_WIDTH),
                 w_mix_out.astype(BF16)]
        scratch.append(pltpu.VMEM((tm, MIX_WIDTH), BF16))
    in_specs += [_resident((1, D_MODEL)), _resident(wg.shape), _resident(wu.shape), _resident(wo.shape)]
    args += [g.reshape(1, D_MODEL), wg, wu, wo]
    if final:
        in_specs.append(_resident((1, D_MODEL)))
        args.append(g_final.reshape(1, D_MODEL))
    return pl.pallas_call(
        functools.partial(_ffn_body, n_chunks=n_chunks, final=final, mixed=mixed),
        grid=(T // tm,),
        in_specs=in_specs,
        out_specs=row(D_MODEL),
        out_shape=jax.ShapeDtypeStruct((T, D_MODEL), F32),
        scratch_shapes=scratch,
        compiler_params=pltpu.CompilerParams(
            dimension_semantics=("parallel",), vmem_limit_bytes=VMEM_LIMIT),
        name="ffn" + ("_mix" if mixed else "") + ("_final" if final else ""),
    )(*args)


BA_PAD = LANES


def _inproj_body(x_ref, g_ref, w_ref, wt_ref, wdv_ref,
                 wq_ref, wk_ref, wv_ref, dq_ref, dkt_ref, dv_ref, dn_ref, z_ref, ba_ref, bat_ref):
    h = _rms(x_ref[0], g_ref[...]).astype(BF16)
    wq_ref[0] = (_dot(h, w_ref[:, OFF_WQ:OFF_WK]) * (HEAD_DIM ** -0.5 * LOG2E)).astype(BF16)
    wk_ref[0] = _dot(h, w_ref[:, OFF_WK:OFF_WV]).astype(BF16)
    wv_ref[0] = _dot(h, w_ref[:, OFF_WV:OFF_DQ]).astype(BF16)
    dq_ref[0] = (_dot(h, w_ref[:, OFF_DQ:OFF_DK]) * (DIFF_QK_DIM ** -0.5 * LOG2E)).astype(BF16)
    wt_out = _dot_nt(wt_ref[...], h)
    kt = wt_out[:DIFF_WIDTH]
    tm = kt.shape[1]
    extra_rows = DIFF_KROWS - DIFF_QK_DIM
    pos = pl.program_id(1) * tm + lax.broadcasted_iota(jnp.int32, (extra_rows, tm), 1)
    kr = (pos % DIFF_TK).astype(F32)
    rowi = lax.broadcasted_iota(jnp.int32, (extra_rows, tm), 0)
    for hd in range(DIFF_HEADS):
        bias = (DIFF_SLOPES[hd] * LOG2E) * kr
        hi = bias.astype(BF16).astype(F32)
        mid = (bias - hi).astype(BF16).astype(F32)
        lo = bias - hi - mid
        extra = jnp.where(rowi == 0, hi, jnp.where(rowi == 1, mid, jnp.where(rowi == 2, lo, 0.0))).astype(BF16)
        for mp in range(2):
            hm = 2 * hd + mp
            dkt_ref[0, hm * DIFF_KROWS:hm * DIFF_KROWS + DIFF_QK_DIM, :] = (
                kt[hm * DIFF_QK_DIM:(hm + 1) * DIFF_QK_DIM, :].astype(BF16))
            dkt_ref[0, hm * DIFF_KROWS + DIFF_QK_DIM:(hm + 1) * DIFF_KROWS, :] = extra
    dv = _dot(h, wdv_ref[...])
    ones_half = (lax.broadcasted_iota(jnp.int32, dv.shape, 1) % LANES) >= HEAD_DIM
    dv_ref[0] = jnp.where(ones_half, 1.0, dv).astype(BF16)
    dn_ref[0] = _dot(h, w_ref[:, OFF_DN:OFF_Z])
    z_ref[0] = _dot(h, w_ref[:, OFF_Z:OFF_BA])
    ba_ref[0] = _dot(h, w_ref[:, OFF_BA:OFF_BA + BA_PAD])
    bat_ref[0] = wt_out[DIFF_WIDTH:DIFF_WIDTH + bat_ref.shape[1]]


def _inproj(x, g, w_in, *, tm=512):
    B, S, _ = x.shape
    tm = min(tm, S)
    n_ba = 4 * DN_HEADS
    w_in = w_in.astype(BF16)
    w = jnp.pad(w_in, ((0, 0), (0, OFF_BA + BA_PAD - MIX_IN)))
    wt = jnp.pad(jnp.concatenate([w_in[:, OFF_DK:OFF_DV], w_in[:, OFF_BA:MIX_IN]], axis=1),
                 ((0, 0), (0, 2 * BF16_ROWS - n_ba % (2 * BF16_ROWS)))).T
    wdv = jnp.pad(w_in[:, OFF_DV:OFF_DN].reshape(D_MODEL, DIFF_HEADS, HEAD_DIM),
                  ((0, 0), (0, 0), (0, LANES - HEAD_DIM))).reshape(D_MODEL, DIFF_V_WIDTH)
    row = lambda n: pl.BlockSpec((1, tm, n), lambda b, i: (b, i, 0))
    col = lambda n: pl.BlockSpec((1, n, tm), lambda b, i: (b, 0, i))
    out_shapes = [
        jax.ShapeDtypeStruct((B, S, WIN_WIDTH), BF16),
        jax.ShapeDtypeStruct((B, S, WIN_KV_WIDTH), BF16),
        jax.ShapeDtypeStruct((B, S, WIN_KV_WIDTH), BF16),
        jax.ShapeDtypeStruct((B, S, DIFF_WIDTH), BF16),
        jax.ShapeDtypeStruct((B, DIFF_KT_ROWS, S), BF16),
        jax.ShapeDtypeStruct((B, S, DIFF_V_WIDTH), BF16),
        jax.ShapeDtypeStruct((B, S, DN_QKV), F32),
        jax.ShapeDtypeStruct((B, S, DN_WIDTH), F32),
        jax.ShapeDtypeStruct((B, S, BA_PAD), F32),
        jax.ShapeDtypeStruct((B, n_ba, S), F32),
    ]
    out_specs = [row(WIN_WIDTH), row(WIN_KV_WIDTH), row(WIN_KV_WIDTH), row(DIFF_WIDTH),
                 col(DIFF_KT_ROWS), row(DIFF_V_WIDTH), row(DN_QKV), row(DN_WIDTH), row(BA_PAD), col(n_ba)]
    return pl.pallas_call(
        _inproj_body,
        grid=(B, S // tm),
        in_specs=[row(D_MODEL), _resident((1, D_MODEL)), _resident(w.shape),
                  _resident(wt.shape), _resident(wdv.shape)],
        out_specs=out_specs,
        out_shape=out_shapes,
        compiler_params=pltpu.CompilerParams(
            dimension_semantics=("parallel", "parallel"), vmem_limit_bytes=VMEM_LIMIT),
        name="inproj",
    )(x, g.reshape(1, D_MODEL), w, wt, wdv)


def _win_body(q_ref, kp_ref, kc_ref, kn_ref, vp_ref, vc_ref, vn_ref, sink_ref, slope_ref, o_ref,
              *, n_steps, nq):
    i = pl.program_id(1)
    W = WINDOW
    KW = 3 * W
    kw = jnp.concatenate([kp_ref[0], kc_ref[0], kn_ref[0]], axis=0).astype(F32)
    vw = jnp.concatenate([vp_ref[0], vc_ref[0], vn_ref[0]], axis=0).astype(F32)
    kw_sw = pltpu.roll(kw, HEAD_DIM, 1)
    vw_sw = pltpu.roll(vw, HEAD_DIM, 1)
    left = lax.broadcasted_iota(jnp.int32, (KW, LANES), 1) < HEAD_DIM
    qi = lax.broadcasted_iota(jnp.int32, (W, KW), 0)
    ki = lax.broadcasted_iota(jnp.int32, (W, KW), 1)
    dist = jnp.abs(ki - W - qi)
    band_bias = [jnp.where(dist <= W, (-LOG2E) * slope_ref[:, hq:hq + 1] * dist.astype(F32), -1e30)
                 for hq in range(WIN_Q_HEADS)]
    edge_lo = jnp.where(jnp.logical_or(ki >= W, i > 0), 0.0, -1e30)
    edge_hi = jnp.where(jnp.logical_or(ki < 2 * W, i < n_steps - 1), 0.0, -1e30)
    n_pairs = WIN_Q_HEADS // 2
    ones_stack = jnp.concatenate([jnp.where(left, 1.0, 0.0), jnp.where(left, 0.0, 1.0)], axis=0).astype(BF16)
    blocks = []
    for a in range(nq):
        rows = slice(a * W, a * W + KW)

        def stack(first, second, rows=rows):
            return jnp.concatenate([jnp.where(left, first[rows], 0.0),
                                    jnp.where(left, 0.0, second[rows])], axis=0).astype(BF16)

        zk = (stack(kw, kw_sw), stack(kw, kw), stack(kw_sw, kw))
        zv = tuple(jnp.concatenate([z, ones_stack], axis=1)
                   for z in (stack(vw, vw_sw), stack(vw, vw), stack(vw_sw, vw)))
        edge = None
        if a == 0:
            edge = edge_lo
        if a == nq - 1:
            edge = edge_hi if edge is None else edge + edge_hi
        q = q_ref[0, a * W:(a + 1) * W, :]
        s2 = [_dot_nt(q[:, p * LANES:(p + 1) * LANES], zk[p]) for p in range(n_pairs)]
        blocks.append((a, edge, zv, s2))
    weights, sink_terms = {}, {}
    for a, edge, zv, s2 in blocks:
        for hq in range(WIN_Q_HEADS):
            p, t = divmod(hq, 2)
            s = s2[p][:, t * KW:(t + 1) * KW] + band_bias[hq]
            if edge is not None:
                s = s + edge
            sink = LOG2E * sink_ref[:, hq:hq + 1]
            m = jnp.maximum(jnp.max(s, axis=-1, keepdims=True), sink)
            weights[(a, hq)] = jnp.exp2(s - m).astype(BF16)
            sink_terms[(a, hq)] = jnp.exp2(sink - m)
    left_w = lax.broadcasted_iota(jnp.int32, (W, LANES), 1) < HEAD_DIM
    for a, edge, zv, s2 in blocks:
        for p in range(n_pairs):
            ee = jnp.concatenate([weights[(a, 2 * p)], weights[(a, 2 * p + 1)]], axis=1)
            acc = _dot(ee, zv[p])
            denom = acc[:, LANES:] + jnp.where(left_w, sink_terms[(a, 2 * p)], sink_terms[(a, 2 * p + 1)])
            o_ref[0, a * W:(a + 1) * W, p * LANES:(p + 1) * LANES] = (acc[:, :LANES] / denom).astype(BF16)


def _win_attention(q, k, v, sink, slopes, *, nq=8):
    B, S, _ = q.shape
    W = WINDOW
    nq = min(nq, S // W)
    ns = S // (nq * W)
    nb = S // W
    qspec = pl.BlockSpec((1, nq * W, WIN_WIDTH), lambda b, i: (b, i, 0))
    prev = pl.BlockSpec((1, W, WIN_KV_WIDTH), lambda b, i: (b, jnp.maximum(i * nq - 1, 0), 0))
    cur = pl.BlockSpec((1, nq * W, WIN_KV_WIDTH), lambda b, i: (b, i, 0))
    nxt = pl.BlockSpec((1, W, WIN_KV_WIDTH), lambda b, i: (b, jnp.minimum((i + 1) * nq, nb - 1), 0))
    return pl.pallas_call(
        functools.partial(_win_body, n_steps=ns, nq=nq),
        grid=(B, ns),
        in_specs=[qspec, prev, cur, nxt, prev, cur, nxt,
                  _resident((1, WIN_Q_HEADS)), _resident((1, WIN_Q_HEADS))],
        out_specs=pl.BlockSpec((1, nq * W, WIN_WIDTH), lambda b, i: (b, i, 0)),
        out_shape=jax.ShapeDtypeStruct((B, S, WIN_WIDTH), BF16),
        compiler_params=pltpu.CompilerParams(dimension_semantics=("parallel", "parallel")),
        name="win_attn",
    )(q, k, k, k, v, v, v, sink.reshape(1, WIN_Q_HEADS).astype(F32), slopes.reshape(1, WIN_Q_HEADS))


def _diff_ranges(q, kt, tq, tk):
    B, S, _ = q.shape
    nq, nk, n_maps = S // tq, S // tk, 2 * DIFF_HEADS
    qf = q.astype(F32).reshape(B, nq, tq, n_maps, DIFF_QK_DIM)
    qn = jnp.sqrt(jnp.max(jnp.sum(qf * qf, axis=-1), axis=2))
    kf = kt.astype(F32).reshape(B, n_maps, DIFF_KROWS, S)[:, :, :DIFF_QK_DIM]
    kn_pos = jnp.sqrt(jnp.sum(kf * kf, axis=2))
    kn = jnp.max(kn_pos.reshape(B, n_maps, nk, tk), axis=-1)
    kown = jnp.max(kn_pos.reshape(B, n_maps, nq, tq), axis=-1)
    reach = jnp.swapaxes(qn, 1, 2)[..., None] * (kn[:, :, None, :] + kown[..., None]) * (1.0 + 1e-3)
    reach = jnp.max(reach.reshape(B, DIFF_HEADS, 2, nq, nk), axis=2)
    q0 = jnp.arange(nq, dtype=jnp.int32) * tq
    k0 = jnp.arange(nk, dtype=jnp.int32) * tk
    dmin = jnp.maximum(jnp.maximum(k0[None, :] - (q0[:, None] + tq - 1), q0[:, None] - (k0[None, :] + tk - 1)), 0)
    c = jnp.asarray(DIFF_SLOPES, F32) * LOG2E
    need = reach - c[None, :, None, None] * dmin.astype(F32)[None, None] > -DIFF_SKIP_LOG2
    idx = jnp.arange(nk, dtype=jnp.int32)
    need = jnp.logical_or(need, (idx[None, :] == (q0 // tk)[:, None])[None, None])
    lo = jnp.min(jnp.where(need, idx, nk), axis=-1)
    hi = jnp.max(jnp.where(need, idx, -1), axis=-1)
    return jnp.stack([lo, hi], axis=-1).transpose(0, 2, 1, 3).reshape(-1).astype(jnp.int32)


def _diff_body(rng_ref, q_ref, kt_ref, v_ref, lam_ref, g_ref, o_ref, qv_ref, m_ref, acc_ref,
               s0_ref, st0_ref, s1_ref, st1_ref, *, tq, tk, lam_init):
    n_maps = 2 * DIFF_HEADS
    n_groups = tk // LANES
    q0 = pl.program_id(1) * tq
    jd = q0 // tk
    q = q_ref[0]
    lane_x = lax.broadcasted_iota(jnp.int32, (tq, DIFF_KROWS - DIFF_QK_DIM), 1)
    for var, sign in enumerate((1.0, -1.0, 0.0)):
        extras = jnp.where(lane_x < DIFF_BIAS_ROWS, -sign, 0.0).astype(BF16)
        for hm in range(n_maps):
            qv_ref[var, hm, :, 0:DIFF_QK_DIM] = q[:, hm * DIFF_QK_DIM:(hm + 1) * DIFF_QK_DIM]
            qv_ref[var, hm, :, DIFF_QK_DIM:DIFF_KROWS] = extras
    m_ref[...] = jnp.full(m_ref.shape, -jnp.inf, F32)
    acc_ref[...] = jnp.zeros(acc_ref.shape, F32)
    qpos = (q0 + lax.broadcasted_iota(jnp.int32, (tq, LANES), 0)).astype(F32)
    bufs = ((s0_ref, st0_ref), (s1_ref, st1_ref))
    rng_base = (pl.program_id(0) * pl.num_programs(1) + pl.program_id(1)) * (2 * DIFF_HEADS)

    los = [rng_ref[rng_base + 2 * h] for h in range(DIFF_HEADS)]
    n_offs = [rng_ref[rng_base + 2 * h + 1] - los[h] for h in range(DIFF_HEADS)]
    starts = [0]
    for h in range(DIFF_HEADS):
        starts.append(starts[-1] + n_offs[h])
    n_total = starts[DIFF_HEADS]

    def pick(h, vals):
        out = vals[-1]
        for i in range(len(vals) - 2, -1, -1):
            out = jnp.where(h == i, vals[i], out)
        return out

    def locate(u):
        h = sum((u >= starts[i]).astype(jnp.int32) for i in range(1, DIFF_HEADS))
        t = pick(h, los) + u - pick(h, starts[:DIFF_HEADS])
        return h, t + (t >= jd).astype(jnp.int32)

    def scores(h, j, buf, diag_dist=None):
        s_ref, st_ref = bufs[buf]
        if diag_dist is not None:
            var = 2
            bias = (DIFF_SLOPES[h] * LOG2E) * diag_dist
        else:
            c = pick(h, [sl * LOG2E for sl in DIFF_SLOPES])
            after = j > jd
            var = jnp.where(after, 0, 1)
            rowoff = jnp.where(after, c, -c) * ((j * tk).astype(F32) - qpos)
        k0 = pl.multiple_of(j * tk, tk)
        for mp in range(2):
            hm = 2 * h + mp
            r0 = hm * DIFF_KROWS if diag_dist is not None else pl.multiple_of(hm * DIFF_KROWS, BF16_ROWS)
            kt = kt_ref[0, pl.ds(r0, DIFF_KROWS), pl.ds(k0, tk)]
            s = _dot(qv_ref[var, hm], kt)
            if diag_dist is not None:
                s = s - bias
            s_ref[mp] = s
            mx = s[:, 0:LANES]
            for g in range(1, n_groups):
                mx = jnp.maximum(mx, s[:, g * LANES:(g + 1) * LANES])
            mrow = jnp.broadcast_to(jnp.max(mx, axis=-1, keepdims=True), (tq, LANES))
            m_old = m_ref[hm]
            if diag_dist is not None:
                m_new = jnp.maximum(m_old, mrow)
                shift = m_new
            else:
                m_new = jnp.maximum(m_old, mrow - rowoff)
                shift = m_new + rowoff
            st_ref[mp, 0] = shift
            st_ref[mp, 1] = jnp.exp2(m_old - m_new)
            m_ref[hm] = m_new

    def values(h, j, buf, static=False):
        s_ref, st_ref = bufs[buf]
        k0 = pl.multiple_of(j * tk, tk)
        c0 = h * LANES if static else pl.multiple_of(h * LANES, LANES)
        v2 = v_ref[0, pl.ds(k0, tk), pl.ds(c0, LANES)]
        for mp in range(2):
            hm = 2 * h + mp
            shift = st_ref[mp, 0]
            alpha = st_ref[mp, 1]
            e = jnp.exp2(s_ref[mp] - jnp.concatenate([shift] * n_groups, axis=1))
            acc_ref[hm] = alpha * acc_ref[hm] + _dot(e.astype(BF16), v2)

    kpos = (jd * tk + lax.broadcasted_iota(jnp.int32, (tq, tk), 1)).astype(F32)
    qp = (q0 + lax.broadcasted_iota(jnp.int32, (tq, tk), 0)).astype(F32)
    diag_dist = jnp.abs(qp - kpos)
    scores(0, jd, 0, diag_dist)
    for h in range(DIFF_HEADS):
        if h + 1 < DIFF_HEADS:
            scores(h + 1, jd, (h + 1) % 2, diag_dist)
        values(h, jd, h % 2, static=True)

    def s_at(u, buf):
        h, j = locate(u)
        scores(h, j, buf)

    def v_at(u, buf):
        h, j = locate(u)
        values(h, j, buf)

    @pl.when(n_total > 0)
    def _():
        s_at(0, 0)

    def pair(p, carry):
        u = 2 * p
        s_at(u + 1, 1)
        v_at(u, 0)
        s_at(u + 2, 0)
        v_at(u + 1, 1)
        return carry

    n_pairs = jnp.maximum(n_total - 1, 0) // 2
    lax.fori_loop(0, n_pairs, pair, 0)
    u_last = 2 * n_pairs
    left_over = n_total - u_last

    @pl.when(left_over == 2)
    def _():
        s_at(u_last + 1, 1)
        v_at(u_last, 0)
        v_at(u_last + 1, 1)

    @pl.when(left_over == 1)
    def _():
        v_at(u_last, 0)

    lp = lam_ref[...]
    lam = (jnp.exp(jnp.sum(lp[0:1] * lp[1:2], axis=-1, keepdims=True))
           - jnp.exp(jnp.sum(lp[2:3] * lp[3:4], axis=-1, keepdims=True)) + lam_init)
    lane = lax.broadcasted_iota(jnp.int32, (tq, LANES), 1)
    left = lane < HEAD_DIM
    ys = []
    for h in range(DIFF_HEADS):
        a1, a2 = acc_ref[2 * h], acc_ref[2 * h + 1]
        od = a1 / pltpu.roll(a1, HEAD_DIM, 1) - lam * (a2 / pltpu.roll(a2, HEAD_DIM, 1))
        ms = jnp.sum(jnp.where(left, od * od, 0.0), axis=-1, keepdims=True) * (1.0 / HEAD_DIM)
        ys.append(od * lax.rsqrt(ms + EPS) * g_ref[...] * (1.0 - lam_init))
    for p in range(DIFF_HEADS // 2):
        o_ref[0, :, p * LANES:(p + 1) * LANES] = jnp.where(
            left, ys[2 * p], pltpu.roll(ys[2 * p + 1], HEAD_DIM, 1)).astype(BF16)


def _diff_attention(q, kt, v, diff_lambda, diff_g, lam_init, *, tq=512):
    B, S, _ = q.shape
    tk = DIFF_TK
    tq = min(tq, S)
    assert S % tk == 0 and tk % tq == 0
    n_maps = 2 * DIFF_HEADS
    g2 = jnp.concatenate([diff_g, diff_g]).reshape(1, LANES).astype(F32)
    grid_spec = pltpu.PrefetchScalarGridSpec(
        num_scalar_prefetch=1,
        grid=(B, S // tq),
        in_specs=[pl.BlockSpec((1, tq, DIFF_WIDTH), lambda b, i, r: (b, i, 0)),
                  pl.BlockSpec((1, DIFF_KT_ROWS, S), lambda b, i, r: (b, 0, 0)),
                  pl.BlockSpec((1, S, DIFF_V_WIDTH), lambda b, i, r: (b, 0, 0)),
                  pl.BlockSpec((4, DIFF_QK_DIM), lambda b, i, r: (0, 0)),
                  pl.BlockSpec((1, LANES), lambda b, i, r: (0, 0))],
        out_specs=pl.BlockSpec((1, tq, DIFF_WIDTH), lambda b, i, r: (b, i, 0)),
        scratch_shapes=[pltpu.VMEM((3, n_maps, tq, DIFF_KROWS), BF16),
                        pltpu.VMEM((n_maps, tq, LANES), F32),
                        pltpu.VMEM((n_maps, tq, LANES), F32),
                        pltpu.VMEM((2, tq, tk), F32), pltpu.VMEM((2, 2, tq, LANES), F32),
                        pltpu.VMEM((2, tq, tk), F32), pltpu.VMEM((2, 2, tq, LANES), F32)])
    return pl.pallas_call(
        functools.partial(_diff_body, tq=tq, tk=tk, lam_init=lam_init),
        grid_spec=grid_spec,
        out_shape=jax.ShapeDtypeStruct((B, S, DIFF_WIDTH), BF16),
        compiler_params=pltpu.CompilerParams(
            dimension_semantics=("parallel", "arbitrary"), vmem_limit_bytes=VMEM_LIMIT),
        name="diff_attn",
    )(_diff_ranges(q, kt, tq, tk), q, kt, v, diff_lambda.astype(F32), g2)


def _l2n64(x):
    return x * lax.rsqrt(_seg_sum64(x * x) + EPS)


def _dnprep_body(x_ref, xp_ref, xn_ref, w_ref, ba_ref, bat_ref, ab_ref, abt_ref,
                 q_ref, k_ref, v_ref, bg_ref, gt_ref, xx_ref, *, ts, n_tiles):
    i = pl.program_id(1)
    H = DN_HEADS
    xx_ref[0:8, :] = jnp.where(i > 0, xp_ref[0], 0.0)
    xx_ref[8:8 + ts, :] = x_ref[0]
    xx_ref[8 + ts:16 + ts, :] = jnp.where(i < n_tiles - 1, xn_ref[0], 0.0)
    half = CONV_W // 2
    y = w_ref[half:half + 1, :] * x_ref[0]
    for j in range(CONV_W):
        if j != half:
            y = y + w_ref[j:j + 1, :] * xx_ref[8 - half + j:8 - half + j + ts, :]
    y = _silu(y)
    q_ref[0] = _l2n64(y[:, :DN_WIDTH]) * (HEAD_DIM ** -0.5)
    k_ref[0] = _l2n64(y[:, DN_WIDTH:2 * DN_WIDTH])
    v_ref[0] = y[:, 2 * DN_WIDTH:]

    def gates(raw, a_log, dt_bias):
        beta = 1.0 / (1.0 + jnp.exp(-raw))
        z = raw + dt_bias
        softplus = jnp.maximum(z, 0.0) + jnp.log(1.0 + jnp.exp(-jnp.abs(z)))
        return beta, -jnp.exp(a_log) * softplus

    ba = ba_ref[0]
    beta, g = gates(ba, ab_ref[0:1, :], ab_ref[1:2, :])
    lane = lax.broadcasted_iota(jnp.int32, ba.shape, 1)
    r = lax.broadcasted_iota(jnp.int32, ba.shape, 0) % DN_CHUNK
    bwd = jnp.logical_and(lane >= 3 * H, lane < 4 * H)
    gc = g
    for d in (1, 2, 4, 8, 16, 32):
        f = jnp.where(r >= d, pltpu.roll(gc, d, 0), 0.0)
        b = jnp.where(r < DN_CHUNK - d, pltpu.roll(gc, ts - d, 0), 0.0)
        gc = gc + jnp.where(bwd, b, f)
    bg_ref[0] = jnp.where(lane < 2 * H, beta, gc)
    bat = bat_ref[0]
    _, gt = gates(bat, abt_ref[:, 0:1], abt_ref[:, 1:2])
    row = lax.broadcasted_iota(jnp.int32, bat.shape, 0)
    c = lax.broadcasted_iota(jnp.int32, bat.shape, 1) % DN_CHUNK
    bwd_t = row >= 3 * H
    for d in (1, 2, 4, 8, 16, 32):
        f = jnp.where(c >= d, pltpu.roll(gt, d, 1), 0.0)
        b = jnp.where(c < DN_CHUNK - d, pltpu.roll(gt, ts - d, 1), 0.0)
        gt = gt + jnp.where(bwd_t, b, f)
    gt_ref[0] = gt


def _dn_prep(dn_raw, ba, bat, conv_w, a_log, dt_bias, *, ts=256):
    B, S, C = dn_raw.shape
    ts = min(ts, S)
    nt = S // ts
    n_ba = 4 * DN_HEADS
    hb = ts // 8
    zeros = jnp.zeros((2 * DN_HEADS,), F32)
    al = jnp.concatenate([zeros, a_log.reshape(-1).astype(F32)])
    db = jnp.concatenate([zeros, dt_bias.reshape(-1).astype(F32)])
    ab = jnp.pad(jnp.stack([al, db]), ((0, 0), (0, BA_PAD - n_ba)))
    abt = jnp.stack([al, db], axis=1)
    row = lambda n: pl.BlockSpec((1, ts, n), lambda b, i: (b, i, 0))
    return pl.pallas_call(
        functools.partial(_dnprep_body, ts=ts, n_tiles=nt),
        grid=(B, nt),
        in_specs=[row(C),
                  pl.BlockSpec((1, 8, C), lambda b, i: (b, jnp.maximum(i * hb - 1, 0), 0)),
                  pl.BlockSpec((1, 8, C), lambda b, i: (b, jnp.minimum((i + 1) * hb, S // 8 - 1), 0)),
                  _resident((CONV_W, C)), row(BA_PAD),
                  pl.BlockSpec((1, n_ba, ts), lambda b, i: (b, 0, i)),
                  _resident((2, BA_PAD)), _resident((n_ba, 2))],
        out_specs=[row(DN_WIDTH), row(DN_WIDTH), row(DN_WIDTH), row(BA_PAD),
                   pl.BlockSpec((1, n_ba, ts), lambda b, i: (b, 0, i))],
        out_shape=[jax.ShapeDtypeStruct((B, S, DN_WIDTH), F32)] * 3
        + [jax.ShapeDtypeStruct((B, S, BA_PAD), F32), jax.ShapeDtypeStruct((B, n_ba, S), F32)],
        scratch_shapes=[pltpu.VMEM((ts + 16, C), F32)],
        compiler_params=pltpu.CompilerParams(dimension_semantics=("parallel", "parallel")),
        name="dn_prep",
    )(dn_raw, dn_raw, dn_raw, conv_w.astype(F32), ba, bat, ab, abt)


def _bd_rows(y, left):
    return jnp.concatenate([jnp.where(left, y, 0.0), jnp.where(left, 0.0, y)], axis=0).astype(BF16)


def _dnscan_body(qf_ref, kf_ref, vf_ref, bgf_ref, gtf_ref, qb_ref, kb_ref, vb_ref, bgb_ref, gtb_ref,
                 of_ref, ob_ref, st_ref, *, G):
    C = DN_CHUNK
    H = DN_HEADS
    P = H // 2

    @pl.when(pl.program_id(1) == 0)
    def _():
        st_ref[...] = jnp.zeros(st_ref.shape, F32)

    lane1 = lax.broadcasted_iota(jnp.int32, (C, LANES), 1)
    row1 = lax.broadcasted_iota(jnp.int32, (C, LANES), 0)
    left1 = lane1 < C
    s_idx = lane1 % C
    eye = (row1 == s_idx).astype(F32)
    lane2 = lax.broadcasted_iota(jnp.int32, (C, 2 * LANES), 1)
    left2 = (lane2 % LANES) < C
    rr = lax.broadcasted_iota(jnp.int32, (LANES, LANES), 0)
    cc = lax.broadcasted_iota(jnp.int32, (LANES, LANES), 1)
    top = rr < C
    bdmask = (rr < C) == (cc < C)
    top_c = lax.broadcasted_iota(jnp.int32, (LANES, C), 0) < C

    dirs = ((0, qf_ref, kf_ref, vf_ref, bgf_ref, gtf_ref, of_ref),
            (1, qb_ref, kb_ref, vb_ref, bgb_ref, gtb_ref, ob_ref))
    kT_cache = {}
    by_key = {}

    def prepare(group):
        chains = []
        for d, p, c in group:
            _, q_ref, k_ref, v_ref, bg_ref, gt_ref, _ = dirs[d]
            tri = (row1 >= s_idx) if d == 0 else (row1 <= s_idx)
            strict = (row1 > s_idx) if d == 0 else (row1 < s_idx)
            last = C - 1 if d == 0 else 0
            sl = slice(p * LANES, (p + 1) * LANES)
            cb = d * H + 2 * p
            cg = 2 * H + d * H + 2 * p
            if (d, p) not in kT_cache:
                kT_cache[(d, p)] = k_ref[0, :, sl].T
            rs = slice(c * C, (c + 1) * C)
            qc, kc, vc = q_ref[0, rs, sl], k_ref[0, rs, sl], v_ref[0, rs, sl]
            bexp = jnp.where(left1, bg_ref[0, rs, cb:cb + 1], bg_ref[0, rs, cb + 1:cb + 2])
            gcc = jnp.where(left1, bg_ref[0, rs, cg:cg + 1], bg_ref[0, rs, cg + 1:cg + 2])
            g0 = gt_ref[0, cg:cg + 1, rs]
            g1 = gt_ref[0, cg + 1:cg + 2, rs]
            grow = jnp.concatenate([g0, g1], axis=1)
            glast = gcc[last:last + 1, :]
            gl0 = glast[:, 0:1]
            gl1 = glast[:, C:C + 1]
            eg = jnp.exp(gcc)
            kb_ = kc * bexp
            fac = jnp.exp(jnp.minimum(jnp.where(top_c, gl0 - g0, gl1 - g1), 0.0))
            chains.append(dict(
                key=(d, p, c),
                rhs=jnp.concatenate([vc * bexp, kb_ * eg], axis=1),
                qg=(qc * eg).astype(BF16),
                kq=jnp.concatenate([kb_, qc], axis=0).astype(BF16),
                Z=_bd_rows(kc, left1),
                dec=jnp.where(tri, jnp.exp(jnp.minimum(gcc - grow, 0.0)), 0.0),
                strict=strict,
                kg=(kT_cache[(d, p)][:, rs] * fac).astype(BF16),
                egl=jnp.exp(glast)))
        yield
        for ch in chains:
            kkqk = _dot_nt(ch["kq"], ch["Z"])
            ch["N"] = jnp.where(ch["strict"], -kkqk[:C] * ch["dec"], 0.0)
            ch["qk"] = (kkqk[C:] * ch["dec"]).astype(BF16)
        for ch in chains:
            ch["Pm"] = eye + ch["N"]
        for j in range(6):
            yield
            for ch in chains:
                nb = _bd_rows(ch["N"], left1)
                if j == 0:
                    ch["N"] = _dot(ch["N"].astype(BF16), nb)
                elif j < 5:
                    both = _dot(jnp.concatenate([ch["Pm"], ch["N"]], axis=0).astype(BF16), nb)
                    ch["Pm"] = ch["Pm"] + both[:C]
                    ch["N"] = both[C:]
                else:
                    ch["Pm"] = ch["Pm"] + _dot(ch["Pm"].astype(BF16), nb)
        yield
        for ch in chains:
            X = _dot(ch["Pm"].astype(BF16), _bd_rows(ch["rhs"], left2))
            by_key[ch["key"]] = dict(
                u=X[:, :LANES], qk=ch["qk"], kg=ch["kg"], egl=ch["egl"],
                wq=jnp.concatenate([X[:, LANES:].astype(BF16), ch["qg"]], axis=0))

    def scan_step(step):
        live = []
        for d, *_refs, o_ref in dirs:
            c = step if d == 0 else G - 1 - step
            for p in range(P):
                live.append((d * P + p, o_ref, slice(c * C, (c + 1) * C), p, by_key[(d, p, c)]))
        S = [st_ref[sidx] for sidx, *_ in live]
        t1 = [_dot(ch["wq"], S[i].astype(BF16)) for i, (*_, ch) in enumerate(live)]
        vn = [ch["u"] - t1[i][:C] for i, (*_, ch) in enumerate(live)]
        for i, (sidx, o_ref, rs, p, ch) in enumerate(live):
            o_ref[0, rs, p * LANES:(p + 1) * LANES] = t1[i][C:] + _dot(ch["qk"], _bd_rows(vn[i], left1))
        for i, (sidx, o_ref, rs, p, ch) in enumerate(live):
            upd = _dot(ch["kg"], vn[i].astype(BF16))
            decay = jnp.where(top, ch["egl"][:, 0:1], ch["egl"][:, C:C + 1])
            st_ref[sidx] = S[i] * decay + jnp.where(bdmask, upd, 0.0)

    bounds = sorted({0, G} | {G - (G * i) // DN_SCAN_GROUPS for i in range(1, DN_SCAN_GROUPS)})
    groups = [range(a, b) for a, b in zip(bounds[:-1], bounds[1:])]
    pending = iter(())
    for steps in groups:
        needed = [(0, p, c) for p in range(P) for c in steps]
        needed += [(1, p, G - 1 - c) for p in range(P) for c in steps]
        for _ in prepare(needed):
            step = next(pending, None)
            if step is not None:
                scan_step(step)
        for step in pending:
            scan_step(step)
        pending = iter(steps)
    for step in pending:
        scan_step(step)


def _dn_scan(q, k, v, bg, gt, *, G=16):
    B, S, _ = q.shape
    G = min(G, S // DN_CHUNK)
    R = G * DN_CHUNK
    nb = S // R
    n_ba = 4 * DN_HEADS
    P = DN_HEADS // 2
    fw = lambda n: pl.BlockSpec((1, R, n), lambda b, i: (b, i, 0))
    bw = lambda n: pl.BlockSpec((1, R, n), lambda b, i: (b, nb - 1 - i, 0))
    fwt = pl.BlockSpec((1, n_ba, R), lambda b, i: (b, 0, i))
    bwt = pl.BlockSpec((1, n_ba, R), lambda b, i: (b, 0, nb - 1 - i))
    return pl.pallas_call(
        functools.partial(_dnscan_body, G=G),
        grid=(B, nb),
        in_specs=[fw(DN_WIDTH), fw(DN_WIDTH), fw(DN_WIDTH), fw(BA_PAD), fwt,
                  bw(DN_WIDTH), bw(DN_WIDTH), bw(DN_WIDTH), bw(BA_PAD), bwt],
        out_specs=[fw(DN_WIDTH), bw(DN_WIDTH)],
        out_shape=[jax.ShapeDtypeStruct((B, S, DN_WIDTH), F32)] * 2,
        scratch_shapes=[pltpu.VMEM((2 * P, LANES, LANES), F32)],
        compiler_params=pltpu.CompilerParams(dimension_semantics=("parallel", "arbitrary")),
        name="dn_scan",
    )(q, k, v, bg, gt, q, k, v, bg, gt)


def _mixers(x, ln_mix, w_mix_in, conv_w, sink, diff_lam, diff_g, a_log, dt_bias, lam_init):
    B, S, _ = x.shape
    wq, wk, wv, dq, dkt, dv, dn_raw, z, ba, bat = _inproj(x, ln_mix, w_mix_in)
    win_slopes = 2.0 ** (-8.0 * jnp.arange(1, WIN_Q_HEADS + 1, dtype=F32) / WIN_Q_HEADS)
    o_win = _win_attention(wq, wk, wv, sink, win_slopes)
    o_diff = _diff_attention(dq, dkt, dv, diff_lam, diff_g, lam_init)
    q, k, v, bg, gt = _dn_prep(dn_raw, ba, bat, conv_w, a_log, dt_bias)
    o_f, o_b = _dn_scan(q, k, v, bg, gt)
    return tuple(t.reshape(B * S, t.shape[-1]) for t in (o_win, o_diff, o_f, o_b, z))


def kernel(x, ln_ffn1, ffn1_w_in, ffn1_w_out, ln_mix, w_mix_in, conv_w, sink_logits, diff_lambda,
           diff_norm_g, dn_A_log, dn_dt_bias, dn_norm_g, w_mix_out, ln_ffn2, ffn2_w_in, ffn2_w_out,
           ln_final):
    B, S, D = x.shape
    depth = ln_ffn1.shape[0]
    for l in range(depth):
        lam_init = 0.8 - 0.6 * math.exp(-0.3 * l)
        x = _ffn(x.reshape(B * S, D), ln_ffn1[l], ffn1_w_in[l], ffn1_w_out[l]).reshape(B, S, D)
        mixed = _mixers(x, ln_mix[l], w_mix_in[l], conv_w[l], sink_logits[l], diff_lambda[l], diff_norm_g[l],
                        dn_A_log[l], dn_dt_bias[l], lam_init)
        g_final = ln_final if l == depth - 1 else None
        x = _ffn(x.reshape(B * S, D), ln_ffn2[l], ffn2_w_in[l], ffn2_w_out[l], g_final,
                 mix=mixed + (dn_norm_g[l], w_mix_out[l])).reshape(B, S, D)
    return x
```

```python
import functools
import math

import jax
import jax.numpy as jnp
from jax import lax
from jax.experimental import pallas as pl
from jax.experimental.pallas import tpu as pltpu

F32 = jnp.float32
BF16 = jnp.bfloat16

D_MODEL = 1024
HEAD_DIM = 64
EPS = 1e-6
WIN_Q_HEADS = 6
WIN_KV_HEADS = 2
WINDOW = 128
DIFF_HEADS = 4
DIFF_QK_DIM = HEAD_DIM // 2
DN_HEADS = 6
DN_CHUNK = 64
CONV_W = 5
D_FF = 2752
WIN_WIDTH = WIN_Q_HEADS * HEAD_DIM
WIN_KV_WIDTH = WIN_KV_HEADS * HEAD_DIM
DIFF_WIDTH = DIFF_HEADS * HEAD_DIM
DN_WIDTH = DN_HEADS * HEAD_DIM
DN_QKV = 3 * DN_WIDTH
MIX_WIDTH = WIN_WIDTH + DIFF_WIDTH + DN_WIDTH
OFF_WQ = 0
OFF_WK = OFF_WQ + WIN_WIDTH
OFF_WV = OFF_WK + WIN_KV_WIDTH
OFF_DQ = OFF_WV + WIN_KV_WIDTH
OFF_DK = OFF_DQ + DIFF_WIDTH
OFF_DV = OFF_DK + DIFF_WIDTH
OFF_DN = OFF_DV + DIFF_WIDTH
OFF_Z = OFF_DN + DN_QKV
OFF_BA = OFF_Z + DN_WIDTH
MIX_IN = OFF_BA + 4 * DN_HEADS

LANES = 128
SUBLANES = 8
BF16_ROWS = 2 * SUBLANES
FF_CHUNK = 256
D_FF_PAD = -(-D_FF // FF_CHUNK) * FF_CHUNK
VMEM_LIMIT = 56 * 1024 * 1024
LOG2E = math.log2(math.e)
DIFF_TK = 512
DIFF_BIAS_ROWS = 3
DIFF_KROWS = -(-(DIFF_QK_DIM + DIFF_BIAS_ROWS) // BF16_ROWS) * BF16_ROWS
DIFF_KT_ROWS = 2 * DIFF_HEADS * DIFF_KROWS
DIFF_V_WIDTH = DIFF_HEADS * LANES
DIFF_SKIP_LOG2 = 150.0
DIFF_BOUND_SLACK = 1.0 + 1e-3
MASKED = -1e30
DIFF_SLOPES = tuple(2.0 ** (-8.0 * (i + 1) / DIFF_HEADS) for i in range(DIFF_HEADS))
DN_SCAN_GROUPS = 5


def _rms(x, g):
    return x * lax.rsqrt(jnp.mean(x * x, axis=-1, keepdims=True) + EPS) * g


def _silu(x):
    h = 0.5 * x
    return h + h * jnp.tanh(h)


def _seg_sum64(x):
    R, width = x.shape
    n = width // LANES
    rows = jnp.concatenate([x[:, c * LANES:(c + 1) * LANES] for c in range(n)], axis=0)
    r = lax.broadcasted_iota(jnp.int32, (LANES, LANES), 0) < HEAD_DIM
    c = lax.broadcasted_iota(jnp.int32, (LANES, LANES), 1) < HEAD_DIM
    ones_bd = (r == c).astype(F32).astype(BF16)
    hi = rows.astype(BF16)
    r1 = rows - hi.astype(F32)
    mid = r1.astype(BF16)
    lo = (r1 - mid.astype(F32)).astype(BF16)
    ss = _dot(hi, ones_bd) + _dot(mid, ones_bd) + _dot(lo, ones_bd)
    return jnp.concatenate([ss[c * R:(c + 1) * R] for c in range(n)], axis=1)


def _dot(a, b):
    return jnp.dot(a, b, preferred_element_type=F32)


def _dot_nt(a, b):
    return lax.dot_general(a, b, (((1,), (1,)), ((), ())), preferred_element_type=F32)


def _resident(shape):
    nd = len(shape)
    return pl.BlockSpec(shape, lambda *_: (0,) * nd, pipeline_mode=pl.Buffered(1))


def _ffn_body(*refs, n_chunks, final, mixed):
    refs = iter(refs)
    x_ref = next(refs)
    mix_refs = [next(refs) for _ in range(7)] if mixed else None
    g_ref, wg_ref, wu_ref, wo_ref = (next(refs) for _ in range(4))
    gf_ref = next(refs) if final else None
    o_ref, acc_ref = next(refs), next(refs)
    x = x_ref[...]
    if mixed:
        ow_ref, od_ref, of_ref, ob_ref, z_ref, dng_ref, wmix_ref = mix_refs
        cat_ref = next(refs)
        oc = of_ref[...] + ob_ref[...]
        on = oc * lax.rsqrt(_seg_sum64(oc * oc) * (1.0 / HEAD_DIM) + EPS) * dng_ref[...] * _silu(z_ref[...])
        cat_ref[:, :WIN_WIDTH] = ow_ref[...]
        cat_ref[:, WIN_WIDTH:WIN_WIDTH + DIFF_WIDTH] = od_ref[...]
        cat_ref[:, WIN_WIDTH + DIFF_WIDTH:] = on.astype(BF16)
        x = x + _dot(cat_ref[...], wmix_ref[...])
    h = _rms(x, g_ref[...]).astype(BF16)
    for c in range(n_chunks):
        cols = slice(c * FF_CHUNK, (c + 1) * FF_CHUNK)
        a = (_silu(_dot(h, wg_ref[:, cols])) * _dot(h, wu_ref[:, cols])).astype(BF16)
        part = _dot(a, wo_ref[cols, :])
        if c == 0:
            acc_ref[...] = part
        else:
            acc_ref[...] += part
    y = x + 0.5 * acc_ref[...]
    if final:
        y = _rms(y, gf_ref[...])
    o_ref[...] = y


def _prep_ffn_weights(w_in, w_out):
    pad = D_FF_PAD - D_FF
    wg = jnp.pad(w_in[:, :D_FF].astype(BF16), ((0, 0), (0, pad)))
    wu = jnp.pad(w_in[:, D_FF:].astype(BF16), ((0, 0), (0, pad)))
    wo = jnp.pad(w_out.astype(BF16), ((0, pad), (0, 0)))
    return wg, wu, wo


def _ffn(x2d, g, w_in, w_out, g_final=None, mix=None, *, tm=512):
    T = x2d.shape[0]
    tm = min(tm, T)
    n_chunks = D_FF_PAD // FF_CHUNK
    wg, wu, wo = _prep_ffn_weights(w_in, w_out)
    final = g_final is not None
    mixed = mix is not None
    row = lambda n: pl.BlockSpec((tm, n), lambda i: (i, 0))
    in_specs = [row(D_MODEL)]
    args = [x2d]
    scratch = [pltpu.VMEM((tm, D_MODEL), F32)]
    if mixed:
        o_win, o_diff, o_f, o_b, z, dn_g, w_mix_out = mix
        in_specs += [row(WIN_WIDTH), row(DIFF_WIDTH), row(DN_WIDTH), row(DN_WIDTH), row(DN_WIDTH),
                     _resident((1, DN_WIDTH)), _resident((MIX_WIDTH, D_MODEL))]
        args += [o_win, o_diff, o_f, o_b, z, jnp.tile(dn_g.astype(F32), DN_HEADS).reshape(1, DN_WIDTH),
                 w_mix_out.astype(BF16)]
        scratch.append(pltpu.VMEM((tm, MIX_WIDTH), BF16))
    in_specs += [_resident((1, D_MODEL)), _resident(wg.shape), _resident(wu.shape), _resident(wo.shape)]
    args += [g.reshape(1, D_MODEL), wg, wu, wo]
    if final:
        in_specs.append(_resident((1, D_MODEL)))
        args.append(g_final.reshape(1, D_MODEL))
    return pl.pallas_call(
        functools.partial(_ffn_body, n_chunks=n_chunks, final=final, mixed=mixed),
        grid=(T // tm,),
        in_specs=in_specs,
        out_specs=row(D_MODEL),
        out_shape=jax.ShapeDtypeStruct((T, D_MODEL), F32),
        scratch_shapes=scratch,
        compiler_params=pltpu.CompilerParams(
            dimension_semantics=("parallel",), vmem_limit_bytes=VMEM_LIMIT),
        name="ffn" + ("_mix" if mixed else "") + ("_final" if final else ""),
    )(*args)


BA_PAD = LANES


def _inproj_body(x_ref, g_ref, w_ref, wt_ref, wdv_ref,
                 wq_ref, wk_ref, wv_ref, dq_ref, dkt_ref, dv_ref, dn_ref, z_ref, ba_ref, bat_ref):
    h = _rms(x_ref[0], g_ref[...]).astype(BF16)
    wq_ref[0] = (_dot(h, w_ref[:, OFF_WQ:OFF_WK]) * (HEAD_DIM ** -0.5 * LOG2E)).astype(BF16)
    wk_ref[0] = _dot(h, w_ref[:, OFF_WK:OFF_WV]).astype(BF16)
    wv_ref[0] = _dot(h, w_ref[:, OFF_WV:OFF_DQ]).astype(BF16)
    dq_ref[0] = (_dot(h, w_ref[:, OFF_DQ:OFF_DK]) * (DIFF_QK_DIM ** -0.5 * LOG2E)).astype(BF16)
    wt_out = lax.dot_general(wt_ref[...], h, (((0,), (1,)), ((), ())),
                             preferred_element_type=F32)
    kt = wt_out[:DIFF_WIDTH]
    tm = kt.shape[1]
    extra_rows = DIFF_KROWS - DIFF_QK_DIM
    pos = pl.program_id(1) * tm + lax.broadcasted_iota(jnp.int32, (extra_rows, tm), 1)
    kr = (pos % DIFF_TK).astype(F32)
    rowi = lax.broadcasted_iota(jnp.int32, (extra_rows, tm), 0)
    for hd in range(DIFF_HEADS):
        bias = (DIFF_SLOPES[hd] * LOG2E) * kr
        hi = bias.astype(BF16).astype(F32)
        mid = (bias - hi).astype(BF16).astype(F32)
        lo = bias - hi - mid
        extra = jnp.where(rowi == 0, hi, jnp.where(rowi == 1, mid, jnp.where(rowi == 2, lo, 0.0))).astype(BF16)
        for mp in range(2):
            hm = 2 * hd + mp
            dkt_ref[0, hm * DIFF_KROWS:hm * DIFF_KROWS + DIFF_QK_DIM, :] = (
                kt[hm * DIFF_QK_DIM:(hm + 1) * DIFF_QK_DIM, :].astype(BF16))
            dkt_ref[0, hm * DIFF_KROWS + DIFF_QK_DIM:(hm + 1) * DIFF_KROWS, :] = extra
    dv = _dot(h, wdv_ref[...])
    ones_half = (lax.broadcasted_iota(jnp.int32, dv.shape, 1) % LANES) >= HEAD_DIM
    dv_ref[0] = jnp.where(ones_half, 1.0, dv).astype(BF16)
    dn_ref[0] = _dot(h, w_ref[:, OFF_DN:OFF_Z])
    z_ref[0] = _dot(h, w_ref[:, OFF_Z:OFF_BA])
    ba_ref[0] = _dot(h, w_ref[:, OFF_BA:OFF_BA + BA_PAD])
    bat_ref[0] = wt_out[DIFF_WIDTH:DIFF_WIDTH + bat_ref.shape[1]]


def _inproj(x, g, w_in, *, tm=1024):
    B, S, _ = x.shape
    tm = min(tm, S)
    n_ba = 4 * DN_HEADS
    w_in = w_in.astype(BF16)
    w = jnp.pad(w_in, ((0, 0), (0, OFF_BA + BA_PAD - MIX_IN)))
    wt = jnp.pad(jnp.concatenate([w_in[:, OFF_DK:OFF_DV], w_in[:, OFF_BA:MIX_IN]], axis=1),
                 ((0, 0), (0, 2 * BF16_ROWS - n_ba % (2 * BF16_ROWS))))
    wdv = jnp.pad(w_in[:, OFF_DV:OFF_DN].reshape(D_MODEL, DIFF_HEADS, HEAD_DIM),
                  ((0, 0), (0, 0), (0, LANES - HEAD_DIM))).reshape(D_MODEL, DIFF_V_WIDTH)
    row = lambda n: pl.BlockSpec((1, tm, n), lambda b, i: (b, i, 0))
    col = lambda n: pl.BlockSpec((1, n, tm), lambda b, i: (b, 0, i))
    out_shapes = [
        jax.ShapeDtypeStruct((B, S, WIN_WIDTH), BF16),
        jax.ShapeDtypeStruct((B, S, WIN_KV_WIDTH), BF16),
        jax.ShapeDtypeStruct((B, S, WIN_KV_WIDTH), BF16),
        jax.ShapeDtypeStruct((B, S, DIFF_WIDTH), BF16),
        jax.ShapeDtypeStruct((B, DIFF_KT_ROWS, S), BF16),
        jax.ShapeDtypeStruct((B, S, DIFF_V_WIDTH), BF16),
        jax.ShapeDtypeStruct((B, S, DN_QKV), F32),
        jax.ShapeDtypeStruct((B, S, DN_WIDTH), F32),
        jax.ShapeDtypeStruct((B, S, BA_PAD), F32),
        jax.ShapeDtypeStruct((B, n_ba, S), F32),
    ]
    out_specs = [row(WIN_WIDTH), row(WIN_KV_WIDTH), row(WIN_KV_WIDTH), row(DIFF_WIDTH),
                 col(DIFF_KT_ROWS), row(DIFF_V_WIDTH), row(DN_QKV), row(DN_WIDTH), row(BA_PAD), col(n_ba)]
    return pl.pallas_call(
        _inproj_body,
        grid=(B, S // tm),
        in_specs=[row(D_MODEL), _resident((1, D_MODEL)), _resident(w.shape),
                  _resident(wt.shape), _resident(wdv.shape)],
        out_specs=out_specs,
        out_shape=out_shapes,
        compiler_params=pltpu.CompilerParams(
            dimension_semantics=("parallel", "parallel"), vmem_limit_bytes=VMEM_LIMIT),
        name="inproj",
    )(x, g.reshape(1, D_MODEL), w, wt, wdv)


def _win_body(q_ref, kp_ref, kc_ref, kn_ref, vp_ref, vc_ref, vn_ref, sink_ref, slope_ref, o_ref,
              *, n_steps, nq):
    i = pl.program_id(1)
    W = WINDOW
    KW = 3 * W
    kw = jnp.concatenate([kp_ref[0], kc_ref[0], kn_ref[0]], axis=0).astype(F32)
    vw = jnp.concatenate([vp_ref[0], vc_ref[0], vn_ref[0]], axis=0).astype(F32)
    kw_sw = pltpu.roll(kw, HEAD_DIM, 1)
    vw_sw = pltpu.roll(vw, HEAD_DIM, 1)
    left = lax.broadcasted_iota(jnp.int32, (KW, LANES), 1) < HEAD_DIM
    qi = lax.broadcasted_iota(jnp.int32, (W, KW), 0)
    ki = lax.broadcasted_iota(jnp.int32, (W, KW), 1)
    dist = jnp.abs(ki - W - qi)
    band_bias = [jnp.where(dist <= W, (-LOG2E) * slope_ref[:, hq:hq + 1] * dist.astype(F32), MASKED)
                 for hq in range(WIN_Q_HEADS)]
    edge_lo = jnp.where(jnp.logical_or(ki >= W, i > 0), 0.0, MASKED)
    edge_hi = jnp.where(jnp.logical_or(ki < 2 * W, i < n_steps - 1), 0.0, MASKED)
    n_pairs = WIN_Q_HEADS // 2
    ones_stack = jnp.concatenate([jnp.where(left, 1.0, 0.0), jnp.where(left, 0.0, 1.0)], axis=0).astype(BF16)
    blocks = []
    for a in range(nq):
        rows = slice(a * W, a * W + KW)

        def stack(first, second, rows=rows):
            return jnp.concatenate([jnp.where(left, first[rows], 0.0),
                                    jnp.where(left, 0.0, second[rows])], axis=0).astype(BF16)

        zk = (stack(kw, kw_sw), stack(kw, kw), stack(kw_sw, kw))
        zv = tuple(jnp.concatenate([z, ones_stack], axis=1)
                   for z in (stack(vw, vw_sw), stack(vw, vw), stack(vw_sw, vw)))
        edge = None
        if a == 0:
            edge = edge_lo
        if a == nq - 1:
            edge = edge_hi if edge is None else edge + edge_hi
        q = q_ref[0, a * W:(a + 1) * W, :]
        s2 = [_dot_nt(q[:, p * LANES:(p + 1) * LANES], zk[p]) for p in range(n_pairs)]
        blocks.append((a, edge, zv, s2))
    weights, sink_terms = {}, {}
    for a, edge, zv, s2 in blocks:
        for hq in range(WIN_Q_HEADS):
            p, t = divmod(hq, 2)
            s = s2[p][:, t * KW:(t + 1) * KW] + band_bias[hq]
            if edge is not None:
                s = s + edge
            sink = LOG2E * sink_ref[:, hq:hq + 1]
            m = jnp.maximum(jnp.max(s, axis=-1, keepdims=True), sink)
            weights[(a, hq)] = jnp.exp2(s - m).astype(BF16)
            sink_terms[(a, hq)] = jnp.exp2(sink - m)
    left_w = lax.broadcasted_iota(jnp.int32, (W, LANES), 1) < HEAD_DIM
    for a, edge, zv, s2 in blocks:
        for p in range(n_pairs):
            ee = jnp.concatenate([weights[(a, 2 * p)], weights[(a, 2 * p + 1)]], axis=1)
            acc = _dot(ee, zv[p])
            denom = acc[:, LANES:] + jnp.where(left_w, sink_terms[(a, 2 * p)], sink_terms[(a, 2 * p + 1)])
            o_ref[0, a * W:(a + 1) * W, p * LANES:(p + 1) * LANES] = (acc[:, :LANES] / denom).astype(BF16)


def _win_attention(q, k, v, sink, slopes, *, nq=8):
    B, S, _ = q.shape
    W = WINDOW
    nq = min(nq, S // W)
    ns = S // (nq * W)
    nb = S // W
    qspec = pl.BlockSpec((1, nq * W, WIN_WIDTH), lambda b, i: (b, i, 0))
    prev = pl.BlockSpec((1, W, WIN_KV_WIDTH), lambda b, i: (b, jnp.maximum(i * nq - 1, 0), 0))
    cur = pl.BlockSpec((1, nq * W, WIN_KV_WIDTH), lambda b, i: (b, i, 0))
    nxt = pl.BlockSpec((1, W, WIN_KV_WIDTH), lambda b, i: (b, jnp.minimum((i + 1) * nq, nb - 1), 0))
    return pl.pallas_call(
        functools.partial(_win_body, n_steps=ns, nq=nq),
        grid=(B, ns),
        in_specs=[qspec, prev, cur, nxt, prev, cur, nxt,
                  _resident((1, WIN_Q_HEADS)), _resident((1, WIN_Q_HEADS))],
        out_specs=pl.BlockSpec((1, nq * W, WIN_WIDTH), lambda b, i: (b, i, 0)),
        out_shape=jax.ShapeDtypeStruct((B, S, WIN_WIDTH), BF16),
        compiler_params=pltpu.CompilerParams(dimension_semantics=("parallel", "parallel")),
        name="win_attn",
    )(q, k, k, k, v, v, v, sink.reshape(1, WIN_Q_HEADS).astype(F32), slopes.reshape(1, WIN_Q_HEADS))


def _diff_ranges(q, kt, tq, tk):
    B, S, _ = q.shape
    nq, nk, n_maps = S // tq, S // tk, 2 * DIFF_HEADS
    qf = q.astype(F32).reshape(B, nq, tq, n_maps, DIFF_QK_DIM)
    qn = jnp.sqrt(jnp.max(jnp.sum(qf * qf, axis=-1), axis=2))
    kf = kt.astype(F32).reshape(B, n_maps, DIFF_KROWS, S)[:, :, :DIFF_QK_DIM]
    kn_pos = jnp.sqrt(jnp.sum(kf * kf, axis=2))
    kn = jnp.max(kn_pos.reshape(B, n_maps, nk, tk), axis=-1)
    kown = jnp.max(kn_pos.reshape(B, n_maps, nq, tq), axis=-1)
    reach = jnp.swapaxes(qn, 1, 2)[..., None] * (kn[:, :, None, :] + kown[..., None]) * DIFF_BOUND_SLACK
    reach = jnp.max(reach.reshape(B, DIFF_HEADS, 2, nq, nk), axis=2)
    q0 = jnp.arange(nq, dtype=jnp.int32) * tq
    k0 = jnp.arange(nk, dtype=jnp.int32) * tk
    dmin = jnp.maximum(jnp.maximum(k0[None, :] - (q0[:, None] + tq - 1), q0[:, None] - (k0[None, :] + tk - 1)), 0)
    c = jnp.asarray(DIFF_SLOPES, F32) * LOG2E
    need = reach - c[None, :, None, None] * dmin.astype(F32)[None, None] > -DIFF_SKIP_LOG2
    idx = jnp.arange(nk, dtype=jnp.int32)
    need = jnp.logical_or(need, (idx[None, :] == (q0 // tk)[:, None])[None, None])
    lo = jnp.min(jnp.where(need, idx, nk), axis=-1)
    hi = jnp.max(jnp.where(need, idx, -1), axis=-1)
    return jnp.stack([lo, hi], axis=-1).transpose(0, 2, 1, 3).reshape(-1).astype(jnp.int32)


def _diff_body(rng_ref, q_ref, kt_ref, v_ref, lam_ref, g_ref, o_ref, qv_ref, m_ref, acc_ref,
               s0_ref, st0_ref, s1_ref, st1_ref, *, tq, tk, lam_init):
    n_maps = 2 * DIFF_HEADS
    n_groups = tk // LANES
    q0 = pl.program_id(1) * tq
    jd = q0 // tk
    q = q_ref[0]
    lane_x = lax.broadcasted_iota(jnp.int32, (tq, DIFF_KROWS - DIFF_QK_DIM), 1)
    for var, sign in enumerate((1.0, -1.0, 0.0)):
        extras = jnp.where(lane_x < DIFF_BIAS_ROWS, -sign, 0.0).astype(BF16)
        for hm in range(n_maps):
            qv_ref[var, hm, :, 0:DIFF_QK_DIM] = q[:, hm * DIFF_QK_DIM:(hm + 1) * DIFF_QK_DIM]
            qv_ref[var, hm, :, DIFF_QK_DIM:DIFF_KROWS] = extras
    m_ref[...] = jnp.full(m_ref.shape, -jnp.inf, F32)
    acc_ref[...] = jnp.zeros(acc_ref.shape, F32)
    qpos = (q0 + lax.broadcasted_iota(jnp.int32, (tq, LANES), 0)).astype(F32)
    bufs = ((s0_ref, st0_ref), (s1_ref, st1_ref))
    rng_base = (pl.program_id(0) * pl.num_programs(1) + pl.program_id(1)) * (2 * DIFF_HEADS)

    los = [rng_ref[rng_base + 2 * h] for h in range(DIFF_HEADS)]
    n_offs = [rng_ref[rng_base + 2 * h + 1] - los[h] for h in range(DIFF_HEADS)]
    starts = [0]
    for h in range(DIFF_HEADS):
        starts.append(starts[-1] + n_offs[h])
    n_total = starts[DIFF_HEADS]

    def pick(h, vals):
        out = vals[-1]
        for i in range(len(vals) - 2, -1, -1):
            out = jnp.where(h == i, vals[i], out)
        return out

    def locate(u):
        h = sum((u >= starts[i]).astype(jnp.int32) for i in range(1, DIFF_HEADS))
        t = pick(h, los) + u - pick(h, starts[:DIFF_HEADS])
        return h, t + (t >= jd).astype(jnp.int32)

    def scores(h, j, buf, diag_dist=None):
        s_ref, st_ref = bufs[buf]
        if diag_dist is not None:
            var = 2
            bias = (DIFF_SLOPES[h] * LOG2E) * diag_dist
        else:
            c = pick(h, [sl * LOG2E for sl in DIFF_SLOPES])
            after = j > jd
            var = jnp.where(after, 0, 1)
            rowoff = jnp.where(after, c, -c) * ((j * tk).astype(F32) - qpos)
        k0 = pl.multiple_of(j * tk, tk)
        for mp in range(2):
            hm = 2 * h + mp
            r0 = hm * DIFF_KROWS if diag_dist is not None else pl.multiple_of(hm * DIFF_KROWS, BF16_ROWS)
            kt = kt_ref[0, pl.ds(r0, DIFF_KROWS), pl.ds(k0, tk)]
            s = _dot(qv_ref[var, hm], kt)
            if diag_dist is not None:
                s = s - bias
            s_ref[mp] = s
            mx = s[:, 0:LANES]
            for g in range(1, n_groups):
                mx = jnp.maximum(mx, s[:, g * LANES:(g + 1) * LANES])
            mrow = jnp.broadcast_to(jnp.max(mx, axis=-1, keepdims=True), (tq, LANES))
            m_old = m_ref[hm]
            if diag_dist is not None:
                m_new = jnp.maximum(m_old, mrow)
                shift = m_new
            else:
                m_new = jnp.maximum(m_old, mrow - rowoff)
                shift = m_new + rowoff
            st_ref[mp, 0] = shift
            st_ref[mp, 1] = jnp.exp2(m_old - m_new)
            m_ref[hm] = m_new

    def values(h, j, buf, static=False):
        s_ref, st_ref = bufs[buf]
        k0 = pl.multiple_of(j * tk, tk)
        c0 = h * LANES if static else pl.multiple_of(h * LANES, LANES)
        v2 = v_ref[0, pl.ds(k0, tk), pl.ds(c0, LANES)]
        for mp in range(2):
            hm = 2 * h + mp
            shift = st_ref[mp, 0]
            alpha = st_ref[mp, 1]
            e = jnp.exp2(s_ref[mp] - jnp.concatenate([shift] * n_groups, axis=1))
            acc_ref[hm] = alpha * acc_ref[hm] + _dot(e.astype(BF16), v2)

    kpos = (jd * tk + lax.broadcasted_iota(jnp.int32, (tq, tk), 1)).astype(F32)
    qp = (q0 + lax.broadcasted_iota(jnp.int32, (tq, tk), 0)).astype(F32)
    diag_dist = jnp.abs(qp - kpos)
    scores(0, jd, 0, diag_dist)
    for h in range(DIFF_HEADS):
        if h + 1 < DIFF_HEADS:
            scores(h + 1, jd, (h + 1) % 2, diag_dist)
        values(h, jd, h % 2, static=True)

    def s_at(u, buf):
        h, j = locate(u)
        scores(h, j, buf)

    def v_at(u, buf):
        h, j = locate(u)
        values(h, j, buf)

    @pl.when(n_total > 0)
    def _():
        s_at(0, 0)

    def pair(p, carry):
        u = 2 * p
        s_at(u + 1, 1)
        v_at(u, 0)
        s_at(u + 2, 0)
        v_at(u + 1, 1)
        return carry

    n_pairs = jnp.maximum(n_total - 1, 0) // 2
    lax.fori_loop(0, n_pairs, pair, 0)
    u_last = 2 * n_pairs
    left_over = n_total - u_last

    @pl.when(left_over == 2)
    def _():
        s_at(u_last + 1, 1)
        v_at(u_last, 0)
        v_at(u_last + 1, 1)

    @pl.when(left_over == 1)
    def _():
        v_at(u_last, 0)

    lp = lam_ref[...]
    lam = (jnp.exp(jnp.sum(lp[0:1] * lp[1:2], axis=-1, keepdims=True))
           - jnp.exp(jnp.sum(lp[2:3] * lp[3:4], axis=-1, keepdims=True)) + lam_init)
    lane = lax.broadcasted_iota(jnp.int32, (tq, LANES), 1)
    left = lane < HEAD_DIM
    ys = []
    for h in range(DIFF_HEADS):
        a1, a2 = acc_ref[2 * h], acc_ref[2 * h + 1]
        od = a1 / pltpu.roll(a1, HEAD_DIM, 1) - lam * (a2 / pltpu.roll(a2, HEAD_DIM, 1))
        ms = jnp.sum(jnp.where(left, od * od, 0.0), axis=-1, keepdims=True) * (1.0 / HEAD_DIM)
        ys.append(od * lax.rsqrt(ms + EPS) * g_ref[...] * (1.0 - lam_init))
    for p in range(DIFF_HEADS // 2):
        o_ref[0, :, p * LANES:(p + 1) * LANES] = jnp.where(
            left, ys[2 * p], pltpu.roll(ys[2 * p + 1], HEAD_DIM, 1)).astype(BF16)


def _diff_attention(q, kt, v, diff_lambda, diff_g, lam_init, *, tq=512):
    B, S, _ = q.shape
    tk = DIFF_TK
    tq = min(tq, S)
    assert S % tk == 0 and tk % tq == 0
    n_maps = 2 * DIFF_HEADS
    g2 = jnp.concatenate([diff_g, diff_g]).reshape(1, LANES).astype(F32)
    grid_spec = pltpu.PrefetchScalarGridSpec(
        num_scalar_prefetch=1,
        grid=(B, S // tq),
        in_specs=[pl.BlockSpec((1, tq, DIFF_WIDTH), lambda b, i, r: (b, i, 0)),
                  pl.BlockSpec((1, DIFF_KT_ROWS, S), lambda b, i, r: (b, 0, 0)),
                  pl.BlockSpec((1, S, DIFF_V_WIDTH), lambda b, i, r: (b, 0, 0)),
                  pl.BlockSpec((4, DIFF_QK_DIM), lambda b, i, r: (0, 0)),
                  pl.BlockSpec((1, LANES), lambda b, i, r: (0, 0))],
        out_specs=pl.BlockSpec((1, tq, DIFF_WIDTH), lambda b, i, r: (b, i, 0)),
        scratch_shapes=[pltpu.VMEM((3, n_maps, tq, DIFF_KROWS), BF16),
                        pltpu.VMEM((n_maps, tq, LANES), F32),
                        pltpu.VMEM((n_maps, tq, LANES), F32),
                        pltpu.VMEM((2, tq, tk), F32), pltpu.VMEM((2, 2, tq, LANES), F32),
                        pltpu.VMEM((2, tq, tk), F32), pltpu.VMEM((2, 2, tq, LANES), F32)])
    return pl.pallas_call(
        functools.partial(_diff_body, tq=tq, tk=tk, lam_init=lam_init),
        grid_spec=grid_spec,
        out_shape=jax.ShapeDtypeStruct((B, S, DIFF_WIDTH), BF16),
        compiler_params=pltpu.CompilerParams(
            dimension_semantics=("parallel", "arbitrary"), vmem_limit_bytes=VMEM_LIMIT),
        name="diff_attn",
    )(_diff_ranges(q, kt, tq, tk), q, kt, v, diff_lambda.astype(F32), g2)


def _l2n64(x):
    return x * lax.rsqrt(_seg_sum64(x * x) + EPS)


def _dnprep_body(x_ref, xp_ref, xn_ref, w_ref, ba_ref, bat_ref, ab_ref, abt_ref,
                 q_ref, k_ref, v_ref, bg_ref, gt_ref, xx_ref, *, ts, n_tiles):
    i = pl.program_id(1)
    H = DN_HEADS
    halo = SUBLANES
    xx_ref[0:halo, :] = jnp.where(i > 0, xp_ref[0], 0.0)
    xx_ref[halo:halo + ts, :] = x_ref[0]
    xx_ref[halo + ts:2 * halo + ts, :] = jnp.where(i < n_tiles - 1, xn_ref[0], 0.0)
    half = CONV_W // 2
    y = w_ref[half:half + 1, :] * x_ref[0]
    for j in range(CONV_W):
        if j != half:
            y = y + w_ref[j:j + 1, :] * xx_ref[halo - half + j:halo - half + j + ts, :]
    y = _silu(y)
    q_ref[0] = _l2n64(y[:, :DN_WIDTH]) * (HEAD_DIM ** -0.5)
    k_ref[0] = _l2n64(y[:, DN_WIDTH:2 * DN_WIDTH])
    v_ref[0] = y[:, 2 * DN_WIDTH:]

    def gates(raw, a_log, dt_bias):
        beta = 1.0 / (1.0 + jnp.exp(-raw))
        z = raw + dt_bias
        softplus = jnp.maximum(z, 0.0) + jnp.log(1.0 + jnp.exp(-jnp.abs(z)))
        return beta, -jnp.exp(a_log) * softplus

    ba = ba_ref[0]
    beta, g = gates(ba, ab_ref[0:1, :], ab_ref[1:2, :])
    lane = lax.broadcasted_iota(jnp.int32, ba.shape, 1)
    r = lax.broadcasted_iota(jnp.int32, ba.shape, 0) % DN_CHUNK
    bwd = jnp.logical_and(lane >= 3 * H, lane < 4 * H)
    gc = g
    for d in (1, 2, 4, 8, 16, 32):
        f = jnp.where(r >= d, pltpu.roll(gc, d, 0), 0.0)
        b = jnp.where(r < DN_CHUNK - d, pltpu.roll(gc, ts - d, 0), 0.0)
        gc = gc + jnp.where(bwd, b, f)
    bg_ref[0] = jnp.where(lane < 2 * H, beta, gc)
    bat = bat_ref[0]
    _, gt = gates(bat, abt_ref[:, 0:1], abt_ref[:, 1:2])
    row = lax.broadcasted_iota(jnp.int32, bat.shape, 0)
    c = lax.broadcasted_iota(jnp.int32, bat.shape, 1) % DN_CHUNK
    bwd_t = row >= 3 * H
    for d in (1, 2, 4, 8, 16, 32):
        f = jnp.where(c >= d, pltpu.roll(gt, d, 1), 0.0)
        b = jnp.where(c < DN_CHUNK - d, pltpu.roll(gt, ts - d, 1), 0.0)
        gt = gt + jnp.where(bwd_t, b, f)
    gt_ref[0] = gt


def _dn_prep(dn_raw, ba, bat, conv_w, a_log, dt_bias, *, ts=256):
    B, S, C = dn_raw.shape
    ts = min(ts, S)
    nt = S // ts
    n_ba = 4 * DN_HEADS
    hb = ts // SUBLANES
    zeros = jnp.zeros((2 * DN_HEADS,), F32)
    al = jnp.concatenate([zeros, a_log.reshape(-1).astype(F32)])
    db = jnp.concatenate([zeros, dt_bias.reshape(-1).astype(F32)])
    ab = jnp.pad(jnp.stack([al, db]), ((0, 0), (0, BA_PAD - n_ba)))
    abt = jnp.stack([al, db], axis=1)
    row = lambda n: pl.BlockSpec((1, ts, n), lambda b, i: (b, i, 0))
    return pl.pallas_call(
        functools.partial(_dnprep_body, ts=ts, n_tiles=nt),
        grid=(B, nt),
        in_specs=[row(C),
                  pl.BlockSpec((1, SUBLANES, C), lambda b, i: (b, jnp.maximum(i * hb - 1, 0), 0)),
                  pl.BlockSpec((1, SUBLANES, C),
                               lambda b, i: (b, jnp.minimum((i + 1) * hb, S // SUBLANES - 1), 0)),
                  _resident((CONV_W, C)), row(BA_PAD),
                  pl.BlockSpec((1, n_ba, ts), lambda b, i: (b, 0, i)),
                  _resident((2, BA_PAD)), _resident((n_ba, 2))],
        out_specs=[row(DN_WIDTH), row(DN_WIDTH), row(DN_WIDTH), row(BA_PAD),
                   pl.BlockSpec((1, n_ba, ts), lambda b, i: (b, 0, i))],
        out_shape=[jax.ShapeDtypeStruct((B, S, DN_WIDTH), F32)] * 3
        + [jax.ShapeDtypeStruct((B, S, BA_PAD), F32), jax.ShapeDtypeStruct((B, n_ba, S), F32)],
        scratch_shapes=[pltpu.VMEM((ts + 2 * SUBLANES, C), F32)],
        compiler_params=pltpu.CompilerParams(dimension_semantics=("parallel", "parallel")),
        name="dn_prep",
    )(dn_raw, dn_raw, dn_raw, conv_w.astype(F32), ba, bat, ab, abt)


def _bd_rows(y, left):
    return jnp.concatenate([jnp.where(left, y, 0.0), jnp.where(left, 0.0, y)], axis=0).astype(BF16)


def _dnscan_body(qf_ref, kf_ref, vf_ref, bgf_ref, gtf_ref, qb_ref, kb_ref, vb_ref, bgb_ref, gtb_ref,
                 of_ref, ob_ref, st_ref, *, G):
    C = DN_CHUNK
    H = DN_HEADS
    P = H // 2

    @pl.when(pl.program_id(1) == 0)
    def _():
        st_ref[...] = jnp.zeros(st_ref.shape, F32)

    lane1 = lax.broadcasted_iota(jnp.int32, (C, LANES), 1)
    row1 = lax.broadcasted_iota(jnp.int32, (C, LANES), 0)
    left1 = lane1 < C
    s_idx = lane1 % C
    eye = (row1 == s_idx).astype(F32)
    lane2 = lax.broadcasted_iota(jnp.int32, (C, 2 * LANES), 1)
    left2 = (lane2 % LANES) < C
    rr = lax.broadcasted_iota(jnp.int32, (LANES, LANES), 0)
    cc = lax.broadcasted_iota(jnp.int32, (LANES, LANES), 1)
    top = rr < C
    bdmask = (rr < C) == (cc < C)
    top_c = lax.broadcasted_iota(jnp.int32, (LANES, C), 0) < C

    dirs = ((0, qf_ref, kf_ref, vf_ref, bgf_ref, gtf_ref, of_ref),
            (1, qb_ref, kb_ref, vb_ref, bgb_ref, gtb_ref, ob_ref))
    kT_cache = {}
    by_key = {}

    def prepare(group):
        chains = []
        for d, p, c in group:
            _, q_ref, k_ref, v_ref, bg_ref, gt_ref, _ = dirs[d]
            tri = (row1 >= s_idx) if d == 0 else (row1 <= s_idx)
            strict = (row1 > s_idx) if d == 0 else (row1 < s_idx)
            last = C - 1 if d == 0 else 0
            sl = slice(p * LANES, (p + 1) * LANES)
            cb = d * H + 2 * p
            cg = 2 * H + d * H + 2 * p
            if (d, p) not in kT_cache:
                kT_cache[(d, p)] = k_ref[0, :, sl].T
            rs = slice(c * C, (c + 1) * C)
            qc, kc, vc = q_ref[0, rs, sl], k_ref[0, rs, sl], v_ref[0, rs, sl]
            bexp = jnp.where(left1, bg_ref[0, rs, cb:cb + 1], bg_ref[0, rs, cb + 1:cb + 2])
            gcc = jnp.where(left1, bg_ref[0, rs, cg:cg + 1], bg_ref[0, rs, cg + 1:cg + 2])
            g0 = gt_ref[0, cg:cg + 1, rs]
            g1 = gt_ref[0, cg + 1:cg + 2, rs]
            grow = jnp.concatenate([g0, g1], axis=1)
            glast = gcc[last:last + 1, :]
            gl0 = glast[:, 0:1]
            gl1 = glast[:, C:C + 1]
            eg = jnp.exp(gcc)
            kb_ = kc * bexp
            fac = jnp.exp(jnp.minimum(jnp.where(top_c, gl0 - g0, gl1 - g1), 0.0))
            chains.append(dict(
                key=(d, p, c),
                rhs=jnp.concatenate([vc * bexp, kb_ * eg], axis=1),
                qg=(qc * eg).astype(BF16),
                kq=jnp.concatenate([kb_, qc], axis=0).astype(BF16),
                Z=_bd_rows(kc, left1),
                dec=jnp.where(tri, jnp.exp(jnp.minimum(gcc - grow, 0.0)), 0.0),
                strict=strict,
                kg=(kT_cache[(d, p)][:, rs] * fac).astype(BF16),
                egl=jnp.exp(glast)))
        yield
        for ch in chains:
            kkqk = _dot_nt(ch["kq"], ch["Z"])
            ch["N"] = jnp.where(ch["strict"], -kkqk[:C] * ch["dec"], 0.0)
            ch["qk"] = (kkqk[C:] * ch["dec"]).astype(BF16)
        for ch in chains:
            ch["Pm"] = eye + ch["N"]
        for j in range(6):
            yield
            for ch in chains:
                nb = _bd_rows(ch["N"], left1)
                if j == 0:
                    ch["N"] = _dot(ch["N"].astype(BF16), nb)
                elif j < 5:
                    both = _dot(jnp.concatenate([ch["Pm"], ch["N"]], axis=0).astype(BF16), nb)
                    ch["Pm"] = ch["Pm"] + both[:C]
                    ch["N"] = both[C:]
                else:
                    ch["Pm"] = ch["Pm"] + _dot(ch["Pm"].astype(BF16), nb)
        yield
        for ch in chains:
            X = _dot(ch["Pm"].astype(BF16), _bd_rows(ch["rhs"], left2))
            by_key[ch["key"]] = dict(
                u=X[:, :LANES], qk=ch["qk"], kg=ch["kg"], egl=ch["egl"],
                wq=jnp.concatenate([X[:, LANES:].astype(BF16), ch["qg"]], axis=0))

    def scan_step(step):
        live = []
        for d, *_refs, o_ref in dirs:
            c = step if d == 0 else G - 1 - step
            for p in range(P):
                live.append((d * P + p, o_ref, slice(c * C, (c + 1) * C), p, by_key[(d, p, c)]))
        S = [st_ref[sidx] for sidx, *_ in live]
        t1 = [_dot(ch["wq"], S[i].astype(BF16)) for i, (*_, ch) in enumerate(live)]
        vn = [ch["u"] - t1[i][:C] for i, (*_, ch) in enumerate(live)]
        for i, (sidx, o_ref, rs, p, ch) in enumerate(live):
            o_ref[0, rs, p * LANES:(p + 1) * LANES] = t1[i][C:] + _dot(ch["qk"], _bd_rows(vn[i], left1))
        for i, (sidx, o_ref, rs, p, ch) in enumerate(live):
            upd = _dot(ch["kg"], vn[i].astype(BF16))
            decay = jnp.where(top, ch["egl"][:, 0:1], ch["egl"][:, C:C + 1])
            st_ref[sidx] = S[i] * decay + jnp.where(bdmask, upd, 0.0)

    bounds = sorted({0, G} | {G - (G * i) // DN_SCAN_GROUPS for i in range(1, DN_SCAN_GROUPS)})
    groups = [range(a, b) for a, b in zip(bounds[:-1], bounds[1:])]
    pending = iter(())
    for steps in groups:
        needed = [(0, p, c) for p in range(P) for c in steps]
        needed += [(1, p, G - 1 - c) for p in range(P) for c in steps]
        for _ in prepare(needed):
            step = next(pending, None)
            if step is not None:
                scan_step(step)
        for step in pending:
            scan_step(step)
        pending = iter(steps)
    for step in pending:
        scan_step(step)


def _dn_scan(q, k, v, bg, gt, *, G=16):
    B, S, _ = q.shape
    G = min(G, S // DN_CHUNK)
    R = G * DN_CHUNK
    nb = S // R
    n_ba = 4 * DN_HEADS
    P = DN_HEADS // 2
    fw = lambda n: pl.BlockSpec((1, R, n), lambda b, i: (b, i, 0))
    bw = lambda n: pl.BlockSpec((1, R, n), lambda b, i: (b, nb - 1 - i, 0))
    fwt = pl.BlockSpec((1, n_ba, R), lambda b, i: (b, 0, i))
    bwt = pl.BlockSpec((1, n_ba, R), lambda b, i: (b, 0, nb - 1 - i))
    return pl.pallas_call(
        functools.partial(_dnscan_body, G=G),
        grid=(B, nb),
        in_specs=[fw(DN_WIDTH), fw(DN_WIDTH), fw(DN_WIDTH), fw(BA_PAD), fwt,
                  bw(DN_WIDTH), bw(DN_WIDTH), bw(DN_WIDTH), bw(BA_PAD), bwt],
        out_specs=[fw(DN_WIDTH), bw(DN_WIDTH)],
        out_shape=[jax.ShapeDtypeStruct((B, S, DN_WIDTH), F32)] * 2,
        scratch_shapes=[pltpu.VMEM((2 * P, LANES, LANES), F32)],
        compiler_params=pltpu.CompilerParams(dimension_semantics=("parallel", "arbitrary")),
        name="dn_scan",
    )(q, k, v, bg, gt, q, k, v, bg, gt)


def _mixers(x, ln_mix, w_mix_in, conv_w, sink, diff_lam, diff_g, a_log, dt_bias, lam_init):
    B, S, _ = x.shape
    wq, wk, wv, dq, dkt, dv, dn_raw, z, ba, bat = _inproj(x, ln_mix, w_mix_in)
    win_slopes = 2.0 ** (-8.0 * jnp.arange(1, WIN_Q_HEADS + 1, dtype=F32) / WIN_Q_HEADS)
    o_win = _win_attention(wq, wk, wv, sink, win_slopes)
    o_diff = _diff_attention(dq, dkt, dv, diff_lam, diff_g, lam_init)
    q, k, v, bg, gt = _dn_prep(dn_raw, ba, bat, conv_w, a_log, dt_bias)
    o_f, o_b = _dn_scan(q, k, v, bg, gt)
    return tuple(t.reshape(B * S, t.shape[-1]) for t in (o_win, o_diff, o_f, o_b, z))


def kernel(x, ln_ffn1, ffn1_w_in, ffn1_w_out, ln_mix, w_mix_in, conv_w, sink_logits, diff_lambda,
           diff_norm_g, dn_A_log, dn_dt_bias, dn_norm_g, w_mix_out, ln_ffn2, ffn2_w_in, ffn2_w_out,
           ln_final):
    B, S, D = x.shape
    depth = ln_ffn1.shape[0]
    for l in range(depth):
        lam_init = 0.8 - 0.6 * math.exp(-0.3 * l)
        x = _ffn(x.reshape(B * S, D), ln_ffn1[l], ffn1_w_in[l], ffn1_w_out[l]).reshape(B, S, D)
        mixed = _mixers(x, ln_mix[l], w_mix_in[l], conv_w[l], sink_logits[l], diff_lambda[l], diff_norm_g[l],
                        dn_A_log[l], dn_dt_bias[l], lam_init)
        g_final = ln_final if l == depth - 1 else None
        x = _ffn(x.reshape(B * S, D), ln_ffn2[l], ffn2_w_in[l], ffn2_w_out[l], g_final,
                 mix=mixed + (dn_norm_g[l], w_mix_out[l])).reshape(B, S, D)
    return x
```

```python
import functools
import math

import jax
import jax.numpy as jnp
from jax import lax
from jax.experimental import pallas as pl
from jax.experimental.pallas import tpu as pltpu

F32 = jnp.float32
BF16 = jnp.bfloat16

D_MODEL = 1024
HEAD_DIM = 64
EPS = 1e-6
WIN_Q_HEADS = 6
WIN_KV_HEADS = 2
WINDOW = 128
DIFF_HEADS = 4
DIFF_QK_DIM = HEAD_DIM // 2
DN_HEADS = 6
DN_CHUNK = 64
CONV_W = 5
D_FF = 2752
WIN_WIDTH = WIN_Q_HEADS * HEAD_DIM
WIN_KV_WIDTH = WIN_KV_HEADS * HEAD_DIM
DIFF_WIDTH = DIFF_HEADS * HEAD_DIM
DN_WIDTH = DN_HEADS * HEAD_DIM
DN_QKV = 3 * DN_WIDTH
MIX_WIDTH = WIN_WIDTH + DIFF_WIDTH + DN_WIDTH
OFF_WQ = 0
OFF_WK = OFF_WQ + WIN_WIDTH
OFF_WV = OFF_WK + WIN_KV_WIDTH
OFF_DQ = OFF_WV + WIN_KV_WIDTH
OFF_DK = OFF_DQ + DIFF_WIDTH
OFF_DV = OFF_DK + DIFF_WIDTH
OFF_DN = OFF_DV + DIFF_WIDTH
OFF_Z = OFF_DN + DN_QKV
OFF_BA = OFF_Z + DN_WIDTH
MIX_IN = OFF_BA + 4 * DN_HEADS

LANES = 128
SUBLANES = 8
BF16_ROWS = 2 * SUBLANES
FF_CHUNK = 256
D_FF_PAD = -(-D_FF // FF_CHUNK) * FF_CHUNK
VMEM_LIMIT = 56 * 1024 * 1024
LOG2E = math.log2(math.e)
DIFF_TK = 512
DIFF_BIAS_ROWS = 3
DIFF_KROWS = -(-(DIFF_QK_DIM + DIFF_BIAS_ROWS) // BF16_ROWS) * BF16_ROWS
DIFF_KT_ROWS = 2 * DIFF_HEADS * DIFF_KROWS
DIFF_V_WIDTH = DIFF_HEADS * LANES
DIFF_SKIP_LOG2 = 150.0
DIFF_BOUND_SLACK = 1.0 + 2.0 ** -6
MASKED = -1e30
DIFF_SLOPES = tuple(2.0 ** (-8.0 * (i + 1) / DIFF_HEADS) for i in range(DIFF_HEADS))
DN_SCAN_GROUPS = 5


def _rms(x, g):
    return x * lax.rsqrt(jnp.mean(x * x, axis=-1, keepdims=True) + EPS) * g


def _silu(x):
    h = 0.5 * x
    return h + h * jnp.tanh(h)


def _seg_sum64(x):
    R, width = x.shape
    n = width // LANES
    rows = jnp.concatenate([x[:, c * LANES:(c + 1) * LANES] for c in range(n)], axis=0)
    r = lax.broadcasted_iota(jnp.int32, (LANES, LANES), 0) < HEAD_DIM
    c = lax.broadcasted_iota(jnp.int32, (LANES, LANES), 1) < HEAD_DIM
    ones_bd = (r == c).astype(F32).astype(BF16)
    hi = rows.astype(BF16)
    r1 = rows - hi.astype(F32)
    mid = r1.astype(BF16)
    lo = (r1 - mid.astype(F32)).astype(BF16)
    ss = _dot(hi, ones_bd) + _dot(mid, ones_bd) + _dot(lo, ones_bd)
    return jnp.concatenate([ss[c * R:(c + 1) * R] for c in range(n)], axis=1)


def _dot(a, b):
    return jnp.dot(a, b, preferred_element_type=F32)


def _dot_nt(a, b):
    return lax.dot_general(a, b, (((1,), (1,)), ((), ())), preferred_element_type=F32)


def _resident(shape):
    nd = len(shape)
    return pl.BlockSpec(shape, lambda *_: (0,) * nd, pipeline_mode=pl.Buffered(1))


def _ffn_body(*refs, n_chunks, final, mixed):
    refs = iter(refs)
    x_ref = next(refs)
    mix_refs = [next(refs) for _ in range(7)] if mixed else None
    g_ref, wg_ref, wu_ref, wo_ref = (next(refs) for _ in range(4))
    gf_ref = next(refs) if final else None
    o_ref, acc_ref = next(refs), next(refs)
    x = x_ref[...]
    if mixed:
        ow_ref, od_ref, of_ref, ob_ref, z_ref, dng_ref, wmix_ref = mix_refs
        cat_ref = next(refs)
        oc = of_ref[...] + ob_ref[...]
        on = oc * lax.rsqrt(_seg_sum64(oc * oc) * (1.0 / HEAD_DIM) + EPS) * dng_ref[...] * _silu(z_ref[...])
        cat_ref[:, :WIN_WIDTH] = ow_ref[...]
        cat_ref[:, WIN_WIDTH:WIN_WIDTH + DIFF_WIDTH] = od_ref[...]
        cat_ref[:, WIN_WIDTH + DIFF_WIDTH:] = on.astype(BF16)
        x = x + _dot(cat_ref[...], wmix_ref[...])
    h = _rms(x, g_ref[...]).astype(BF16)
    for c in range(n_chunks):
        cols = slice(c * FF_CHUNK, (c + 1) * FF_CHUNK)
        a = (_silu(_dot(h, wg_ref[:, cols])) * _dot(h, wu_ref[:, cols])).astype(BF16)
        part = _dot(a, wo_ref[cols, :])
        if c == 0:
            acc_ref[...] = part
        else:
            acc_ref[...] += part
    y = x + 0.5 * acc_ref[...]
    if final:
        y = _rms(y, gf_ref[...])
    o_ref[...] = y


def _prep_ffn_weights(w_in, w_out):
    pad = D_FF_PAD - D_FF
    wg = jnp.pad(w_in[:, :D_FF].astype(BF16), ((0, 0), (0, pad)))
    wu = jnp.pad(w_in[:, D_FF:].astype(BF16), ((0, 0), (0, pad)))
    wo = jnp.pad(w_out.astype(BF16), ((0, pad), (0, 0)))
    return wg, wu, wo


def _ffn(x2d, g, w_in, w_out, g_final=None, mix=None, *, tm=512):
    T = x2d.shape[0]
    tm = min(tm, T)
    assert T % tm == 0
    n_chunks = D_FF_PAD // FF_CHUNK
    wg, wu, wo = _prep_ffn_weights(w_in, w_out)
    final = g_final is not None
    mixed = mix is not None
    row = lambda n: pl.BlockSpec((tm, n), lambda i: (i, 0))
    in_specs = [row(D_MODEL)]
    args = [x2d]
    scratch = [pltpu.VMEM((tm, D_MODEL), F32)]
    if mixed:
        o_win, o_diff, o_f, o_b, z, dn_g, w_mix_out = mix
        in_specs += [row(WIN_WIDTH), row(DIFF_WIDTH), row(DN_WIDTH), row(DN_WIDTH), row(DN_WIDTH),
                     _resident((1, DN_WIDTH)), _resident((MIX_WIDTH, D_MODEL))]
        args += [o_win, o_diff, o_f, o_b, z, jnp.tile(dn_g.astype(F32), DN_HEADS).reshape(1, DN_WIDTH),
                 w_mix_out.astype(BF16)]
        scratch.append(pltpu.VMEM((tm, MIX_WIDTH), BF16))
    in_specs += [_resident((1, D_MODEL)), _resident(wg.shape), _resident(wu.shape), _resident(wo.shape)]
    args += [g.reshape(1, D_MODEL), wg, wu, wo]
    if final:
        in_specs.append(_resident((1, D_MODEL)))
        args.append(g_final.reshape(1, D_MODEL))
    return pl.pallas_call(
        functools.partial(_ffn_body, n_chunks=n_chunks, final=final, mixed=mixed),
        grid=(T // tm,),
        in_specs=in_specs,
        out_specs=row(D_MODEL),
        out_shape=jax.ShapeDtypeStruct((T, D_MODEL), F32),
        scratch_shapes=scratch,
        compiler_params=pltpu.CompilerParams(
            dimension_semantics=("parallel",), vmem_limit_bytes=VMEM_LIMIT),
        name="ffn" + ("_mix" if mixed else "") + ("_final" if final else ""),
    )(*args)


BA_PAD = LANES


def _inproj_body(x_ref, g_ref, w_ref, wt_ref, wdv_ref,
                 wq_ref, wk_ref, wv_ref, dq_ref, dkt_ref, dv_ref, dn_ref, z_ref, ba_ref, bat_ref,
                 qn_ref, kn_ref):
    h = _rms(x_ref[0], g_ref[...]).astype(BF16)
    wq_ref[0] = (_dot(h, w_ref[:, OFF_WQ:OFF_WK]) * (HEAD_DIM ** -0.5 * LOG2E)).astype(BF16)
    wk_ref[0] = _dot(h, w_ref[:, OFF_WK:OFF_WV]).astype(BF16)
    wv_ref[0] = _dot(h, w_ref[:, OFF_WV:OFF_DQ]).astype(BF16)
    dq = (_dot(h, w_ref[:, OFF_DQ:OFF_DK]) * (DIFF_QK_DIM ** -0.5 * LOG2E)).astype(BF16)
    dq_ref[0] = dq
    n_tiles = dq.shape[0] // DIFF_TK
    dqf = dq.astype(F32)
    group = (lax.broadcasted_iota(jnp.int32, (DIFF_WIDTH, LANES), 0) // DIFF_QK_DIM
             == lax.broadcasted_iota(jnp.int32, (DIFF_WIDTH, LANES), 1))
    q_sq = _dot((dqf * dqf).astype(BF16), group.astype(F32).astype(BF16))
    for a in range(n_tiles):
        qn_ref[0, a] = jnp.broadcast_to(
            jnp.max(q_sq[a * DIFF_TK:(a + 1) * DIFF_TK], axis=0, keepdims=True), (SUBLANES, LANES))
    wt_out = lax.dot_general(wt_ref[...], h, (((0,), (1,)), ((), ())),
                             preferred_element_type=F32)
    kt = wt_out[:DIFF_WIDTH]
    tm = kt.shape[1]
    extra_rows = DIFF_KROWS - DIFF_QK_DIM
    pos = pl.program_id(1) * tm + lax.broadcasted_iota(jnp.int32, (extra_rows, tm), 1)
    kr = (pos % DIFF_TK).astype(F32)
    rowi = lax.broadcasted_iota(jnp.int32, (extra_rows, tm), 0)
    k_sq = []
    for hd in range(DIFF_HEADS):
        bias = (DIFF_SLOPES[hd] * LOG2E) * kr
        hi = bias.astype(BF16).astype(F32)
        mid = (bias - hi).astype(BF16).astype(F32)
        lo = bias - hi - mid
        extra = jnp.where(rowi == 0, hi, jnp.where(rowi == 1, mid, jnp.where(rowi == 2, lo, 0.0))).astype(BF16)
        for mp in range(2):
            hm = 2 * hd + mp
            kb = kt[hm * DIFF_QK_DIM:(hm + 1) * DIFF_QK_DIM, :].astype(BF16)
            dkt_ref[0, hm * DIFF_KROWS:hm * DIFF_KROWS + DIFF_QK_DIM, :] = kb
            dkt_ref[0, hm * DIFF_KROWS + DIFF_QK_DIM:(hm + 1) * DIFF_KROWS, :] = extra
            kbf = kb.astype(F32)
            k_sq.append(jnp.sum(kbf * kbf, axis=0, keepdims=True))
    k_sq = jnp.concatenate(k_sq, axis=0)
    for a in range(n_tiles):
        kn_ref[0, a] = jnp.broadcast_to(
            jnp.max(k_sq[:, a * DIFF_TK:(a + 1) * DIFF_TK], axis=1, keepdims=True), (SUBLANES, LANES))
    dv = _dot(h, wdv_ref[...])
    ones_half = (lax.broadcasted_iota(jnp.int32, dv.shape, 1) % LANES) >= HEAD_DIM
    dv_ref[0] = jnp.where(ones_half, 1.0, dv).astype(BF16)
    dn_ref[0] = _dot(h, w_ref[:, OFF_DN:OFF_Z])
    z_ref[0] = _dot(h, w_ref[:, OFF_Z:OFF_BA])
    ba_ref[0] = _dot(h, w_ref[:, OFF_BA:OFF_BA + BA_PAD])
    bat_ref[0] = wt_out[DIFF_WIDTH:DIFF_WIDTH + bat_ref.shape[1]]


def _inproj(x, g, w_in, *, tm=1024):
    B, S, _ = x.shape
    tm = min(tm, S)
    assert S % tm == 0 and tm % DIFF_TK == 0
    n_ba = 4 * DN_HEADS
    w_in = w_in.astype(BF16)
    w = jnp.pad(w_in, ((0, 0), (0, OFF_BA + BA_PAD - MIX_IN)))
    wt = jnp.pad(jnp.concatenate([w_in[:, OFF_DK:OFF_DV], w_in[:, OFF_BA:MIX_IN]], axis=1),
                 ((0, 0), (0, 2 * BF16_ROWS - n_ba % (2 * BF16_ROWS))))
    wdv = jnp.pad(w_in[:, OFF_DV:OFF_DN].reshape(D_MODEL, DIFF_HEADS, HEAD_DIM),
                  ((0, 0), (0, 0), (0, LANES - HEAD_DIM))).reshape(D_MODEL, DIFF_V_WIDTH)
    row = lambda n: pl.BlockSpec((1, tm, n), lambda b, i: (b, i, 0))
    col = lambda n: pl.BlockSpec((1, n, tm), lambda b, i: (b, 0, i))
    out_shapes = [
        jax.ShapeDtypeStruct((B, S, WIN_WIDTH), BF16),
        jax.ShapeDtypeStruct((B, S, WIN_KV_WIDTH), BF16),
        jax.ShapeDtypeStruct((B, S, WIN_KV_WIDTH), BF16),
        jax.ShapeDtypeStruct((B, S, DIFF_WIDTH), BF16),
        jax.ShapeDtypeStruct((B, DIFF_KT_ROWS, S), BF16),
        jax.ShapeDtypeStruct((B, S, DIFF_V_WIDTH), BF16),
        jax.ShapeDtypeStruct((B, S, DN_QKV), F32),
        jax.ShapeDtypeStruct((B, S, DN_WIDTH), F32),
        jax.ShapeDtypeStruct((B, S, BA_PAD), F32),
        jax.ShapeDtypeStruct((B, n_ba, S), F32),
        jax.ShapeDtypeStruct((B, S // DIFF_TK, SUBLANES, LANES), F32),
        jax.ShapeDtypeStruct((B, S // DIFF_TK, SUBLANES, LANES), F32),
    ]
    norms = pl.BlockSpec((1, tm // DIFF_TK, SUBLANES, LANES), lambda b, i: (b, i, 0, 0))
    out_specs = [row(WIN_WIDTH), row(WIN_KV_WIDTH), row(WIN_KV_WIDTH), row(DIFF_WIDTH),
                 col(DIFF_KT_ROWS), row(DIFF_V_WIDTH), row(DN_QKV), row(DN_WIDTH), row(BA_PAD), col(n_ba),
                 norms, norms]
    return pl.pallas_call(
        _inproj_body,
        grid=(B, S // tm),
        in_specs=[row(D_MODEL), _resident((1, D_MODEL)), _resident(w.shape),
                  _resident(wt.shape), _resident(wdv.shape)],
        out_specs=out_specs,
        out_shape=out_shapes,
        compiler_params=pltpu.CompilerParams(
            dimension_semantics=("parallel", "parallel"), vmem_limit_bytes=VMEM_LIMIT),
        name="inproj",
    )(x, g.reshape(1, D_MODEL), w, wt, wdv)


def _win_body(q_ref, kp_ref, kc_ref, kn_ref, vp_ref, vc_ref, vn_ref, sink_ref, slope_ref, o_ref,
              *, n_steps, nq):
    i = pl.program_id(1)
    W = WINDOW
    KW = 3 * W
    kw = jnp.concatenate([kp_ref[0], kc_ref[0], kn_ref[0]], axis=0).astype(F32)
    vw = jnp.concatenate([vp_ref[0], vc_ref[0], vn_ref[0]], axis=0).astype(F32)
    kw_sw = pltpu.roll(kw, HEAD_DIM, 1)
    vw_sw = pltpu.roll(vw, HEAD_DIM, 1)
    left = lax.broadcasted_iota(jnp.int32, (KW, LANES), 1) < HEAD_DIM
    qi = lax.broadcasted_iota(jnp.int32, (W, KW), 0)
    ki = lax.broadcasted_iota(jnp.int32, (W, KW), 1)
    dist = jnp.abs(ki - W - qi)
    band_bias = [jnp.where(dist <= W, (-LOG2E) * slope_ref[:, hq:hq + 1] * dist.astype(F32), MASKED)
                 for hq in range(WIN_Q_HEADS)]
    edge_lo = jnp.where(jnp.logical_or(ki >= W, i > 0), 0.0, MASKED)
    edge_hi = jnp.where(jnp.logical_or(ki < 2 * W, i < n_steps - 1), 0.0, MASKED)
    n_pairs = WIN_Q_HEADS // 2
    ones_stack = jnp.concatenate([jnp.where(left, 1.0, 0.0), jnp.where(left, 0.0, 1.0)], axis=0).astype(BF16)
    blocks = []
    for a in range(nq):
        rows = slice(a * W, a * W + KW)

        def stack(first, second, rows=rows):
            return jnp.concatenate([jnp.where(left, first[rows], 0.0),
                                    jnp.where(left, 0.0, second[rows])], axis=0).astype(BF16)

        zk = (stack(kw, kw_sw), stack(kw, kw), stack(kw_sw, kw))
        zv = tuple(jnp.concatenate([z, ones_stack], axis=1)
                   for z in (stack(vw, vw_sw), stack(vw, vw), stack(vw_sw, vw)))
        edge = None
        if a == 0:
            edge = edge_lo
        if a == nq - 1:
            edge = edge_hi if edge is None else edge + edge_hi
        q = q_ref[0, a * W:(a + 1) * W, :]
        s2 = [_dot_nt(q[:, p * LANES:(p + 1) * LANES], zk[p]) for p in range(n_pairs)]
        blocks.append((a, edge, zv, s2))
    weights, sink_terms = {}, {}
    for a, edge, zv, s2 in blocks:
        for hq in range(WIN_Q_HEADS):
            p, t = divmod(hq, 2)
            s = s2[p][:, t * KW:(t + 1) * KW] + band_bias[hq]
            if edge is not None:
                s = s + edge
            sink = LOG2E * sink_ref[:, hq:hq + 1]
            m = jnp.maximum(jnp.max(s, axis=-1, keepdims=True), sink)
            weights[(a, hq)] = jnp.exp2(s - m).astype(BF16)
            sink_terms[(a, hq)] = jnp.exp2(sink - m)
    left_w = lax.broadcasted_iota(jnp.int32, (W, LANES), 1) < HEAD_DIM
    for a, edge, zv, s2 in blocks:
        for p in range(n_pairs):
            ee = jnp.concatenate([weights[(a, 2 * p)], weights[(a, 2 * p + 1)]], axis=1)
            acc = _dot(ee, zv[p])
            denom = acc[:, LANES:] + jnp.where(left_w, sink_terms[(a, 2 * p)], sink_terms[(a, 2 * p + 1)])
            o_ref[0, a * W:(a + 1) * W, p * LANES:(p + 1) * LANES] = (acc[:, :LANES] / denom).astype(BF16)


def _win_attention(q, k, v, sink, slopes, *, nq=8):
    B, S, _ = q.shape
    W = WINDOW
    nq = min(nq, S // W)
    assert S % (nq * W) == 0
    ns = S // (nq * W)
    nb = S // W
    qspec = pl.BlockSpec((1, nq * W, WIN_WIDTH), lambda b, i: (b, i, 0))
    prev = pl.BlockSpec((1, W, WIN_KV_WIDTH), lambda b, i: (b, jnp.maximum(i * nq - 1, 0), 0))
    cur = pl.BlockSpec((1, nq * W, WIN_KV_WIDTH), lambda b, i: (b, i, 0))
    nxt = pl.BlockSpec((1, W, WIN_KV_WIDTH), lambda b, i: (b, jnp.minimum((i + 1) * nq, nb - 1), 0))
    return pl.pallas_call(
        functools.partial(_win_body, n_steps=ns, nq=nq),
        grid=(B, ns),
        in_specs=[qspec, prev, cur, nxt, prev, cur, nxt,
                  _resident((1, WIN_Q_HEADS)), _resident((1, WIN_Q_HEADS))],
        out_specs=pl.BlockSpec((1, nq * W, WIN_WIDTH), lambda b, i: (b, i, 0)),
        out_shape=jax.ShapeDtypeStruct((B, S, WIN_WIDTH), BF16),
        compiler_params=pltpu.CompilerParams(dimension_semantics=("parallel", "parallel")),
        name="win_attn",
    )(q, k, k, k, v, v, v, sink.reshape(1, WIN_Q_HEADS).astype(F32), slopes.reshape(1, WIN_Q_HEADS))


def _diff_ranges(q_sq, k_sq, tq, tk):
    assert tq == tk == DIFF_TK
    n_maps = 2 * DIFF_HEADS
    B, nq = q_sq.shape[:2]
    nk = nq
    qn = jnp.sqrt(q_sq[:, :, 0, :n_maps])
    kn = jnp.swapaxes(jnp.sqrt(k_sq[:, :, :n_maps, 0]), 1, 2)
    kown = kn
    reach = jnp.swapaxes(qn, 1, 2)[..., None] * (kn[:, :, None, :] + kown[..., None]) * DIFF_BOUND_SLACK
    reach = jnp.max(reach.reshape(B, DIFF_HEADS, 2, nq, nk), axis=2)
    q0 = jnp.arange(nq, dtype=jnp.int32) * tq
    k0 = jnp.arange(nk, dtype=jnp.int32) * tk
    dmin = jnp.maximum(jnp.maximum(k0[None, :] - (q0[:, None] + tq - 1), q0[:, None] - (k0[None, :] + tk - 1)), 0)
    c = jnp.asarray(DIFF_SLOPES, F32) * LOG2E
    need = reach - c[None, :, None, None] * dmin.astype(F32)[None, None] > -DIFF_SKIP_LOG2
    idx = jnp.arange(nk, dtype=jnp.int32)
    need = jnp.logical_or(need, (idx[None, :] == (q0 // tk)[:, None])[None, None])
    lo = jnp.min(jnp.where(need, idx, nk), axis=-1)
    hi = jnp.max(jnp.where(need, idx, -1), axis=-1)
    return jnp.stack([lo, hi], axis=-1).transpose(0, 2, 1, 3).reshape(-1).astype(jnp.int32)


def _diff_body(rng_ref, q_ref, kt_ref, v_ref, lam_ref, g_ref, o_ref, qv_ref, m_ref, acc_ref,
               s0_ref, st0_ref, s1_ref, st1_ref, *, tq, tk, lam_init):
    n_maps = 2 * DIFF_HEADS
    n_groups = tk // LANES
    q0 = pl.program_id(1) * tq
    jd = q0 // tk
    q = q_ref[0]
    lane_x = lax.broadcasted_iota(jnp.int32, (tq, DIFF_KROWS - DIFF_QK_DIM), 1)
    for var, sign in enumerate((1.0, -1.0, 0.0)):
        extras = jnp.where(lane_x < DIFF_BIAS_ROWS, -sign, 0.0).astype(BF16)
        for hm in range(n_maps):
            qv_ref[var, hm, :, 0:DIFF_QK_DIM] = q[:, hm * DIFF_QK_DIM:(hm + 1) * DIFF_QK_DIM]
            qv_ref[var, hm, :, DIFF_QK_DIM:DIFF_KROWS] = extras
    m_ref[...] = jnp.full(m_ref.shape, -jnp.inf, F32)
    acc_ref[...] = jnp.zeros(acc_ref.shape, F32)
    qpos = (q0 + lax.broadcasted_iota(jnp.int32, (tq, LANES), 0)).astype(F32)
    bufs = ((s0_ref, st0_ref), (s1_ref, st1_ref))
    rng_base = (pl.program_id(0) * pl.num_programs(1) + pl.program_id(1)) * (2 * DIFF_HEADS)

    los = [rng_ref[rng_base + 2 * h] for h in range(DIFF_HEADS)]
    n_offs = [rng_ref[rng_base + 2 * h + 1] - los[h] for h in range(DIFF_HEADS)]
    starts = [0]
    for h in range(DIFF_HEADS):
        starts.append(starts[-1] + n_offs[h])
    n_total = starts[DIFF_HEADS]

    def pick(h, vals):
        out = vals[-1]
        for i in range(len(vals) - 2, -1, -1):
            out = jnp.where(h == i, vals[i], out)
        return out

    def locate(u):
        h = sum((u >= starts[i]).astype(jnp.int32) for i in range(1, DIFF_HEADS))
        t = pick(h, los) + u - pick(h, starts[:DIFF_HEADS])
        return h, t + (t >= jd).astype(jnp.int32)

    def scores(h, j, buf, diag_dist=None):
        s_ref, st_ref = bufs[buf]
        if diag_dist is not None:
            var = 2
            bias = (DIFF_SLOPES[h] * LOG2E) * diag_dist
        else:
            c = pick(h, [sl * LOG2E for sl in DIFF_SLOPES])
            after = j > jd
            var = jnp.where(after, 0, 1)
            rowoff = jnp.where(after, c, -c) * ((j * tk).astype(F32) - qpos)
        k0 = pl.multiple_of(j * tk, tk)
        for mp in range(2):
            hm = 2 * h + mp
            r0 = hm * DIFF_KROWS if diag_dist is not None else pl.multiple_of(hm * DIFF_KROWS, BF16_ROWS)
            kt = kt_ref[0, pl.ds(r0, DIFF_KROWS), pl.ds(k0, tk)]
            s = _dot(qv_ref[var, hm], kt)
            if diag_dist is not None:
                s = s - bias
            s_ref[mp] = s
            mx = s[:, 0:LANES]
            for g in range(1, n_groups):
                mx = jnp.maximum(mx, s[:, g * LANES:(g + 1) * LANES])
            mrow = jnp.broadcast_to(jnp.max(mx, axis=-1, keepdims=True), (tq, LANES))
            m_old = m_ref[hm]
            if diag_dist is not None:
                m_new = jnp.maximum(m_old, mrow)
                shift = m_new
            else:
                m_new = jnp.maximum(m_old, mrow - rowoff)
                shift = m_new + rowoff
            st_ref[mp, 0] = shift
            st_ref[mp, 1] = jnp.exp2(m_old - m_new)
            m_ref[hm] = m_new

    def values(h, j, buf, static=False):
        s_ref, st_ref = bufs[buf]
        k0 = pl.multiple_of(j * tk, tk)
        c0 = h * LANES if static else pl.multiple_of(h * LANES, LANES)
        v2 = v_ref[0, pl.ds(k0, tk), pl.ds(c0, LANES)]
        for mp in range(2):
            hm = 2 * h + mp
            shift = st_ref[mp, 0]
            alpha = st_ref[mp, 1]
            e = jnp.exp2(s_ref[mp] - jnp.concatenate([shift] * n_groups, axis=1))
            acc_ref[hm] = alpha * acc_ref[hm] + _dot(e.astype(BF16), v2)

    kpos = (jd * tk + lax.broadcasted_iota(jnp.int32, (tq, tk), 1)).astype(F32)
    qp = (q0 + lax.broadcasted_iota(jnp.int32, (tq, tk), 0)).astype(F32)
    diag_dist = jnp.abs(qp - kpos)
    scores(0, jd, 0, diag_dist)
    for h in range(DIFF_HEADS):
        if h + 1 < DIFF_HEADS:
            scores(h + 1, jd, (h + 1) % 2, diag_dist)
        values(h, jd, h % 2, static=True)

    def s_at(u, buf):
        h, j = locate(u)
        scores(h, j, buf)

    def v_at(u, buf):
        h, j = locate(u)
        values(h, j, buf)

    @pl.when(n_total > 0)
    def _():
        s_at(0, 0)

    def pair(p, carry):
        u = 2 * p
        s_at(u + 1, 1)
        v_at(u, 0)
        s_at(u + 2, 0)
        v_at(u + 1, 1)
        return carry

    n_pairs = jnp.maximum(n_total - 1, 0) // 2
    lax.fori_loop(0, n_pairs, pair, 0)
    u_last = 2 * n_pairs
    left_over = n_total - u_last

    @pl.when(left_over == 2)
    def _():
        s_at(u_last + 1, 1)
        v_at(u_last, 0)
        v_at(u_last + 1, 1)

    @pl.when(left_over == 1)
    def _():
        v_at(u_last, 0)

    lp = lam_ref[...]
    lam = (jnp.exp(jnp.sum(lp[0:1] * lp[1:2], axis=-1, keepdims=True))
           - jnp.exp(jnp.sum(lp[2:3] * lp[3:4], axis=-1, keepdims=True)) + lam_init)
    lane = lax.broadcasted_iota(jnp.int32, (tq, LANES), 1)
    left = lane < HEAD_DIM
    ys = []
    for h in range(DIFF_HEADS):
        a1, a2 = acc_ref[2 * h], acc_ref[2 * h + 1]
        od = a1 / pltpu.roll(a1, HEAD_DIM, 1) - lam * (a2 / pltpu.roll(a2, HEAD_DIM, 1))
        ms = jnp.sum(jnp.where(left, od * od, 0.0), axis=-1, keepdims=True) * (1.0 / HEAD_DIM)
        ys.append(od * lax.rsqrt(ms + EPS) * g_ref[...] * (1.0 - lam_init))
    for p in range(DIFF_HEADS // 2):
        o_ref[0, :, p * LANES:(p + 1) * LANES] = jnp.where(
            left, ys[2 * p], pltpu.roll(ys[2 * p + 1], HEAD_DIM, 1)).astype(BF16)


def _diff_attention(q, kt, v, q_sq, k_sq, diff_lambda, diff_g, lam_init, *, tq=DIFF_TK):
    B, S, _ = q.shape
    tk = DIFF_TK
    tq = min(tq, S)
    assert S % tk == 0 and tk % tq == 0
    n_maps = 2 * DIFF_HEADS
    g2 = jnp.concatenate([diff_g, diff_g]).reshape(1, LANES).astype(F32)
    grid_spec = pltpu.PrefetchScalarGridSpec(
        num_scalar_prefetch=1,
        grid=(B, S // tq),
        in_specs=[pl.BlockSpec((1, tq, DIFF_WIDTH), lambda b, i, r: (b, i, 0)),
                  pl.BlockSpec((1, DIFF_KT_ROWS, S), lambda b, i, r: (b, 0, 0)),
                  pl.BlockSpec((1, S, DIFF_V_WIDTH), lambda b, i, r: (b, 0, 0)),
                  pl.BlockSpec((4, DIFF_QK_DIM), lambda b, i, r: (0, 0)),
                  pl.BlockSpec((1, LANES), lambda b, i, r: (0, 0))],
        out_specs=pl.BlockSpec((1, tq, DIFF_WIDTH), lambda b, i, r: (b, i, 0)),
        scratch_shapes=[pltpu.VMEM((3, n_maps, tq, DIFF_KROWS), BF16),
                        pltpu.VMEM((n_maps, tq, LANES), F32),
                        pltpu.VMEM((n_maps, tq, LANES), F32),
                        pltpu.VMEM((2, tq, tk), F32), pltpu.VMEM((2, 2, tq, LANES), F32),
                        pltpu.VMEM((2, tq, tk), F32), pltpu.VMEM((2, 2, tq, LANES), F32)])
    return pl.pallas_call(
        functools.partial(_diff_body, tq=tq, tk=tk, lam_init=lam_init),
        grid_spec=grid_spec,
        out_shape=jax.ShapeDtypeStruct((B, S, DIFF_WIDTH), BF16),
        compiler_params=pltpu.CompilerParams(
            dimension_semantics=("parallel", "arbitrary"), vmem_limit_bytes=VMEM_LIMIT),
        name="diff_attn",
    )(_diff_ranges(q_sq, k_sq, tq, tk), q, kt, v, diff_lambda.astype(F32), g2)


def _l2n64(x):
    return x * lax.rsqrt(_seg_sum64(x * x) + EPS)


def _dnprep_body(x_ref, xp_ref, xn_ref, w_ref, ba_ref, bat_ref, ab_ref, abt_ref,
                 q_ref, k_ref, v_ref, bg_ref, gt_ref, xx_ref, *, ts, n_tiles):
    i = pl.program_id(1)
    H = DN_HEADS
    halo = SUBLANES
    xx_ref[0:halo, :] = jnp.where(i > 0, xp_ref[0], 0.0)
    xx_ref[halo:halo + ts, :] = x_ref[0]
    xx_ref[halo + ts:2 * halo + ts, :] = jnp.where(i < n_tiles - 1, xn_ref[0], 0.0)
    half = CONV_W // 2
    y = w_ref[half:half + 1, :] * x_ref[0]
    for j in range(CONV_W):
        if j != half:
            y = y + w_ref[j:j + 1, :] * xx_ref[halo - half + j:halo - half + j + ts, :]
    y = _silu(y)
    q_ref[0] = _l2n64(y[:, :DN_WIDTH]) * (HEAD_DIM ** -0.5)
    k_ref[0] = _l2n64(y[:, DN_WIDTH:2 * DN_WIDTH])
    v_ref[0] = y[:, 2 * DN_WIDTH:]

    def gates(raw, a_log, dt_bias):
        beta = 1.0 / (1.0 + jnp.exp(-raw))
        z = raw + dt_bias
        softplus = jnp.maximum(z, 0.0) + jnp.log(1.0 + jnp.exp(-jnp.abs(z)))
        return beta, -jnp.exp(a_log) * softplus

    ba = ba_ref[0]
    beta, g = gates(ba, ab_ref[0:1, :], ab_ref[1:2, :])
    lane = lax.broadcasted_iota(jnp.int32, ba.shape, 1)
    r = lax.broadcasted_iota(jnp.int32, ba.shape, 0) % DN_CHUNK
    bwd = jnp.logical_and(lane >= 3 * H, lane < 4 * H)
    gc = g
    for d in (1, 2, 4, 8, 16, 32):
        f = jnp.where(r >= d, pltpu.roll(gc, d, 0), 0.0)
        b = jnp.where(r < DN_CHUNK - d, pltpu.roll(gc, ts - d, 0), 0.0)
        gc = gc + jnp.where(bwd, b, f)
    bg_ref[0] = jnp.where(lane < 2 * H, beta, gc)
    bat = bat_ref[0]
    _, gt = gates(bat, abt_ref[:, 0:1], abt_ref[:, 1:2])
    row = lax.broadcasted_iota(jnp.int32, bat.shape, 0)
    c = lax.broadcasted_iota(jnp.int32, bat.shape, 1) % DN_CHUNK
    bwd_t = row >= 3 * H
    for d in (1, 2, 4, 8, 16, 32):
        f = jnp.where(c >= d, pltpu.roll(gt, d, 1), 0.0)
        b = jnp.where(c < DN_CHUNK - d, pltpu.roll(gt, ts - d, 1), 0.0)
        gt = gt + jnp.where(bwd_t, b, f)
    gt_ref[0] = gt


def _dn_prep(dn_raw, ba, bat, conv_w, a_log, dt_bias, *, ts=256):
    B, S, C = dn_raw.shape
    ts = min(ts, S)
    assert S % ts == 0 and ts % DN_CHUNK == 0
    nt = S // ts
    n_ba = 4 * DN_HEADS
    hb = ts // SUBLANES
    zeros = jnp.zeros((2 * DN_HEADS,), F32)
    al = jnp.concatenate([zeros, a_log.reshape(-1).astype(F32)])
    db = jnp.concatenate([zeros, dt_bias.reshape(-1).astype(F32)])
    ab = jnp.pad(jnp.stack([al, db]), ((0, 0), (0, BA_PAD - n_ba)))
    abt = jnp.stack([al, db], axis=1)
    row = lambda n: pl.BlockSpec((1, ts, n), lambda b, i: (b, i, 0))
    return pl.pallas_call(
        functools.partial(_dnprep_body, ts=ts, n_tiles=nt),
        grid=(B, nt),
        in_specs=[row(C),
                  pl.BlockSpec((1, SUBLANES, C), lambda b, i: (b, jnp.maximum(i * hb - 1, 0), 0)),
                  pl.BlockSpec((1, SUBLANES, C),
                               lambda b, i: (b, jnp.minimum((i + 1) * hb, S // SUBLANES - 1), 0)),
                  _resident((CONV_W, C)), row(BA_PAD),
                  pl.BlockSpec((1, n_ba, ts), lambda b, i: (b, 0, i)),
                  _resident((2, BA_PAD)), _resident((n_ba, 2))],
        out_specs=[row(DN_WIDTH), row(DN_WIDTH), row(DN_WIDTH), row(BA_PAD),
                   pl.BlockSpec((1, n_ba, ts), lambda b, i: (b, 0, i))],
        out_shape=[jax.ShapeDtypeStruct((B, S, DN_WIDTH), F32)] * 3
        + [jax.ShapeDtypeStruct((B, S, BA_PAD), F32), jax.ShapeDtypeStruct((B, n_ba, S), F32)],
        scratch_shapes=[pltpu.VMEM((ts + 2 * SUBLANES, C), F32)],
        compiler_params=pltpu.CompilerParams(dimension_semantics=("parallel", "parallel")),
        name="dn_prep",
    )(dn_raw, dn_raw, dn_raw, conv_w.astype(F32), ba, bat, ab, abt)


def _bd_rows(y, left):
    return jnp.concatenate([jnp.where(left, y, 0.0), jnp.where(left, 0.0, y)], axis=0).astype(BF16)


def _dnscan_body(qf_ref, kf_ref, vf_ref, bgf_ref, gtf_ref, qb_ref, kb_ref, vb_ref, bgb_ref, gtb_ref,
                 of_ref, ob_ref, st_ref, *, G):
    C = DN_CHUNK
    H = DN_HEADS
    P = H // 2

    @pl.when(pl.program_id(1) == 0)
    def _():
        st_ref[...] = jnp.zeros(st_ref.shape, F32)

    lane1 = lax.broadcasted_iota(jnp.int32, (C, LANES), 1)
    row1 = lax.broadcasted_iota(jnp.int32, (C, LANES), 0)
    left1 = lane1 < C
    s_idx = lane1 % C
    eye = (row1 == s_idx).astype(F32)
    lane2 = lax.broadcasted_iota(jnp.int32, (C, 2 * LANES), 1)
    left2 = (lane2 % LANES) < C
    rr = lax.broadcasted_iota(jnp.int32, (LANES, LANES), 0)
    cc = lax.broadcasted_iota(jnp.int32, (LANES, LANES), 1)
    top = rr < C
    bdmask = (rr < C) == (cc < C)
    top_c = lax.broadcasted_iota(jnp.int32, (LANES, C), 0) < C

    dirs = ((0, qf_ref, kf_ref, vf_ref, bgf_ref, gtf_ref, of_ref),
            (1, qb_ref, kb_ref, vb_ref, bgb_ref, gtb_ref, ob_ref))
    kT_cache = {}
    by_key = {}

    def prepare(group):
        chains = []
        for d, p, c in group:
            _, q_ref, k_ref, v_ref, bg_ref, gt_ref, _ = dirs[d]
            tri = (row1 >= s_idx) if d == 0 else (row1 <= s_idx)
            strict = (row1 > s_idx) if d == 0 else (row1 < s_idx)
            last = C - 1 if d == 0 else 0
            sl = slice(p * LANES, (p + 1) * LANES)
            cb = d * H + 2 * p
            cg = 2 * H + d * H + 2 * p
            if (d, p) not in kT_cache:
                kT_cache[(d, p)] = k_ref[0, :, sl].T
            rs = slice(c * C, (c + 1) * C)
            qc, kc, vc = q_ref[0, rs, sl], k_ref[0, rs, sl], v_ref[0, rs, sl]
            bexp = jnp.where(left1, bg_ref[0, rs, cb:cb + 1], bg_ref[0, rs, cb + 1:cb + 2])
            gcc = jnp.where(left1, bg_ref[0, rs, cg:cg + 1], bg_ref[0, rs, cg + 1:cg + 2])
            g0 = gt_ref[0, cg:cg + 1, rs]
            g1 = gt_ref[0, cg + 1:cg + 2, rs]
            grow = jnp.concatenate([g0, g1], axis=1)
            glast = gcc[last:last + 1, :]
            gl0 = glast[:, 0:1]
            gl1 = glast[:, C:C + 1]
            eg = jnp.exp(gcc)
            kb_ = kc * bexp
            fac = jnp.exp(jnp.minimum(jnp.where(top_c, gl0 - g0, gl1 - g1), 0.0))
            chains.append(dict(
                key=(d, p, c),
                rhs=jnp.concatenate([vc * bexp, kb_ * eg], axis=1),
                qg=(qc * eg).astype(BF16),
                kq=jnp.concatenate([kb_, qc], axis=0).astype(BF16),
                Z=_bd_rows(kc, left1),
                dec=jnp.where(tri, jnp.exp(jnp.minimum(gcc - grow, 0.0)), 0.0),
                strict=strict,
                kg=(kT_cache[(d, p)][:, rs] * fac).astype(BF16),
                egl=jnp.exp(glast)))
        yield
        for ch in chains:
            kkqk = _dot_nt(ch["kq"], ch["Z"])
            ch["N"] = jnp.where(ch["strict"], -kkqk[:C] * ch["dec"], 0.0)
            ch["qk"] = (kkqk[C:] * ch["dec"]).astype(BF16)
        for ch in chains:
            ch["Pm"] = eye + ch["N"]
        for j in range(6):
            yield
            for ch in chains:
                nb = _bd_rows(ch["N"], left1)
                if j == 0:
                    ch["N"] = _dot(ch["N"].astype(BF16), nb)
                elif j < 5:
                    both = _dot(jnp.concatenate([ch["Pm"], ch["N"]], axis=0).astype(BF16), nb)
                    ch["Pm"] = ch["Pm"] + both[:C]
                    ch["N"] = both[C:]
                else:
                    ch["Pm"] = ch["Pm"] + _dot(ch["Pm"].astype(BF16), nb)
        yield
        for ch in chains:
            X = _dot(ch["Pm"].astype(BF16), _bd_rows(ch["rhs"], left2))
            by_key[ch["key"]] = dict(
                u=X[:, :LANES], qk=ch["qk"], kg=ch["kg"], egl=ch["egl"],
                wq=jnp.concatenate([X[:, LANES:].astype(BF16), ch["qg"]], axis=0))

    def scan_step(step):
        live = []
        for d, *_refs, o_ref in dirs:
            c = step if d == 0 else G - 1 - step
            for p in range(P):
                live.append((d * P + p, o_ref, slice(c * C, (c + 1) * C), p, by_key[(d, p, c)]))
        S = [st_ref[sidx] for sidx, *_ in live]
        t1 = [_dot(ch["wq"], S[i].astype(BF16)) for i, (*_, ch) in enumerate(live)]
        vn = [ch["u"] - t1[i][:C] for i, (*_, ch) in enumerate(live)]
        for i, (sidx, o_ref, rs, p, ch) in enumerate(live):
            o_ref[0, rs, p * LANES:(p + 1) * LANES] = t1[i][C:] + _dot(ch["qk"], _bd_rows(vn[i], left1))
        for i, (sidx, o_ref, rs, p, ch) in enumerate(live):
            upd = _dot(ch["kg"], vn[i].astype(BF16))
            decay = jnp.where(top, ch["egl"][:, 0:1], ch["egl"][:, C:C + 1])
            st_ref[sidx] = S[i] * decay + jnp.where(bdmask, upd, 0.0)

    bounds = sorted({0, G} | {G - (G * i) // DN_SCAN_GROUPS for i in range(1, DN_SCAN_GROUPS)})
    groups = [range(a, b) for a, b in zip(bounds[:-1], bounds[1:])]
    pending = iter(())
    for steps in groups:
        needed = [(0, p, c) for p in range(P) for c in steps]
        needed += [(1, p, G - 1 - c) for p in range(P) for c in steps]
        for _ in prepare(needed):
            step = next(pending, None)
            if step is not None:
                scan_step(step)
        for step in pending:
            scan_step(step)
        pending = iter(steps)
    for step in pending:
        scan_step(step)


def _dn_scan(q, k, v, bg, gt, *, G=16):
    B, S, _ = q.shape
    G = min(G, S // DN_CHUNK)
    R = G * DN_CHUNK
    assert S % R == 0
    nb = S // R
    n_ba = 4 * DN_HEADS
    P = DN_HEADS // 2
    fw = lambda n: pl.BlockSpec((1, R, n), lambda b, i: (b, i, 0))
    bw = lambda n: pl.BlockSpec((1, R, n), lambda b, i: (b, nb - 1 - i, 0))
    fwt = pl.BlockSpec((1, n_ba, R), lambda b, i: (b, 0, i))
    bwt = pl.BlockSpec((1, n_ba, R), lambda b, i: (b, 0, nb - 1 - i))
    return pl.pallas_call(
        functools.partial(_dnscan_body, G=G),
        grid=(B, nb),
        in_specs=[fw(DN_WIDTH), fw(DN_WIDTH), fw(DN_WIDTH), fw(BA_PAD), fwt,
                  bw(DN_WIDTH), bw(DN_WIDTH), bw(DN_WIDTH), bw(BA_PAD), bwt],
        out_specs=[fw(DN_WIDTH), bw(DN_WIDTH)],
        out_shape=[jax.ShapeDtypeStruct((B, S, DN_WIDTH), F32)] * 2,
        scratch_shapes=[pltpu.VMEM((2 * P, LANES, LANES), F32)],
        compiler_params=pltpu.CompilerParams(dimension_semantics=("parallel", "arbitrary")),
        name="dn_scan",
    )(q, k, v, bg, gt, q, k, v, bg, gt)


def _mixers(x, ln_mix, w_mix_in, conv_w, sink, diff_lam, diff_g, a_log, dt_bias, lam_init):
    B, S, _ = x.shape
    wq, wk, wv, dq, dkt, dv, dn_raw, z, ba, bat, q_sq, k_sq = _inproj(x, ln_mix, w_mix_in)
    win_slopes = 2.0 ** (-8.0 * jnp.arange(1, WIN_Q_HEADS + 1, dtype=F32) / WIN_Q_HEADS)
    o_win = _win_attention(wq, wk, wv, sink, win_slopes)
    o_diff = _diff_attention(dq, dkt, dv, q_sq, k_sq, diff_lam, diff_g, lam_init)
    q, k, v, bg, gt = _dn_prep(dn_raw, ba, bat, conv_w, a_log, dt_bias)
    o_f, o_b = _dn_scan(q, k, v, bg, gt)
    return tuple(t.reshape(B * S, t.shape[-1]) for t in (o_win, o_diff, o_f, o_b, z))


def kernel(x, ln_ffn1, ffn1_w_in, ffn1_w_out, ln_mix, w_mix_in, conv_w, sink_logits, diff_lambda,
           diff_norm_g, dn_A_log, dn_dt_bias, dn_norm_g, w_mix_out, ln_ffn2, ffn2_w_in, ffn2_w_out,
           ln_final):
    B, S, D = x.shape
    depth = ln_ffn1.shape[0]
    for l in range(depth):
        lam_init = 0.8 - 0.6 * math.exp(-0.3 * l)
        x = _ffn(x.reshape(B * S, D), ln_ffn1[l], ffn1_w_in[l], ffn1_w_out[l]).reshape(B, S, D)
        mixed = _mixers(x, ln_mix[l], w_mix_in[l], conv_w[l], sink_logits[l], diff_lambda[l], diff_norm_g[l],
                        dn_A_log[l], dn_dt_bias[l], lam_init)
        g_final = ln_final if l == depth - 1 else None
        x = _ffn(x.reshape(B * S, D), ln_ffn2[l], ffn2_w_in[l], ffn2_w_out[l], g_final,
                 mix=mixed + (dn_norm_g[l], w_mix_out[l])).reshape(B, S, D)
    return x
```

```python
import functools
import math

import jax
import jax.numpy as jnp
from jax import lax
from jax.experimental import pallas as pl
from jax.experimental.pallas import tpu as pltpu

F32 = jnp.float32
BF16 = jnp.bfloat16

D_MODEL = 1024
HEAD_DIM = 64
EPS = 1e-6
WIN_Q_HEADS = 6
WIN_KV_HEADS = 2
WINDOW = 128
DIFF_HEADS = 4
DIFF_QK_DIM = HEAD_DIM // 2
DN_HEADS = 6
DN_CHUNK = 64
CONV_W = 5
D_FF = 2752
WIN_WIDTH = WIN_Q_HEADS * HEAD_DIM
WIN_KV_WIDTH = WIN_KV_HEADS * HEAD_DIM
DIFF_WIDTH = DIFF_HEADS * HEAD_DIM
DN_WIDTH = DN_HEADS * HEAD_DIM
DN_QKV = 3 * DN_WIDTH
MIX_WIDTH = WIN_WIDTH + DIFF_WIDTH + DN_WIDTH
OFF_WQ = 0
OFF_WK = OFF_WQ + WIN_WIDTH
OFF_WV = OFF_WK + WIN_KV_WIDTH
OFF_DQ = OFF_WV + WIN_KV_WIDTH
OFF_DK = OFF_DQ + DIFF_WIDTH
OFF_DV = OFF_DK + DIFF_WIDTH
OFF_DN = OFF_DV + DIFF_WIDTH
OFF_Z = OFF_DN + DN_QKV
OFF_BA = OFF_Z + DN_WIDTH
MIX_IN = OFF_BA + 4 * DN_HEADS

LANES = 128
SUBLANES = 8
BF16_ROWS = 2 * SUBLANES
FF_CHUNK = 256
D_FF_PAD = -(-D_FF // FF_CHUNK) * FF_CHUNK
VMEM_LIMIT = 56 * 1024 * 1024
LOG2E = math.log2(math.e)
DIFF_TK = 512
DIFF_BIAS_ROWS = 3
DIFF_KROWS = -(-(DIFF_QK_DIM + DIFF_BIAS_ROWS) // BF16_ROWS) * BF16_ROWS
DIFF_KT_ROWS = 2 * DIFF_HEADS * DIFF_KROWS
DIFF_V_WIDTH = DIFF_HEADS * LANES
DIFF_SKIP_LOG2 = 150.0
DIFF_BOUND_SLACK = 1.0 + 2.0 ** -6
MASKED = -1e30
DIFF_SLOPES = tuple(2.0 ** (-8.0 * (i + 1) / DIFF_HEADS) for i in range(DIFF_HEADS))
DN_SCAN_GROUPS = 5


def _rms(x, g):
    return x * lax.rsqrt(jnp.mean(x * x, axis=-1, keepdims=True) + EPS) * g


def _silu(x):
    h = 0.5 * x
    return h + h * jnp.tanh(h)


def _seg_sum64(x):
    R, width = x.shape
    n = width // LANES
    rows = jnp.concatenate([x[:, c * LANES:(c + 1) * LANES] for c in range(n)], axis=0)
    r = lax.broadcasted_iota(jnp.int32, (LANES, LANES), 0) < HEAD_DIM
    c = lax.broadcasted_iota(jnp.int32, (LANES, LANES), 1) < HEAD_DIM
    ones_bd = (r == c).astype(F32).astype(BF16)
    hi = rows.astype(BF16)
    r1 = rows - hi.astype(F32)
    mid = r1.astype(BF16)
    lo = (r1 - mid.astype(F32)).astype(BF16)
    ss = _dot(hi, ones_bd) + _dot(mid, ones_bd) + _dot(lo, ones_bd)
    return jnp.concatenate([ss[c * R:(c + 1) * R] for c in range(n)], axis=1)


def _dot(a, b):
    return jnp.dot(a, b, preferred_element_type=F32)


def _dot_nt(a, b):
    return lax.dot_general(a, b, (((1,), (1,)), ((), ())), preferred_element_type=F32)


def _resident(shape):
    nd = len(shape)
    return pl.BlockSpec(shape, lambda *_: (0,) * nd, pipeline_mode=pl.Buffered(1))


def _ffn_body(*refs, n_chunks, final, mixed):
    refs = iter(refs)
    x_ref = next(refs)
    mix_refs = [next(refs) for _ in range(7)] if mixed else None
    g_ref, wg_ref, wu_ref, wo_ref = (next(refs) for _ in range(4))
    gf_ref = next(refs) if final else None
    o_ref, acc_ref = next(refs), next(refs)
    x = x_ref[...]
    if mixed:
        ow_ref, od_ref, of_ref, ob_ref, z_ref, dng_ref, wmix_ref = mix_refs
        cat_ref = next(refs)
        oc = of_ref[...] + ob_ref[...]
        on = oc * lax.rsqrt(_seg_sum64(oc * oc) * (1.0 / HEAD_DIM) + EPS) * dng_ref[...] * _silu(z_ref[...])
        cat_ref[:, :WIN_WIDTH] = ow_ref[...]
        cat_ref[:, WIN_WIDTH:WIN_WIDTH + DIFF_WIDTH] = od_ref[...]
        cat_ref[:, WIN_WIDTH + DIFF_WIDTH:] = on.astype(BF16)
        x = x + _dot(cat_ref[...], wmix_ref[...])
    h = _rms(x, g_ref[...]).astype(BF16)
    for c in range(n_chunks):
        cols = slice(c * FF_CHUNK, (c + 1) * FF_CHUNK)
        a = (_silu(_dot(h, wg_ref[:, cols])) * _dot(h, wu_ref[:, cols])).astype(BF16)
        part = _dot(a, wo_ref[cols, :])
        if c == 0:
            acc_ref[...] = part
        else:
            acc_ref[...] += part
    y = x + 0.5 * acc_ref[...]
    if final:
        y = _rms(y, gf_ref[...])
    o_ref[...] = y


def _prep_ffn_weights(w_in, w_out):
    pad = D_FF_PAD - D_FF
    wg = jnp.pad(w_in[:, :D_FF].astype(BF16), ((0, 0), (0, pad)))
    wu = jnp.pad(w_in[:, D_FF:].astype(BF16), ((0, 0), (0, pad)))
    wo = jnp.pad(w_out.astype(BF16), ((0, pad), (0, 0)))
    return wg, wu, wo


def _ffn(x2d, g, w_in, w_out, g_final=None, mix=None, *, tm=512):
    T = x2d.shape[0]
    tm = min(tm, T)
    assert T % tm == 0
    n_chunks = D_FF_PAD // FF_CHUNK
    wg, wu, wo = _prep_ffn_weights(w_in, w_out)
    final = g_final is not None
    mixed = mix is not None
    row = lambda n: pl.BlockSpec((tm, n), lambda i: (i, 0))
    in_specs = [row(D_MODEL)]
    args = [x2d]
    scratch = [pltpu.VMEM((tm, D_MODEL), F32)]
    if mixed:
        o_win, o_diff, o_f, o_b, z, dn_g, w_mix_out = mix
        in_specs += [row(WIN_WIDTH), row(DIFF_WIDTH), row(DN_WIDTH), row(DN_WIDTH), row(DN_WIDTH),
                     _resident((1, DN_WIDTH)), _resident((MIX_WIDTH, D_MODEL))]
        args += [o_win, o_diff, o_f, o_b, z, jnp.tile(dn_g.astype(F32), DN_HEADS).reshape(1, DN_WIDTH),
                 w_mix_out.astype(BF16)]
        scratch.append(pltpu.VMEM((tm, MIX_WIDTH), BF16))
    in_specs += [_resident((1, D_MODEL)), _resident(wg.shape), _resident(wu.shape), _resident(wo.shape)]
    args += [g.reshape(1, D_MODEL), wg, wu, wo]
    if final:
        in_specs.append(_resident((1, D_MODEL)))
        args.append(g_final.reshape(1, D_MODEL))
    return pl.pallas_call(
        functools.partial(_ffn_body, n_chunks=n_chunks, final=final, mixed=mixed),
        grid=(T // tm,),
        in_specs=in_specs,
        out_specs=row(D_MODEL),
        out_shape=jax.ShapeDtypeStruct((T, D_MODEL), F32),
        scratch_shapes=scratch,
        compiler_params=pltpu.CompilerParams(
            dimension_semantics=("parallel",), vmem_limit_bytes=VMEM_LIMIT),
        name="ffn" + ("_mix" if mixed else "") + ("_final" if final else ""),
    )(*args)


BA_PAD = LANES


def _inproj_body(x_ref, g_ref, w_ref, wt_ref, wdv_ref,
                 wq_ref, wk_ref, wv_ref, dq_ref, dkt_ref, dv_ref, dn_ref, z_ref, ba_ref, bat_ref,
                 qn_ref, kn_ref):
    h = _rms(x_ref[0], g_ref[...]).astype(BF16)
    wq_ref[0] = (_dot(h, w_ref[:, OFF_WQ:OFF_WK]) * (HEAD_DIM ** -0.5 * LOG2E)).astype(BF16)
    wk_ref[0] = _dot(h, w_ref[:, OFF_WK:OFF_WV]).astype(BF16)
    wv_ref[0] = _dot(h, w_ref[:, OFF_WV:OFF_DQ]).astype(BF16)
    dq = (_dot(h, w_ref[:, OFF_DQ:OFF_DK]) * (DIFF_QK_DIM ** -0.5 * LOG2E)).astype(BF16)
    dq_ref[0] = dq
    n_tiles = dq.shape[0] // DIFF_TK
    dqf = dq.astype(F32)
    group = (lax.broadcasted_iota(jnp.int32, (DIFF_WIDTH, LANES), 0) // DIFF_QK_DIM
             == lax.broadcasted_iota(jnp.int32, (DIFF_WIDTH, LANES), 1))
    q_sq = _dot((dqf * dqf).astype(BF16), group.astype(F32).astype(BF16))
    for a in range(n_tiles):
        qn_ref[0, a] = jnp.broadcast_to(
            jnp.max(q_sq[a * DIFF_TK:(a + 1) * DIFF_TK], axis=0, keepdims=True), (SUBLANES, LANES))
    wt_out = lax.dot_general(wt_ref[...], h, (((0,), (1,)), ((), ())),
                             preferred_element_type=F32)
    kt = wt_out[:DIFF_WIDTH]
    tm = kt.shape[1]
    extra_rows = DIFF_KROWS - DIFF_QK_DIM
    pos = pl.program_id(1) * tm + lax.broadcasted_iota(jnp.int32, (extra_rows, tm), 1)
    kr = (pos % DIFF_TK).astype(F32)
    rowi = lax.broadcasted_iota(jnp.int32, (extra_rows, tm), 0)
    k_sq = []
    for hd in range(DIFF_HEADS):
        bias = (DIFF_SLOPES[hd] * LOG2E) * kr
        hi = bias.astype(BF16).astype(F32)
        mid = (bias - hi).astype(BF16).astype(F32)
        lo = bias - hi - mid
        extra = jnp.where(rowi == 0, hi, jnp.where(rowi == 1, mid, jnp.where(rowi == 2, lo, 0.0))).astype(BF16)
        for mp in range(2):
            hm = 2 * hd + mp
            kb = kt[hm * DIFF_QK_DIM:(hm + 1) * DIFF_QK_DIM, :].astype(BF16)
            dkt_ref[0, hm * DIFF_KROWS:hm * DIFF_KROWS + DIFF_QK_DIM, :] = kb
            dkt_ref[0, hm * DIFF_KROWS + DIFF_QK_DIM:(hm + 1) * DIFF_KROWS, :] = extra
            kbf = kb.astype(F32)
            k_sq.append(jnp.sum(kbf * kbf, axis=0, keepdims=True))
    k_sq = jnp.concatenate(k_sq, axis=0)
    for a in range(n_tiles):
        kn_ref[0, a] = jnp.broadcast_to(
            jnp.max(k_sq[:, a * DIFF_TK:(a + 1) * DIFF_TK], axis=1, keepdims=True), (SUBLANES, LANES))
    dv = _dot(h, wdv_ref[...])
    ones_half = (lax.broadcasted_iota(jnp.int32, dv.shape, 1) % LANES) >= HEAD_DIM
    dv_ref[0] = jnp.where(ones_half, 1.0, dv).astype(BF16)
    dn_ref[0] = _dot(h, w_ref[:, OFF_DN:OFF_Z])
    z_ref[0] = _dot(h, w_ref[:, OFF_Z:OFF_BA])
    ba_ref[0] = _dot(h, w_ref[:, OFF_BA:OFF_BA + BA_PAD])
    bat_ref[0] = wt_out[DIFF_WIDTH:DIFF_WIDTH + bat_ref.shape[1]]


def _inproj(x, g, w_in, *, tm=1024):
    B, S, _ = x.shape
    tm = min(tm, S)
    assert S % tm == 0 and tm % DIFF_TK == 0
    n_ba = 4 * DN_HEADS
    w_in = w_in.astype(BF16)
    w = jnp.pad(w_in, ((0, 0), (0, OFF_BA + BA_PAD - MIX_IN)))
    wt = jnp.pad(jnp.concatenate([w_in[:, OFF_DK:OFF_DV], w_in[:, OFF_BA:MIX_IN]], axis=1),
                 ((0, 0), (0, 2 * BF16_ROWS - n_ba % (2 * BF16_ROWS))))
    wdv = jnp.pad(w_in[:, OFF_DV:OFF_DN].reshape(D_MODEL, DIFF_HEADS, HEAD_DIM),
                  ((0, 0), (0, 0), (0, LANES - HEAD_DIM))).reshape(D_MODEL, DIFF_V_WIDTH)
    row = lambda n: pl.BlockSpec((1, tm, n), lambda b, i: (b, i, 0))
    col = lambda n: pl.BlockSpec((1, n, tm), lambda b, i: (b, 0, i))
    out_shapes = [
        jax.ShapeDtypeStruct((B, S, WIN_WIDTH), BF16),
        jax.ShapeDtypeStruct((B, S, WIN_KV_WIDTH), BF16),
        jax.ShapeDtypeStruct((B, S, WIN_KV_WIDTH), BF16),
        jax.ShapeDtypeStruct((B, S, DIFF_WIDTH), BF16),
        jax.ShapeDtypeStruct((B, DIFF_KT_ROWS, S), BF16),
        jax.ShapeDtypeStruct((B, S, DIFF_V_WIDTH), BF16),
        jax.ShapeDtypeStruct((B, S, DN_QKV), F32),
        jax.ShapeDtypeStruct((B, S, DN_WIDTH), F32),
        jax.ShapeDtypeStruct((B, S, BA_PAD), F32),
        jax.ShapeDtypeStruct((B, n_ba, S), F32),
        jax.ShapeDtypeStruct((B, S // DIFF_TK, SUBLANES, LANES), F32),
        jax.ShapeDtypeStruct((B, S // DIFF_TK, SUBLANES, LANES), F32),
    ]
    norms = pl.BlockSpec((1, tm // DIFF_TK, SUBLANES, LANES), lambda b, i: (b, i, 0, 0))
    out_specs = [row(WIN_WIDTH), row(WIN_KV_WIDTH), row(WIN_KV_WIDTH), row(DIFF_WIDTH),
                 col(DIFF_KT_ROWS), row(DIFF_V_WIDTH), row(DN_QKV), row(DN_WIDTH), row(BA_PAD), col(n_ba),
                 norms, norms]
    return pl.pallas_call(
        _inproj_body,
        grid=(B, S // tm),
        in_specs=[row(D_MODEL), _resident((1, D_MODEL)), _resident(w.shape),
                  _resident(wt.shape), _resident(wdv.shape)],
        out_specs=out_specs,
        out_shape=out_shapes,
        compiler_params=pltpu.CompilerParams(
            dimension_semantics=("parallel", "parallel"), vmem_limit_bytes=VMEM_LIMIT),
        name="inproj",
    )(x, g.reshape(1, D_MODEL), w, wt, wdv)


def _win_body(q_ref, kp_ref, kc_ref, kn_ref, vp_ref, vc_ref, vn_ref, sink_ref, slope_ref, o_ref,
              *, n_steps, nq):
    i = pl.program_id(1)
    W = WINDOW
    KW = 3 * W
    kw = jnp.concatenate([kp_ref[0], kc_ref[0], kn_ref[0]], axis=0).astype(F32)
    vw = jnp.concatenate([vp_ref[0], vc_ref[0], vn_ref[0]], axis=0).astype(F32)
    kw_sw = pltpu.roll(kw, HEAD_DIM, 1)
    vw_sw = pltpu.roll(vw, HEAD_DIM, 1)
    left = lax.broadcasted_iota(jnp.int32, (KW, LANES), 1) < HEAD_DIM
    qi = lax.broadcasted_iota(jnp.int32, (W, KW), 0)
    ki = lax.broadcasted_iota(jnp.int32, (W, KW), 1)
    dist = jnp.abs(ki - W - qi)
    band_bias = [jnp.where(dist <= W, (-LOG2E) * slope_ref[:, hq:hq + 1] * dist.astype(F32), MASKED)
                 for hq in range(WIN_Q_HEADS)]
    edge_lo = jnp.where(jnp.logical_or(ki >= W, i > 0), 0.0, MASKED)
    edge_hi = jnp.where(jnp.logical_or(ki < 2 * W, i < n_steps - 1), 0.0, MASKED)
    n_pairs = WIN_Q_HEADS // 2
    ones_stack = jnp.concatenate([jnp.where(left, 1.0, 0.0), jnp.where(left, 0.0, 1.0)], axis=0).astype(BF16)
    blocks = []
    for a in range(nq):
        rows = slice(a * W, a * W + KW)

        def stack(first, second, rows=rows):
            return jnp.concatenate([jnp.where(left, first[rows], 0.0),
                                    jnp.where(left, 0.0, second[rows])], axis=0).astype(BF16)

        zk = (stack(kw, kw_sw), stack(kw, kw), stack(kw_sw, kw))
        zv = tuple(jnp.concatenate([z, ones_stack], axis=1)
                   for z in (stack(vw, vw_sw), stack(vw, vw), stack(vw_sw, vw)))
        edge = None
        if a == 0:
            edge = edge_lo
        if a == nq - 1:
            edge = edge_hi if edge is None else edge + edge_hi
        q = q_ref[0, a * W:(a + 1) * W, :]
        s2 = [_dot_nt(q[:, p * LANES:(p + 1) * LANES], zk[p]) for p in range(n_pairs)]
        blocks.append((a, edge, zv, s2))
    weights, sink_terms = {}, {}
    for a, edge, zv, s2 in blocks:
        for hq in range(WIN_Q_HEADS):
            p, t = divmod(hq, 2)
            s = s2[p][:, t * KW:(t + 1) * KW] + band_bias[hq]
            if edge is not None:
                s = s + edge
            sink = LOG2E * sink_ref[:, hq:hq + 1]
            m = jnp.maximum(jnp.max(s, axis=-1, keepdims=True), sink)
            weights[(a, hq)] = jnp.exp2(s - m).astype(BF16)
            sink_terms[(a, hq)] = jnp.exp2(sink - m)
    left_w = lax.broadcasted_iota(jnp.int32, (W, LANES), 1) < HEAD_DIM
    for a, edge, zv, s2 in blocks:
        for p in range(n_pairs):
            ee = jnp.concatenate([weights[(a, 2 * p)], weights[(a, 2 * p + 1)]], axis=1)
            acc = _dot(ee, zv[p])
            denom = acc[:, LANES:] + jnp.where(left_w, sink_terms[(a, 2 * p)], sink_terms[(a, 2 * p + 1)])
            o_ref[0, a * W:(a + 1) * W, p * LANES:(p + 1) * LANES] = (acc[:, :LANES] / denom).astype(BF16)


def _win_attention(q, k, v, sink, slopes, *, nq=8):
    B, S, _ = q.shape
    W = WINDOW
    nq = min(nq, S // W)
    assert S % (nq * W) == 0
    ns = S // (nq * W)
    nb = S // W
    qspec = pl.BlockSpec((1, nq * W, WIN_WIDTH), lambda b, i: (b, i, 0))
    prev = pl.BlockSpec((1, W, WIN_KV_WIDTH), lambda b, i: (b, jnp.maximum(i * nq - 1, 0), 0))
    cur = pl.BlockSpec((1, nq * W, WIN_KV_WIDTH), lambda b, i: (b, i, 0))
    nxt = pl.BlockSpec((1, W, WIN_KV_WIDTH), lambda b, i: (b, jnp.minimum((i + 1) * nq, nb - 1), 0))
    return pl.pallas_call(
        functools.partial(_win_body, n_steps=ns, nq=nq),
        grid=(B, ns),
        in_specs=[qspec, prev, cur, nxt, prev, cur, nxt,
                  _resident((1, WIN_Q_HEADS)), _resident((1, WIN_Q_HEADS))],
        out_specs=pl.BlockSpec((1, nq * W, WIN_WIDTH), lambda b, i: (b, i, 0)),
        out_shape=jax.ShapeDtypeStruct((B, S, WIN_WIDTH), BF16),
        compiler_params=pltpu.CompilerParams(dimension_semantics=("parallel", "parallel")),
        name="win_attn",
    )(q, k, k, k, v, v, v, sink.reshape(1, WIN_Q_HEADS).astype(F32), slopes.reshape(1, WIN_Q_HEADS))


def _diff_ranges(q_sq, k_sq, tq, tk):
    assert tq == tk == DIFF_TK
    n_maps = 2 * DIFF_HEADS
    B, nq = q_sq.shape[:2]
    nk = nq
    qn = jnp.sqrt(q_sq[:, :, 0, :n_maps])
    kn = jnp.swapaxes(jnp.sqrt(k_sq[:, :, :n_maps, 0]), 1, 2)
    kown = kn
    reach = jnp.swapaxes(qn, 1, 2)[..., None] * (kn[:, :, None, :] + kown[..., None]) * DIFF_BOUND_SLACK
    reach = jnp.max(reach.reshape(B, DIFF_HEADS, 2, nq, nk), axis=2)
    q0 = jnp.arange(nq, dtype=jnp.int32) * tq
    k0 = jnp.arange(nk, dtype=jnp.int32) * tk
    dmin = jnp.maximum(jnp.maximum(k0[None, :] - (q0[:, None] + tq - 1), q0[:, None] - (k0[None, :] + tk - 1)), 0)
    c = jnp.asarray(DIFF_SLOPES, F32) * LOG2E
    need = reach - c[None, :, None, None] * dmin.astype(F32)[None, None] > -DIFF_SKIP_LOG2
    idx = jnp.arange(nk, dtype=jnp.int32)
    need = jnp.logical_or(need, (idx[None, :] == (q0 // tk)[:, None])[None, None])
    lo = jnp.min(jnp.where(need, idx, nk), axis=-1)
    hi = jnp.max(jnp.where(need, idx, -1), axis=-1)
    return jnp.stack([lo, hi], axis=-1).transpose(0, 2, 1, 3).reshape(-1).astype(jnp.int32)


def _diff_body(rng_ref, q_ref, kt_ref, v_ref, lam_ref, g_ref, o_ref, qv_ref, m_ref, acc_ref,
               s0_ref, st0_ref, s1_ref, st1_ref, *, tq, tk, lam_init):
    n_maps = 2 * DIFF_HEADS
    n_groups = tk // LANES
    q0 = pl.program_id(1) * tq
    jd = q0 // tk
    q = q_ref[0]
    lane_x = lax.broadcasted_iota(jnp.int32, (tq, DIFF_KROWS - DIFF_QK_DIM), 1)
    for var, sign in enumerate((1.0, -1.0, 0.0)):
        extras = jnp.where(lane_x < DIFF_BIAS_ROWS, -sign, 0.0).astype(BF16)
        for hm in range(n_maps):
            qv_ref[var, hm, :, 0:DIFF_QK_DIM] = q[:, hm * DIFF_QK_DIM:(hm + 1) * DIFF_QK_DIM]
            qv_ref[var, hm, :, DIFF_QK_DIM:DIFF_KROWS] = extras
    m_ref[...] = jnp.full(m_ref.shape, -jnp.inf, F32)
    acc_ref[...] = jnp.zeros(acc_ref.shape, F32)
    qpos = (q0 + lax.broadcasted_iota(jnp.int32, (tq, LANES), 0)).astype(F32)
    bufs = ((s0_ref, st0_ref), (s1_ref, st1_ref))
    rng_base = (pl.program_id(0) * pl.num_programs(1) + pl.program_id(1)) * (2 * DIFF_HEADS)

    los = [rng_ref[rng_base + 2 * h] for h in range(DIFF_HEADS)]
    n_offs = [rng_ref[rng_base + 2 * h + 1] - los[h] for h in range(DIFF_HEADS)]
    starts = [0]
    for h in range(DIFF_HEADS):
        starts.append(starts[-1] + n_offs[h])
    n_total = starts[DIFF_HEADS]

    def pick(h, vals):
        out = vals[-1]
        for i in range(len(vals) - 2, -1, -1):
            out = jnp.where(h == i, vals[i], out)
        return out

    def locate(u):
        h = sum((u >= starts[i]).astype(jnp.int32) for i in range(1, DIFF_HEADS))
        t = pick(h, los) + u - pick(h, starts[:DIFF_HEADS])
        return h, t + (t >= jd).astype(jnp.int32)

    def scores(h, j, buf, diag_dist=None):
        s_ref, st_ref = bufs[buf]
        if diag_dist is not None:
            var = 2
            bias = (DIFF_SLOPES[h] * LOG2E) * diag_dist
        else:
            c = pick(h, [sl * LOG2E for sl in DIFF_SLOPES])
            after = j > jd
            var = jnp.where(after, 0, 1)
            rowoff = jnp.where(after, c, -c) * ((j * tk).astype(F32) - qpos)
        k0 = pl.multiple_of(j * tk, tk)
        for mp in range(2):
            hm = 2 * h + mp
            r0 = hm * DIFF_KROWS if diag_dist is not None else pl.multiple_of(hm * DIFF_KROWS, BF16_ROWS)
            kt = kt_ref[0, pl.ds(r0, DIFF_KROWS), pl.ds(k0, tk)]
            s = _dot(qv_ref[var, hm], kt)
            if diag_dist is not None:
                s = s - bias
            s_ref[mp] = s
            mx = s[:, 0:LANES]
            for g in range(1, n_groups):
                mx = jnp.maximum(mx, s[:, g * LANES:(g + 1) * LANES])
            mrow = jnp.broadcast_to(jnp.max(mx, axis=-1, keepdims=True), (tq, LANES))
            m_old = m_ref[hm]
            if diag_dist is not None:
                m_new = jnp.maximum(m_old, mrow)
                shift = m_new
            else:
                m_new = jnp.maximum(m_old, mrow - rowoff)
                shift = m_new + rowoff
            st_ref[mp, 0] = shift
            st_ref[mp, 1] = jnp.exp2(m_old - m_new)
            m_ref[hm] = m_new

    def values(h, j, buf, static=False):
        s_ref, st_ref = bufs[buf]
        k0 = pl.multiple_of(j * tk, tk)
        c0 = h * LANES if static else pl.multiple_of(h * LANES, LANES)
        v2 = v_ref[0, pl.ds(k0, tk), pl.ds(c0, LANES)]
        for mp in range(2):
            hm = 2 * h + mp
            shift = st_ref[mp, 0]
            alpha = st_ref[mp, 1]
            e = jnp.exp2((s_ref[mp] - jnp.concatenate([shift] * n_groups, axis=1)).astype(BF16))
            acc_ref[hm] = alpha * acc_ref[hm] + _dot(e, v2)

    kpos = (jd * tk + lax.broadcasted_iota(jnp.int32, (tq, tk), 1)).astype(F32)
    qp = (q0 + lax.broadcasted_iota(jnp.int32, (tq, tk), 0)).astype(F32)
    diag_dist = jnp.abs(qp - kpos)
    scores(0, jd, 0, diag_dist)
    for h in range(DIFF_HEADS):
        if h + 1 < DIFF_HEADS:
            scores(h + 1, jd, (h + 1) % 2, diag_dist)
        values(h, jd, h % 2, static=True)

    def s_at(u, buf):
        h, j = locate(u)
        scores(h, j, buf)

    def v_at(u, buf):
        h, j = locate(u)
        values(h, j, buf)

    @pl.when(n_total > 0)
    def _():
        s_at(0, 0)

    def pair(p, carry):
        u = 2 * p
        s_at(u + 1, 1)
        v_at(u, 0)
        s_at(u + 2, 0)
        v_at(u + 1, 1)
        return carry

    n_pairs = jnp.maximum(n_total - 1, 0) // 2
    lax.fori_loop(0, n_pairs, pair, 0)
    u_last = 2 * n_pairs
    left_over = n_total - u_last

    @pl.when(left_over == 2)
    def _():
        s_at(u_last + 1, 1)
        v_at(u_last, 0)
        v_at(u_last + 1, 1)

    @pl.when(left_over == 1)
    def _():
        v_at(u_last, 0)

    lp = lam_ref[...]
    lam = (jnp.exp(jnp.sum(lp[0:1] * lp[1:2], axis=-1, keepdims=True))
           - jnp.exp(jnp.sum(lp[2:3] * lp[3:4], axis=-1, keepdims=True)) + lam_init)
    lane = lax.broadcasted_iota(jnp.int32, (tq, LANES), 1)
    left = lane < HEAD_DIM
    ys = []
    for h in range(DIFF_HEADS):
        a1, a2 = acc_ref[2 * h], acc_ref[2 * h + 1]
        od = a1 / pltpu.roll(a1, HEAD_DIM, 1) - lam * (a2 / pltpu.roll(a2, HEAD_DIM, 1))
        ms = jnp.sum(jnp.where(left, od * od, 0.0), axis=-1, keepdims=True) * (1.0 / HEAD_DIM)
        ys.append(od * lax.rsqrt(ms + EPS) * g_ref[...] * (1.0 - lam_init))
    for p in range(DIFF_HEADS // 2):
        o_ref[0, :, p * LANES:(p + 1) * LANES] = jnp.where(
            left, ys[2 * p], pltpu.roll(ys[2 * p + 1], HEAD_DIM, 1)).astype(BF16)


def _diff_attention(q, kt, v, q_sq, k_sq, diff_lambda, diff_g, lam_init, *, tq=DIFF_TK):
    B, S, _ = q.shape
    tk = DIFF_TK
    tq = min(tq, S)
    assert S % tk == 0 and tk % tq == 0
    n_maps = 2 * DIFF_HEADS
    g2 = jnp.concatenate([diff_g, diff_g]).reshape(1, LANES).astype(F32)
    grid_spec = pltpu.PrefetchScalarGridSpec(
        num_scalar_prefetch=1,
        grid=(B, S // tq),
        in_specs=[pl.BlockSpec((1, tq, DIFF_WIDTH), lambda b, i, r: (b, i, 0)),
                  pl.BlockSpec((1, DIFF_KT_ROWS, S), lambda b, i, r: (b, 0, 0)),
                  pl.BlockSpec((1, S, DIFF_V_WIDTH), lambda b, i, r: (b, 0, 0)),
                  pl.BlockSpec((4, DIFF_QK_DIM), lambda b, i, r: (0, 0)),
                  pl.BlockSpec((1, LANES), lambda b, i, r: (0, 0))],
        out_specs=pl.BlockSpec((1, tq, DIFF_WIDTH), lambda b, i, r: (b, i, 0)),
        scratch_shapes=[pltpu.VMEM((3, n_maps, tq, DIFF_KROWS), BF16),
                        pltpu.VMEM((n_maps, tq, LANES), F32),
                        pltpu.VMEM((n_maps, tq, LANES), F32),
                        pltpu.VMEM((2, tq, tk), F32), pltpu.VMEM((2, 2, tq, LANES), F32),
                        pltpu.VMEM((2, tq, tk), F32), pltpu.VMEM((2, 2, tq, LANES), F32)])
    return pl.pallas_call(
        functools.partial(_diff_body, tq=tq, tk=tk, lam_init=lam_init),
        grid_spec=grid_spec,
        out_shape=jax.ShapeDtypeStruct((B, S, DIFF_WIDTH), BF16),
        compiler_params=pltpu.CompilerParams(
            dimension_semantics=("parallel", "arbitrary"), vmem_limit_bytes=VMEM_LIMIT),
        name="diff_attn",
    )(_diff_ranges(q_sq, k_sq, tq, tk), q, kt, v, diff_lambda.astype(F32), g2)


def _l2n64(x):
    return x * lax.rsqrt(_seg_sum64(x * x) + EPS)


def _dnprep_body(x_ref, xp_ref, xn_ref, w_ref, ba_ref, bat_ref, ab_ref, abt_ref,
                 q_ref, k_ref, v_ref, bg_ref, gt_ref, xx_ref, *, ts, n_tiles):
    i = pl.program_id(1)
    H = DN_HEADS
    halo = SUBLANES
    xx_ref[0:halo, :] = jnp.where(i > 0, xp_ref[0], 0.0)
    xx_ref[halo:halo + ts, :] = x_ref[0]
    xx_ref[halo + ts:2 * halo + ts, :] = jnp.where(i < n_tiles - 1, xn_ref[0], 0.0)
    half = CONV_W // 2
    y = w_ref[half:half + 1, :] * x_ref[0]
    for j in range(CONV_W):
        if j != half:
            y = y + w_ref[j:j + 1, :] * xx_ref[halo - half + j:halo - half + j + ts, :]
    y = _silu(y)
    q_ref[0] = _l2n64(y[:, :DN_WIDTH]) * (HEAD_DIM ** -0.5)
    k_ref[0] = _l2n64(y[:, DN_WIDTH:2 * DN_WIDTH])
    v_ref[0] = y[:, 2 * DN_WIDTH:]

    def gates(raw, a_log, dt_bias):
        beta = 1.0 / (1.0 + jnp.exp(-raw))
        z = raw + dt_bias
        softplus = jnp.maximum(z, 0.0) + jnp.log(1.0 + jnp.exp(-jnp.abs(z)))
        return beta, -jnp.exp(a_log) * softplus

    ba = ba_ref[0]
    beta, g = gates(ba, ab_ref[0:1, :], ab_ref[1:2, :])
    lane = lax.broadcasted_iota(jnp.int32, ba.shape, 1)
    r = lax.broadcasted_iota(jnp.int32, ba.shape, 0) % DN_CHUNK
    bwd = jnp.logical_and(lane >= 3 * H, lane < 4 * H)
    gc = g
    for d in (1, 2, 4, 8, 16, 32):
        f = jnp.where(r >= d, pltpu.roll(gc, d, 0), 0.0)
        b = jnp.where(r < DN_CHUNK - d, pltpu.roll(gc, ts - d, 0), 0.0)
        gc = gc + jnp.where(bwd, b, f)
    bg_ref[0] = jnp.where(lane < 2 * H, beta, gc)
    bat = bat_ref[0]
    _, gt = gates(bat, abt_ref[:, 0:1], abt_ref[:, 1:2])
    row = lax.broadcasted_iota(jnp.int32, bat.shape, 0)
    c = lax.broadcasted_iota(jnp.int32, bat.shape, 1) % DN_CHUNK
    bwd_t = row >= 3 * H
    for d in (1, 2, 4, 8, 16, 32):
        f = jnp.where(c >= d, pltpu.roll(gt, d, 1), 0.0)
        b = jnp.where(c < DN_CHUNK - d, pltpu.roll(gt, ts - d, 1), 0.0)
        gt = gt + jnp.where(bwd_t, b, f)
    gt_ref[0] = gt


def _dn_prep(dn_raw, ba, bat, conv_w, a_log, dt_bias, *, ts=256):
    B, S, C = dn_raw.shape
    ts = min(ts, S)
    assert S % ts == 0 and ts % DN_CHUNK == 0
    nt = S // ts
    n_ba = 4 * DN_HEADS
    hb = ts // SUBLANES
    zeros = jnp.zeros((2 * DN_HEADS,), F32)
    al = jnp.concatenate([zeros, a_log.reshape(-1).astype(F32)])
    db = jnp.concatenate([zeros, dt_bias.reshape(-1).astype(F32)])
    ab = jnp.pad(jnp.stack([al, db]), ((0, 0), (0, BA_PAD - n_ba)))
    abt = jnp.stack([al, db], axis=1)
    row = lambda n: pl.BlockSpec((1, ts, n), lambda b, i: (b, i, 0))
    return pl.pallas_call(
        functools.partial(_dnprep_body, ts=ts, n_tiles=nt),
        grid=(B, nt),
        in_specs=[row(C),
                  pl.BlockSpec((1, SUBLANES, C), lambda b, i: (b, jnp.maximum(i * hb - 1, 0), 0)),
                  pl.BlockSpec((1, SUBLANES, C),
                               lambda b, i: (b, jnp.minimum((i + 1) * hb, S // SUBLANES - 1), 0)),
                  _resident((CONV_W, C)), row(BA_PAD),
                  pl.BlockSpec((1, n_ba, ts), lambda b, i: (b, 0, i)),
                  _resident((2, BA_PAD)), _resident((n_ba, 2))],
        out_specs=[row(DN_WIDTH), row(DN_WIDTH), row(DN_WIDTH), row(BA_PAD),
                   pl.BlockSpec((1, n_ba, ts), lambda b, i: (b, 0, i))],
        out_shape=[jax.ShapeDtypeStruct((B, S, DN_WIDTH), F32)] * 3
        + [jax.ShapeDtypeStruct((B, S, BA_PAD), F32), jax.ShapeDtypeStruct((B, n_ba, S), F32)],
        scratch_shapes=[pltpu.VMEM((ts + 2 * SUBLANES, C), F32)],
        compiler_params=pltpu.CompilerParams(dimension_semantics=("parallel", "parallel")),
        name="dn_prep",
    )(dn_raw, dn_raw, dn_raw, conv_w.astype(F32), ba, bat, ab, abt)


def _bd_rows(y, left):
    return jnp.concatenate([jnp.where(left, y, 0.0), jnp.where(left, 0.0, y)], axis=0).astype(BF16)


def _dnscan_body(qf_ref, kf_ref, vf_ref, bgf_ref, gtf_ref, qb_ref, kb_ref, vb_ref, bgb_ref, gtb_ref,
                 of_ref, ob_ref, st_ref, *, G):
    C = DN_CHUNK
    H = DN_HEADS
    P = H // 2

    @pl.when(pl.program_id(1) == 0)
    def _():
        st_ref[...] = jnp.zeros(st_ref.shape, F32)

    lane1 = lax.broadcasted_iota(jnp.int32, (C, LANES), 1)
    row1 = lax.broadcasted_iota(jnp.int32, (C, LANES), 0)
    left1 = lane1 < C
    s_idx = lane1 % C
    eye = (row1 == s_idx).astype(F32)
    lane2 = lax.broadcasted_iota(jnp.int32, (C, 2 * LANES), 1)
    left2 = (lane2 % LANES) < C
    rr = lax.broadcasted_iota(jnp.int32, (LANES, LANES), 0)
    cc = lax.broadcasted_iota(jnp.int32, (LANES, LANES), 1)
    top = rr < C
    bdmask = (rr < C) == (cc < C)
    top_c = lax.broadcasted_iota(jnp.int32, (LANES, C), 0) < C

    dirs = ((0, qf_ref, kf_ref, vf_ref, bgf_ref, gtf_ref, of_ref),
            (1, qb_ref, kb_ref, vb_ref, bgb_ref, gtb_ref, ob_ref))
    kT_cache = {}
    by_key = {}

    def prepare(group):
        chains = []
        for d, p, c in group:
            _, q_ref, k_ref, v_ref, bg_ref, gt_ref, _ = dirs[d]
            tri = (row1 >= s_idx) if d == 0 else (row1 <= s_idx)
            strict = (row1 > s_idx) if d == 0 else (row1 < s_idx)
            last = C - 1 if d == 0 else 0
            sl = slice(p * LANES, (p + 1) * LANES)
            cb = d * H + 2 * p
            cg = 2 * H + d * H + 2 * p
            if (d, p) not in kT_cache:
                kT_cache[(d, p)] = k_ref[0, :, sl].T
            rs = slice(c * C, (c + 1) * C)
            qc, kc, vc = q_ref[0, rs, sl], k_ref[0, rs, sl], v_ref[0, rs, sl]
            bexp = jnp.where(left1, bg_ref[0, rs, cb:cb + 1], bg_ref[0, rs, cb + 1:cb + 2])
            gcc = jnp.where(left1, bg_ref[0, rs, cg:cg + 1], bg_ref[0, rs, cg + 1:cg + 2])
            g0 = gt_ref[0, cg:cg + 1, rs]
            g1 = gt_ref[0, cg + 1:cg + 2, rs]
            grow = jnp.concatenate([g0, g1], axis=1)
            glast = gcc[last:last + 1, :]
            gl0 = glast[:, 0:1]
            gl1 = glast[:, C:C + 1]
            eg = jnp.exp(gcc)
            kb_ = kc * bexp
            fac = jnp.exp(jnp.minimum(jnp.where(top_c, gl0 - g0, gl1 - g1), 0.0))
            chains.append(dict(
                key=(d, p, c),
                rhs=jnp.concatenate([vc * bexp, kb_ * eg], axis=1),
                qg=(qc * eg).astype(BF16),
                kq=jnp.concatenate([kb_, qc], axis=0).astype(BF16),
                Z=_bd_rows(kc, left1),
                dec=jnp.where(tri, jnp.exp(jnp.minimum(gcc - grow, 0.0)), 0.0),
                strict=strict,
                kg=(kT_cache[(d, p)][:, rs] * fac).astype(BF16),
                egl=jnp.exp(glast)))
        yield
        for ch in chains:
            kkqk = _dot_nt(ch["kq"], ch["Z"])
            ch["N"] = jnp.where(ch["strict"], -kkqk[:C] * ch["dec"], 0.0)
            ch["qk"] = (kkqk[C:] * ch["dec"]).astype(BF16)
        for ch in chains:
            ch["Pm"] = eye + ch["N"]
        for j in range(6):
            yield
            for ch in chains:
                nb = _bd_rows(ch["N"], left1)
                if j == 0:
                    ch["N"] = _dot(ch["N"].astype(BF16), nb)
                elif j < 5:
                    both = _dot(jnp.concatenate([ch["Pm"], ch["N"]], axis=0).astype(BF16), nb)
                    ch["Pm"] = ch["Pm"] + both[:C]
                    ch["N"] = both[C:]
                else:
                    ch["Pm"] = ch["Pm"] + _dot(ch["Pm"].astype(BF16), nb)
        yield
        for ch in chains:
            X = _dot(ch["Pm"].astype(BF16), _bd_rows(ch["rhs"], left2))
            by_key[ch["key"]] = dict(
                u=X[:, :LANES], qk=ch["qk"], kg=ch["kg"], egl=ch["egl"],
                wq=jnp.concatenate([X[:, LANES:].astype(BF16), ch["qg"]], axis=0))

    def scan_step(step):
        live = []
        for d, *_refs, o_ref in dirs:
            c = step if d == 0 else G - 1 - step
            for p in range(P):
                live.append((d * P + p, o_ref, slice(c * C, (c + 1) * C), p, by_key[(d, p, c)]))
        S = [st_ref[sidx] for sidx, *_ in live]
        t1 = [_dot(ch["wq"], S[i].astype(BF16)) for i, (*_, ch) in enumerate(live)]
        vn = [ch["u"] - t1[i][:C] for i, (*_, ch) in enumerate(live)]
        for i, (sidx, o_ref, rs, p, ch) in enumerate(live):
            o_ref[0, rs, p * LANES:(p + 1) * LANES] = t1[i][C:] + _dot(ch["qk"], _bd_rows(vn[i], left1))
        for i, (sidx, o_ref, rs, p, ch) in enumerate(live):
            upd = _dot(ch["kg"], vn[i].astype(BF16))
            decay = jnp.where(top, ch["egl"][:, 0:1], ch["egl"][:, C:C + 1])
            st_ref[sidx] = S[i] * decay + jnp.where(bdmask, upd, 0.0)

    bounds = sorted({0, G} | {G - (G * i) // DN_SCAN_GROUPS for i in range(1, DN_SCAN_GROUPS)})
    groups = [range(a, b) for a, b in zip(bounds[:-1], bounds[1:])]
    pending = iter(())
    for steps in groups:
        needed = [(0, p, c) for p in range(P) for c in steps]
        needed += [(1, p, G - 1 - c) for p in range(P) for c in steps]
        for _ in prepare(needed):
            step = next(pending, None)
            if step is not None:
                scan_step(step)
        for step in pending:
            scan_step(step)
        pending = iter(steps)
    for step in pending:
        scan_step(step)


def _dn_scan(q, k, v, bg, gt, *, G=16):
    B, S, _ = q.shape
    G = min(G, S // DN_CHUNK)
    R = G * DN_CHUNK
    assert S % R == 0
    nb = S // R
    n_ba = 4 * DN_HEADS
    P = DN_HEADS // 2
    fw = lambda n: pl.BlockSpec((1, R, n), lambda b, i: (b, i, 0))
    bw = lambda n: pl.BlockSpec((1, R, n), lambda b, i: (b, nb - 1 - i, 0))
    fwt = pl.BlockSpec((1, n_ba, R), lambda b, i: (b, 0, i))
    bwt = pl.BlockSpec((1, n_ba, R), lambda b, i: (b, 0, nb - 1 - i))
    return pl.pallas_call(
        functools.partial(_dnscan_body, G=G),
        grid=(B, nb),
        in_specs=[fw(DN_WIDTH), fw(DN_WIDTH), fw(DN_WIDTH), fw(BA_PAD), fwt,
                  bw(DN_WIDTH), bw(DN_WIDTH), bw(DN_WIDTH), bw(BA_PAD), bwt],
        out_specs=[fw(DN_WIDTH), bw(DN_WIDTH)],
        out_shape=[jax.ShapeDtypeStruct((B, S, DN_WIDTH), F32)] * 2,
        scratch_shapes=[pltpu.VMEM((2 * P, LANES, LANES), F32)],
        compiler_params=pltpu.CompilerParams(dimension_semantics=("parallel", "arbitrary")),
        name="dn_scan",
    )(q, k, v, bg, gt, q, k, v, bg, gt)


def _mixers(x, ln_mix, w_mix_in, conv_w, sink, diff_lam, diff_g, a_log, dt_bias, lam_init):
    B, S, _ = x.shape
    wq, wk, wv, dq, dkt, dv, dn_raw, z, ba, bat, q_sq, k_sq = _inproj(x, ln_mix, w_mix_in)
    win_slopes = 2.0 ** (-8.0 * jnp.arange(1, WIN_Q_HEADS + 1, dtype=F32) / WIN_Q_HEADS)
    o_win = _win_attention(wq, wk, wv, sink, win_slopes)
    o_diff = _diff_attention(dq, dkt, dv, q_sq, k_sq, diff_lam, diff_g, lam_init)
    q, k, v, bg, gt = _dn_prep(dn_raw, ba, bat, conv_w, a_log, dt_bias)
    o_f, o_b = _dn_scan(q, k, v, bg, gt)
    return tuple(t.reshape(B * S, t.shape[-1]) for t in (o_win, o_diff, o_f, o_b, z))


def kernel(x, ln_ffn1, ffn1_w_in, ffn1_w_out, ln_mix, w_mix_in, conv_w, sink_logits, diff_lambda,
           diff_norm_g, dn_A_log, dn_dt_bias, dn_norm_g, w_mix_out, ln_ffn2, ffn2_w_in, ffn2_w_out,
           ln_final):
    B, S, D = x.shape
    depth = ln_ffn1.shape[0]
    for l in range(depth):
        lam_init = 0.8 - 0.6 * math.exp(-0.3 * l)
        x = _ffn(x.reshape(B * S, D), ln_ffn1[l], ffn1_w_in[l], ffn1_w_out[l]).reshape(B, S, D)
        mixed = _mixers(x, ln_mix[l], w_mix_in[l], conv_w[l], sink_logits[l], diff_lambda[l], diff_norm_g[l],
                        dn_A_log[l], dn_dt_bias[l], lam_init)
        g_final = ln_final if l == depth - 1 else None
        x = _ffn(x.reshape(B * S, D), ln_ffn2[l], ffn2_w_in[l], ffn2_w_out[l], g_final,
                 mix=mixed + (dn_norm_g[l], w_mix_out[l])).reshape(B, S, D)
    return x
```

```python
import functools
import math

import jax
import jax.numpy as jnp
from jax import lax
from jax.experimental import pallas as pl
from jax.experimental.pallas import tpu as pltpu

F32 = jnp.float32
BF16 = jnp.bfloat16

D_MODEL = 1024
HEAD_DIM = 64
EPS = 1e-6
WIN_Q_HEADS = 6
WIN_KV_HEADS = 2
WINDOW = 128
DIFF_HEADS = 4
DIFF_QK_DIM = HEAD_DIM // 2
DN_HEADS = 6
DN_CHUNK = 64
CONV_W = 5
D_FF = 2752
WIN_WIDTH = WIN_Q_HEADS * HEAD_DIM
WIN_KV_WIDTH = WIN_KV_HEADS * HEAD_DIM
DIFF_WIDTH = DIFF_HEADS * HEAD_DIM
DN_WIDTH = DN_HEADS * HEAD_DIM
DN_QKV = 3 * DN_WIDTH
MIX_WIDTH = WIN_WIDTH + DIFF_WIDTH + DN_WIDTH
OFF_WQ = 0
OFF_WK = OFF_WQ + WIN_WIDTH
OFF_WV = OFF_WK + WIN_KV_WIDTH
OFF_DQ = OFF_WV + WIN_KV_WIDTH
OFF_DK = OFF_DQ + DIFF_WIDTH
OFF_DV = OFF_DK + DIFF_WIDTH
OFF_DN = OFF_DV + DIFF_WIDTH
OFF_Z = OFF_DN + DN_QKV
OFF_BA = OFF_Z + DN_WIDTH
MIX_IN = OFF_BA + 4 * DN_HEADS

LANES = 128
SUBLANES = 8
BF16_ROWS = 2 * SUBLANES
FF_CHUNK = 256
D_FF_PAD = -(-D_FF // FF_CHUNK) * FF_CHUNK
VMEM_LIMIT = 56 * 1024 * 1024
LOG2E = math.log2(math.e)
DIFF_TK = 512
DIFF_BIAS_ROWS = 3
DIFF_KROWS = -(-(DIFF_QK_DIM + DIFF_BIAS_ROWS) // BF16_ROWS) * BF16_ROWS
DIFF_KT_ROWS = 2 * DIFF_HEADS * DIFF_KROWS
DIFF_V_WIDTH = DIFF_HEADS * LANES
DIFF_SKIP_LOG2 = 150.0
DIFF_BOUND_SLACK = 1.0 + 2.0 ** -6
MASKED = -1e30
DIFF_SLOPES = tuple(2.0 ** (-8.0 * (i + 1) / DIFF_HEADS) for i in range(DIFF_HEADS))
DN_SCAN_GROUPS = 5


def _rms(x, g):
    return x * lax.rsqrt(jnp.mean(x * x, axis=-1, keepdims=True) + EPS) * g


def _silu(x):
    h = 0.5 * x
    return h + h * jnp.tanh(h)


def _seg_sum64(x):
    R, width = x.shape
    n = width // LANES
    rows = jnp.concatenate([x[:, c * LANES:(c + 1) * LANES] for c in range(n)], axis=0)
    r = lax.broadcasted_iota(jnp.int32, (LANES, LANES), 0) < HEAD_DIM
    c = lax.broadcasted_iota(jnp.int32, (LANES, LANES), 1) < HEAD_DIM
    ones_bd = (r == c).astype(F32).astype(BF16)
    hi = rows.astype(BF16)
    r1 = rows - hi.astype(F32)
    mid = r1.astype(BF16)
    lo = (r1 - mid.astype(F32)).astype(BF16)
    ss = _dot(hi, ones_bd) + _dot(mid, ones_bd) + _dot(lo, ones_bd)
    return jnp.concatenate([ss[c * R:(c + 1) * R] for c in range(n)], axis=1)


def _dot(a, b):
    return jnp.dot(a, b, preferred_element_type=F32)


def _dot_nt(a, b):
    return lax.dot_general(a, b, (((1,), (1,)), ((), ())), preferred_element_type=F32)


def _resident(shape):
    nd = len(shape)
    return pl.BlockSpec(shape, lambda *_: (0,) * nd, pipeline_mode=pl.Buffered(1))


def _ffn_body(*refs, n_chunks, final, mixed):
    refs = iter(refs)
    x_ref = next(refs)
    mix_refs = [next(refs) for _ in range(7)] if mixed else None
    g_ref, wg_ref, wu_ref, wo_ref = (next(refs) for _ in range(4))
    gf_ref = next(refs) if final else None
    o_ref, acc_ref = next(refs), next(refs)
    x = x_ref[...]
    if mixed:
        ow_ref, od_ref, of_ref, ob_ref, z_ref, dng_ref, wmix_ref = mix_refs
        cat_ref = next(refs)
        oc = of_ref[...] + ob_ref[...]
        on = oc * lax.rsqrt(_seg_sum64(oc * oc) * (1.0 / HEAD_DIM) + EPS) * dng_ref[...] * _silu(z_ref[...])
        cat_ref[:, :WIN_WIDTH] = ow_ref[...]
        cat_ref[:, WIN_WIDTH:WIN_WIDTH + DIFF_WIDTH] = od_ref[...]
        cat_ref[:, WIN_WIDTH + DIFF_WIDTH:] = on.astype(BF16)
        x = x + _dot(cat_ref[...], wmix_ref[...])
    h = _rms(x, g_ref[...]).astype(BF16)
    for c in range(n_chunks):
        cols = slice(c * FF_CHUNK, (c + 1) * FF_CHUNK)
        a = (_silu(_dot(h, wg_ref[:, cols])) * _dot(h, wu_ref[:, cols])).astype(BF16)
        part = _dot(a, wo_ref[cols, :])
        if c == 0:
            acc_ref[...] = part
        else:
            acc_ref[...] += part
    y = x + 0.5 * acc_ref[...]
    if final:
        y = _rms(y, gf_ref[...])
    o_ref[...] = y


def _prep_ffn_weights(w_in, w_out):
    pad = D_FF_PAD - D_FF
    wg = jnp.pad(w_in[:, :D_FF].astype(BF16), ((0, 0), (0, pad)))
    wu = jnp.pad(w_in[:, D_FF:].astype(BF16), ((0, 0), (0, pad)))
    wo = jnp.pad(w_out.astype(BF16), ((0, pad), (0, 0)))
    return wg, wu, wo


def _ffn(x2d, g, w_in, w_out, g_final=None, mix=None, *, tm=512):
    T = x2d.shape[0]
    tm = min(tm, T)
    assert T % tm == 0
    n_chunks = D_FF_PAD // FF_CHUNK
    wg, wu, wo = _prep_ffn_weights(w_in, w_out)
    final = g_final is not None
    mixed = mix is not None
    row = lambda n: pl.BlockSpec((tm, n), lambda i: (i, 0))
    in_specs = [row(D_MODEL)]
    args = [x2d]
    scratch = [pltpu.VMEM((tm, D_MODEL), F32)]
    if mixed:
        o_win, o_diff, o_f, o_b, z, dn_g, w_mix_out = mix
        in_specs += [row(WIN_WIDTH), row(DIFF_WIDTH), row(DN_WIDTH), row(DN_WIDTH), row(DN_WIDTH),
                     _resident((1, DN_WIDTH)), _resident((MIX_WIDTH, D_MODEL))]
        args += [o_win, o_diff, o_f, o_b, z, jnp.tile(dn_g.astype(F32), DN_HEADS).reshape(1, DN_WIDTH),
                 w_mix_out.astype(BF16)]
        scratch.append(pltpu.VMEM((tm, MIX_WIDTH), BF16))
    in_specs += [_resident((1, D_MODEL)), _resident(wg.shape), _resident(wu.shape), _resident(wo.shape)]
    args += [g.reshape(1, D_MODEL), wg, wu, wo]
    if final:
        in_specs.append(_resident((1, D_MODEL)))
        args.append(g_final.reshape(1, D_MODEL))
    return pl.pallas_call(
        functools.partial(_ffn_body, n_chunks=n_chunks, final=final, mixed=mixed),
        grid=(T // tm,),
        in_specs=in_specs,
        out_specs=row(D_MODEL),
        out_shape=jax.ShapeDtypeStruct((T, D_MODEL), F32),
        scratch_shapes=scratch,
        compiler_params=pltpu.CompilerParams(
            dimension_semantics=("parallel",), vmem_limit_bytes=VMEM_LIMIT),
        name="ffn" + ("_mix" if mixed else "") + ("_final" if final else ""),
    )(*args)


BA_PAD = LANES


def _inproj_body(x_ref, g_ref, w_ref, wt_ref, wdv_ref,
                 wq_ref, wk_ref, wv_ref, dq_ref, dkt_ref, dv_ref, dn_ref, z_ref, ba_ref, bat_ref,
                 qn_ref, kn_ref):
    h = _rms(x_ref[0], g_ref[...]).astype(BF16)
    wq_ref[0] = (_dot(h, w_ref[:, OFF_WQ:OFF_WK]) * (HEAD_DIM ** -0.5 * LOG2E)).astype(BF16)
    wk_ref[0] = _dot(h, w_ref[:, OFF_WK:OFF_WV]).astype(BF16)
    wv_ref[0] = _dot(h, w_ref[:, OFF_WV:OFF_DQ]).astype(BF16)
    dq = (_dot(h, w_ref[:, OFF_DQ:OFF_DK]) * (DIFF_QK_DIM ** -0.5 * LOG2E)).astype(BF16)
    dq_ref[0] = dq
    n_tiles = dq.shape[0] // DIFF_TK
    dqf = dq.astype(F32)
    group = (lax.broadcasted_iota(jnp.int32, (DIFF_WIDTH, LANES), 0) // DIFF_QK_DIM
             == lax.broadcasted_iota(jnp.int32, (DIFF_WIDTH, LANES), 1))
    q_sq = _dot((dqf * dqf).astype(BF16), group.astype(F32).astype(BF16))
    for a in range(n_tiles):
        qn_ref[0, a] = jnp.broadcast_to(
            jnp.max(q_sq[a * DIFF_TK:(a + 1) * DIFF_TK], axis=0, keepdims=True), (SUBLANES, LANES))
    wt_out = lax.dot_general(wt_ref[...], h, (((0,), (1,)), ((), ())),
                             preferred_element_type=F32)
    kt = wt_out[:DIFF_WIDTH]
    tm = kt.shape[1]
    extra_rows = DIFF_KROWS - DIFF_QK_DIM
    pos = pl.program_id(1) * tm + lax.broadcasted_iota(jnp.int32, (extra_rows, tm), 1)
    kr = (pos % DIFF_TK).astype(F32)
    rowi = lax.broadcasted_iota(jnp.int32, (extra_rows, tm), 0)
    k_sq = []
    for hd in range(DIFF_HEADS):
        bias = (DIFF_SLOPES[hd] * LOG2E) * kr
        hi = bias.astype(BF16).astype(F32)
        mid = (bias - hi).astype(BF16).astype(F32)
        lo = bias - hi - mid
        extra = jnp.where(rowi == 0, hi, jnp.where(rowi == 1, mid, jnp.where(rowi == 2, lo, 0.0))).astype(BF16)
        for mp in range(2):
            hm = 2 * hd + mp
            kb = kt[hm * DIFF_QK_DIM:(hm + 1) * DIFF_QK_DIM, :].astype(BF16)
            dkt_ref[0, hm * DIFF_KROWS:hm * DIFF_KROWS + DIFF_QK_DIM, :] = kb
            dkt_ref[0, hm * DIFF_KROWS + DIFF_QK_DIM:(hm + 1) * DIFF_KROWS, :] = extra
            kbf = kb.astype(F32)
            k_sq.append(jnp.sum(kbf * kbf, axis=0, keepdims=True))
    k_sq = jnp.concatenate(k_sq, axis=0)
    for a in range(n_tiles):
        kn_ref[0, a] = jnp.broadcast_to(
            jnp.max(k_sq[:, a * DIFF_TK:(a + 1) * DIFF_TK], axis=1, keepdims=True), (SUBLANES, LANES))
    dv = _dot(h, wdv_ref[...])
    ones_half = (lax.broadcasted_iota(jnp.int32, dv.shape, 1) % LANES) >= HEAD_DIM
    dv_ref[0] = jnp.where(ones_half, 1.0, dv).astype(BF16)
    dn_ref[0] = _dot(h, w_ref[:, OFF_DN:OFF_Z])
    z_ref[0] = _dot(h, w_ref[:, OFF_Z:OFF_BA])
    ba_ref[0] = _dot(h, w_ref[:, OFF_BA:OFF_BA + BA_PAD])
    bat_ref[0] = wt_out[DIFF_WIDTH:DIFF_WIDTH + bat_ref.shape[1]]


def _inproj(x, g, w_in, *, tm=1024):
    B, S, _ = x.shape
    tm = min(tm, S)
    assert S % tm == 0 and tm % DIFF_TK == 0
    n_ba = 4 * DN_HEADS
    w_in = w_in.astype(BF16)
    w = jnp.pad(w_in, ((0, 0), (0, OFF_BA + BA_PAD - MIX_IN)))
    wt = jnp.pad(jnp.concatenate([w_in[:, OFF_DK:OFF_DV], w_in[:, OFF_BA:MIX_IN]], axis=1),
                 ((0, 0), (0, 2 * BF16_ROWS - n_ba % (2 * BF16_ROWS))))
    wdv = jnp.pad(w_in[:, OFF_DV:OFF_DN].reshape(D_MODEL, DIFF_HEADS, HEAD_DIM),
                  ((0, 0), (0, 0), (0, LANES - HEAD_DIM))).reshape(D_MODEL, DIFF_V_WIDTH)
    row = lambda n: pl.BlockSpec((1, tm, n), lambda b, i: (b, i, 0))
    col = lambda n: pl.BlockSpec((1, n, tm), lambda b, i: (b, 0, i))
    out_shapes = [
        jax.ShapeDtypeStruct((B, S, WIN_WIDTH), BF16),
        jax.ShapeDtypeStruct((B, S, WIN_KV_WIDTH), BF16),
        jax.ShapeDtypeStruct((B, S, WIN_KV_WIDTH), BF16),
        jax.ShapeDtypeStruct((B, S, DIFF_WIDTH), BF16),
        jax.ShapeDtypeStruct((B, DIFF_KT_ROWS, S), BF16),
        jax.ShapeDtypeStruct((B, S, DIFF_V_WIDTH), BF16),
        jax.ShapeDtypeStruct((B, S, DN_QKV), F32),
        jax.ShapeDtypeStruct((B, S, DN_WIDTH), F32),
        jax.ShapeDtypeStruct((B, S, BA_PAD), F32),
        jax.ShapeDtypeStruct((B, n_ba, S), F32),
        jax.ShapeDtypeStruct((B, S // DIFF_TK, SUBLANES, LANES), F32),
        jax.ShapeDtypeStruct((B, S // DIFF_TK, SUBLANES, LANES), F32),
    ]
    norms = pl.BlockSpec((1, tm // DIFF_TK, SUBLANES, LANES), lambda b, i: (b, i, 0, 0))
    out_specs = [row(WIN_WIDTH), row(WIN_KV_WIDTH), row(WIN_KV_WIDTH), row(DIFF_WIDTH),
                 col(DIFF_KT_ROWS), row(DIFF_V_WIDTH), row(DN_QKV), row(DN_WIDTH), row(BA_PAD), col(n_ba),
                 norms, norms]
    return pl.pallas_call(
        _inproj_body,
        grid=(B, S // tm),
        in_specs=[row(D_MODEL), _resident((1, D_MODEL)), _resident(w.shape),
                  _resident(wt.shape), _resident(wdv.shape)],
        out_specs=out_specs,
        out_shape=out_shapes,
        compiler_params=pltpu.CompilerParams(
            dimension_semantics=("parallel", "parallel"), vmem_limit_bytes=VMEM_LIMIT),
        name="inproj",
    )(x, g.reshape(1, D_MODEL), w, wt, wdv)


def _win_body(q_ref, kp_ref, kc_ref, kn_ref, vp_ref, vc_ref, vn_ref, sink_ref, slope_ref, o_ref,
              *, n_steps, nq):
    i = pl.program_id(1)
    W = WINDOW
    KW = 3 * W
    kw = jnp.concatenate([kp_ref[0], kc_ref[0], kn_ref[0]], axis=0).astype(F32)
    vw = jnp.concatenate([vp_ref[0], vc_ref[0], vn_ref[0]], axis=0).astype(F32)
    kw_sw = pltpu.roll(kw, HEAD_DIM, 1)
    vw_sw = pltpu.roll(vw, HEAD_DIM, 1)
    left = lax.broadcasted_iota(jnp.int32, (KW, LANES), 1) < HEAD_DIM
    qi = lax.broadcasted_iota(jnp.int32, (W, KW), 0)
    ki = lax.broadcasted_iota(jnp.int32, (W, KW), 1)
    dist = jnp.abs(ki - W - qi)
    band_bias = [jnp.where(dist <= W, (-LOG2E) * slope_ref[:, hq:hq + 1] * dist.astype(F32), MASKED)
                 for hq in range(WIN_Q_HEADS)]
    edge_lo = jnp.where(jnp.logical_or(ki >= W, i > 0), 0.0, MASKED)
    edge_hi = jnp.where(jnp.logical_or(ki < 2 * W, i < n_steps - 1), 0.0, MASKED)
    n_pairs = WIN_Q_HEADS // 2
    ones_stack = jnp.concatenate([jnp.where(left, 1.0, 0.0), jnp.where(left, 0.0, 1.0)], axis=0).astype(BF16)
    blocks = []
    for a in range(nq):
        rows = slice(a * W, a * W + KW)

        def stack(first, second, rows=rows):
            return jnp.concatenate([jnp.where(left, first[rows], 0.0),
                                    jnp.where(left, 0.0, second[rows])], axis=0).astype(BF16)

        zk = (stack(kw, kw_sw), stack(kw, kw), stack(kw_sw, kw))
        zv = tuple(jnp.concatenate([z, ones_stack], axis=1)
                   for z in (stack(vw, vw_sw), stack(vw, vw), stack(vw_sw, vw)))
        edge = None
        if a == 0:
            edge = edge_lo
        if a == nq - 1:
            edge = edge_hi if edge is None else edge + edge_hi
        q = q_ref[0, a * W:(a + 1) * W, :]
        s2 = [_dot_nt(q[:, p * LANES:(p + 1) * LANES], zk[p]) for p in range(n_pairs)]
        blocks.append((a, edge, zv, s2))
    weights, sink_terms = {}, {}
    for a, edge, zv, s2 in blocks:
        for hq in range(WIN_Q_HEADS):
            p, t = divmod(hq, 2)
            s = s2[p][:, t * KW:(t + 1) * KW] + band_bias[hq]
            if edge is not None:
                s = s + edge
            sink = LOG2E * sink_ref[:, hq:hq + 1]
            m = jnp.maximum(jnp.max(s, axis=-1, keepdims=True), sink)
            weights[(a, hq)] = jnp.exp2(s - m).astype(BF16)
            sink_terms[(a, hq)] = jnp.exp2(sink - m)
    left_w = lax.broadcasted_iota(jnp.int32, (W, LANES), 1) < HEAD_DIM
    for a, edge, zv, s2 in blocks:
        for p in range(n_pairs):
            ee = jnp.concatenate([weights[(a, 2 * p)], weights[(a, 2 * p + 1)]], axis=1)
            acc = _dot(ee, zv[p])
            denom = acc[:, LANES:] + jnp.where(left_w, sink_terms[(a, 2 * p)], sink_terms[(a, 2 * p + 1)])
            o_ref[0, a * W:(a + 1) * W, p * LANES:(p + 1) * LANES] = (acc[:, :LANES] / denom).astype(BF16)


def _win_attention(q, k, v, sink, slopes, *, nq=8):
    B, S, _ = q.shape
    W = WINDOW
    nq = min(nq, S // W)
    assert S % (nq * W) == 0
    ns = S // (nq * W)
    nb = S // W
    qspec = pl.BlockSpec((1, nq * W, WIN_WIDTH), lambda b, i: (b, i, 0))
    prev = pl.BlockSpec((1, W, WIN_KV_WIDTH), lambda b, i: (b, jnp.maximum(i * nq - 1, 0), 0))
    cur = pl.BlockSpec((1, nq * W, WIN_KV_WIDTH), lambda b, i: (b, i, 0))
    nxt = pl.BlockSpec((1, W, WIN_KV_WIDTH), lambda b, i: (b, jnp.minimum((i + 1) * nq, nb - 1), 0))
    return pl.pallas_call(
        functools.partial(_win_body, n_steps=ns, nq=nq),
        grid=(B, ns),
        in_specs=[qspec, prev, cur, nxt, prev, cur, nxt,
                  _resident((1, WIN_Q_HEADS)), _resident((1, WIN_Q_HEADS))],
        out_specs=pl.BlockSpec((1, nq * W, WIN_WIDTH), lambda b, i: (b, i, 0)),
        out_shape=jax.ShapeDtypeStruct((B, S, WIN_WIDTH), BF16),
        compiler_params=pltpu.CompilerParams(dimension_semantics=("parallel", "parallel")),
        name="win_attn",
    )(q, k, k, k, v, v, v, sink.reshape(1, WIN_Q_HEADS).astype(F32), slopes.reshape(1, WIN_Q_HEADS))


def _diff_ranges(q_sq, k_sq, tq, tk):
    assert tq == tk == DIFF_TK
    n_maps = 2 * DIFF_HEADS
    B, nq = q_sq.shape[:2]
    nk = nq
    qn = jnp.sqrt(q_sq[:, :, 0, :n_maps])
    kn = jnp.swapaxes(jnp.sqrt(k_sq[:, :, :n_maps, 0]), 1, 2)
    kown = kn
    reach = jnp.swapaxes(qn, 1, 2)[..., None] * (kn[:, :, None, :] + kown[..., None]) * DIFF_BOUND_SLACK
    reach = jnp.max(reach.reshape(B, DIFF_HEADS, 2, nq, nk), axis=2)
    q0 = jnp.arange(nq, dtype=jnp.int32) * tq
    k0 = jnp.arange(nk, dtype=jnp.int32) * tk
    dmin = jnp.maximum(jnp.maximum(k0[None, :] - (q0[:, None] + tq - 1), q0[:, None] - (k0[None, :] + tk - 1)), 0)
    c = jnp.asarray(DIFF_SLOPES, F32) * LOG2E
    need = reach - c[None, :, None, None] * dmin.astype(F32)[None, None] > -DIFF_SKIP_LOG2
    idx = jnp.arange(nk, dtype=jnp.int32)
    need = jnp.logical_or(need, (idx[None, :] == (q0 // tk)[:, None])[None, None])
    lo = jnp.min(jnp.where(need, idx, nk), axis=-1)
    hi = jnp.max(jnp.where(need, idx, -1), axis=-1)
    return jnp.stack([lo, hi], axis=-1).transpose(0, 2, 1, 3).reshape(-1).astype(jnp.int32)


def _diff_body(rng_ref, q_ref, kt_ref, v_ref, lam_ref, g_ref, o_ref, qv_ref, m_ref, acc_ref,
               s0_ref, st0_ref, s1_ref, st1_ref, *, tq, tk, lam_init):
    n_maps = 2 * DIFF_HEADS
    n_groups = tk // LANES
    q0 = pl.program_id(1) * tq
    jd = q0 // tk
    q = q_ref[0]
    lane_x = lax.broadcasted_iota(jnp.int32, (tq, DIFF_KROWS - DIFF_QK_DIM), 1)
    for var, sign in enumerate((1.0, -1.0, 0.0)):
        extras = jnp.where(lane_x < DIFF_BIAS_ROWS, -sign, 0.0).astype(BF16)
        for hm in range(n_maps):
            qv_ref[var, hm, :, 0:DIFF_QK_DIM] = q[:, hm * DIFF_QK_DIM:(hm + 1) * DIFF_QK_DIM]
            qv_ref[var, hm, :, DIFF_QK_DIM:DIFF_KROWS] = extras
    m_ref[...] = jnp.full(m_ref.shape, -jnp.inf, F32)
    acc_ref[...] = jnp.zeros(acc_ref.shape, F32)
    qpos = (q0 + lax.broadcasted_iota(jnp.int32, (tq, LANES), 0)).astype(F32)
    bufs = ((s0_ref, st0_ref), (s1_ref, st1_ref))
    rng_base = (pl.program_id(0) * pl.num_programs(1) + pl.program_id(1)) * (2 * DIFF_HEADS)

    los = [rng_ref[rng_base + 2 * h] for h in range(DIFF_HEADS)]
    n_offs = [rng_ref[rng_base + 2 * h + 1] - los[h] for h in range(DIFF_HEADS)]
    starts = [0]
    for h in range(DIFF_HEADS):
        starts.append(starts[-1] + n_offs[h])
    n_total = starts[DIFF_HEADS]

    def pick(h, vals):
        out = vals[-1]
        for i in range(len(vals) - 2, -1, -1):
            out = jnp.where(h == i, vals[i], out)
        return out

    def locate(u):
        h = sum((u >= starts[i]).astype(jnp.int32) for i in range(1, DIFF_HEADS))
        t = pick(h, los) + u - pick(h, starts[:DIFF_HEADS])
        return h, t + (t >= jd).astype(jnp.int32)

    def scores(h, j, buf, diag_dist=None):
        s_ref, st_ref = bufs[buf]
        if diag_dist is not None:
            var = 2
            bias = (DIFF_SLOPES[h] * LOG2E) * diag_dist
        else:
            c = pick(h, [sl * LOG2E for sl in DIFF_SLOPES])
            after = j > jd
            var = jnp.where(after, 0, 1)
            rowoff = jnp.where(after, c, -c) * ((j * tk).astype(F32) - qpos)
        k0 = pl.multiple_of(j * tk, tk)
        for mp in range(2):
            hm = 2 * h + mp
            r0 = hm * DIFF_KROWS if diag_dist is not None else pl.multiple_of(hm * DIFF_KROWS, BF16_ROWS)
            kt = kt_ref[0, pl.ds(r0, DIFF_KROWS), pl.ds(k0, tk)]
            s = _dot(qv_ref[var, hm], kt)
            if diag_dist is not None:
                s = s - bias
            s_ref[mp] = s
            mx = s[:, 0:LANES]
            for g in range(1, n_groups):
                mx = jnp.maximum(mx, s[:, g * LANES:(g + 1) * LANES])
            mrow = jnp.broadcast_to(jnp.max(mx, axis=-1, keepdims=True), (tq, LANES))
            m_old = m_ref[hm]
            if diag_dist is not None:
                m_new = jnp.maximum(m_old, mrow)
                shift = m_new
            else:
                m_new = jnp.maximum(m_old, mrow - rowoff)
                shift = m_new + rowoff
            st_ref[mp, 0] = shift
            st_ref[mp, 1] = jnp.exp2(m_old - m_new)
            m_ref[hm] = m_new

    def values(h, j, buf, static=False):
        s_ref, st_ref = bufs[buf]
        k0 = pl.multiple_of(j * tk, tk)
        c0 = h * LANES if static else pl.multiple_of(h * LANES, LANES)
        v2 = v_ref[0, pl.ds(k0, tk), pl.ds(c0, LANES)]
        for mp in range(2):
            hm = 2 * h + mp
            shift = st_ref[mp, 0]
            alpha = st_ref[mp, 1]
            e = jnp.exp2((s_ref[mp] - jnp.concatenate([shift] * n_groups, axis=1)).astype(BF16))
            acc_ref[hm] = alpha * acc_ref[hm] + _dot(e, v2)

    kpos = (jd * tk + lax.broadcasted_iota(jnp.int32, (tq, tk), 1)).astype(F32)
    qp = (q0 + lax.broadcasted_iota(jnp.int32, (tq, tk), 0)).astype(F32)
    diag_dist = jnp.abs(qp - kpos)
    scores(0, jd, 0, diag_dist)
    for h in range(DIFF_HEADS):
        if h + 1 < DIFF_HEADS:
            scores(h + 1, jd, (h + 1) % 2, diag_dist)
        values(h, jd, h % 2, static=True)

    def s_at(u, buf):
        h, j = locate(u)
        scores(h, j, buf)

    def v_at(u, buf):
        h, j = locate(u)
        values(h, j, buf)

    @pl.when(n_total > 0)
    def _():
        s_at(0, 0)

    def pair(p, carry):
        u = 2 * p
        s_at(u + 1, 1)
        v_at(u, 0)
        s_at(u + 2, 0)
        v_at(u + 1, 1)
        return carry

    n_pairs = jnp.maximum(n_total - 1, 0) // 2
    lax.fori_loop(0, n_pairs, pair, 0)
    u_last = 2 * n_pairs
    left_over = n_total - u_last

    @pl.when(left_over == 2)
    def _():
        s_at(u_last + 1, 1)
        v_at(u_last, 0)
        v_at(u_last + 1, 1)

    @pl.when(left_over == 1)
    def _():
        v_at(u_last, 0)

    lp = lam_ref[...]
    lam = (jnp.exp(jnp.sum(lp[0:1] * lp[1:2], axis=-1, keepdims=True))
           - jnp.exp(jnp.sum(lp[2:3] * lp[3:4], axis=-1, keepdims=True)) + lam_init)
    lane = lax.broadcasted_iota(jnp.int32, (tq, LANES), 1)
    left = lane < HEAD_DIM
    ys = []
    for h in range(DIFF_HEADS):
        a1, a2 = acc_ref[2 * h], acc_ref[2 * h + 1]
        od = a1 / pltpu.roll(a1, HEAD_DIM, 1) - lam * (a2 / pltpu.roll(a2, HEAD_DIM, 1))
        ms = jnp.sum(jnp.where(left, od * od, 0.0), axis=-1, keepdims=True) * (1.0 / HEAD_DIM)
        ys.append(od * lax.rsqrt(ms + EPS) * g_ref[...] * (1.0 - lam_init))
    for p in range(DIFF_HEADS // 2):
        o_ref[0, :, p * LANES:(p + 1) * LANES] = jnp.where(
            left, ys[2 * p], pltpu.roll(ys[2 * p + 1], HEAD_DIM, 1)).astype(BF16)


def _diff_attention(q, kt, v, q_sq, k_sq, diff_lambda, diff_g, lam_init, *, tq=DIFF_TK):
    B, S, _ = q.shape
    tk = DIFF_TK
    tq = min(tq, S)
    assert S % tk == 0 and tk % tq == 0
    n_maps = 2 * DIFF_HEADS
    g2 = jnp.concatenate([diff_g, diff_g]).reshape(1, LANES).astype(F32)
    grid_spec = pltpu.PrefetchScalarGridSpec(
        num_scalar_prefetch=1,
        grid=(B, S // tq),
        in_specs=[pl.BlockSpec((1, tq, DIFF_WIDTH), lambda b, i, r: (b, i, 0)),
                  pl.BlockSpec((1, DIFF_KT_ROWS, S), lambda b, i, r: (b, 0, 0)),
                  pl.BlockSpec((1, S, DIFF_V_WIDTH), lambda b, i, r: (b, 0, 0)),
                  pl.BlockSpec((4, DIFF_QK_DIM), lambda b, i, r: (0, 0)),
                  pl.BlockSpec((1, LANES), lambda b, i, r: (0, 0))],
        out_specs=pl.BlockSpec((1, tq, DIFF_WIDTH), lambda b, i, r: (b, i, 0)),
        scratch_shapes=[pltpu.VMEM((3, n_maps, tq, DIFF_KROWS), BF16),
                        pltpu.VMEM((n_maps, tq, LANES), F32),
                        pltpu.VMEM((n_maps, tq, LANES), F32),
                        pltpu.VMEM((2, tq, tk), F32), pltpu.VMEM((2, 2, tq, LANES), F32),
                        pltpu.VMEM((2, tq, tk), F32), pltpu.VMEM((2, 2, tq, LANES), F32)])
    return pl.pallas_call(
        functools.partial(_diff_body, tq=tq, tk=tk, lam_init=lam_init),
        grid_spec=grid_spec,
        out_shape=jax.ShapeDtypeStruct((B, S, DIFF_WIDTH), BF16),
        compiler_params=pltpu.CompilerParams(
            dimension_semantics=("parallel", "arbitrary"), vmem_limit_bytes=VMEM_LIMIT),
        name="diff_attn",
    )(_diff_ranges(q_sq, k_sq, tq, tk), q, kt, v, diff_lambda.astype(F32), g2)


def _l2n64(x):
    return x * lax.rsqrt(_seg_sum64(x * x) + EPS)


def _dnprep_body(x_ref, xp_ref, xn_ref, w_ref, ba_ref, bat_ref, ab_ref, abt_ref,
                 q_ref, k_ref, v_ref, bg_ref, gt_ref, xx_ref, *, ts, n_tiles):
    i = pl.program_id(1)
    H = DN_HEADS
    halo = SUBLANES
    xx_ref[0:halo, :] = jnp.where(i > 0, xp_ref[0], 0.0)
    xx_ref[halo:halo + ts, :] = x_ref[0]
    xx_ref[halo + ts:2 * halo + ts, :] = jnp.where(i < n_tiles - 1, xn_ref[0], 0.0)
    half = CONV_W // 2
    y = w_ref[half:half + 1, :] * x_ref[0]
    for j in range(CONV_W):
        if j != half:
            y = y + w_ref[j:j + 1, :] * xx_ref[halo - half + j:halo - half + j + ts, :]
    y = _silu(y)
    q_ref[0] = _l2n64(y[:, :DN_WIDTH]) * (HEAD_DIM ** -0.5)
    k_ref[0] = _l2n64(y[:, DN_WIDTH:2 * DN_WIDTH])
    v_ref[0] = y[:, 2 * DN_WIDTH:]

    def gates(raw, a_log, dt_bias):
        beta = 1.0 / (1.0 + jnp.exp(-raw))
        z = raw + dt_bias
        softplus = jnp.maximum(z, 0.0) + jnp.log(1.0 + jnp.exp(-jnp.abs(z)))
        return beta, -jnp.exp(a_log) * softplus

    ba = ba_ref[0]
    beta, g = gates(ba, ab_ref[0:1, :], ab_ref[1:2, :])
    lane = lax.broadcasted_iota(jnp.int32, ba.shape, 1)
    r = lax.broadcasted_iota(jnp.int32, ba.shape, 0) % DN_CHUNK
    bwd = jnp.logical_and(lane >= 3 * H, lane < 4 * H)
    gc = g
    for d in (1, 2, 4, 8, 16, 32):
        f = jnp.where(r >= d, pltpu.roll(gc, d, 0), 0.0)
        b = jnp.where(r < DN_CHUNK - d, pltpu.roll(gc, ts - d, 0), 0.0)
        gc = gc + jnp.where(bwd, b, f)
    bg_ref[0] = jnp.where(lane < 2 * H, beta, gc)
    bat = bat_ref[0]
    _, gt = gates(bat, abt_ref[:, 0:1], abt_ref[:, 1:2])
    row = lax.broadcasted_iota(jnp.int32, bat.shape, 0)
    c = lax.broadcasted_iota(jnp.int32, bat.shape, 1) % DN_CHUNK
    bwd_t = row >= 3 * H
    for d in (1, 2, 4, 8, 16, 32):
        f = jnp.where(c >= d, pltpu.roll(gt, d, 1), 0.0)
        b = jnp.where(c < DN_CHUNK - d, pltpu.roll(gt, ts - d, 1), 0.0)
        gt = gt + jnp.where(bwd_t, b, f)
    gt_ref[0] = gt


def _dn_prep(dn_raw, ba, bat, conv_w, a_log, dt_bias, *, ts=256):
    B, S, C = dn_raw.shape
    ts = min(ts, S)
    assert S % ts == 0 and ts % DN_CHUNK == 0
    nt = S // ts
    n_ba = 4 * DN_HEADS
    hb = ts // SUBLANES
    zeros = jnp.zeros((2 * DN_HEADS,), F32)
    al = jnp.concatenate([zeros, a_log.reshape(-1).astype(F32)])
    db = jnp.concatenate([zeros, dt_bias.reshape(-1).astype(F32)])
    ab = jnp.pad(jnp.stack([al, db]), ((0, 0), (0, BA_PAD - n_ba)))
    abt = jnp.stack([al, db], axis=1)
    row = lambda n: pl.BlockSpec((1, ts, n), lambda b, i: (b, i, 0))
    return pl.pallas_call(
        functools.partial(_dnprep_body, ts=ts, n_tiles=nt),
        grid=(B, nt),
        in_specs=[row(C),
                  pl.BlockSpec((1, SUBLANES, C), lambda b, i: (b, jnp.maximum(i * hb - 1, 0), 0)),
                  pl.BlockSpec((1, SUBLANES, C),
                               lambda b, i: (b, jnp.minimum((i + 1) * hb, S // SUBLANES - 1), 0)),
                  _resident((CONV_W, C)), row(BA_PAD),
                  pl.BlockSpec((1, n_ba, ts), lambda b, i: (b, 0, i)),
                  _resident((2, BA_PAD)), _resident((n_ba, 2))],
        out_specs=[row(DN_WIDTH), row(DN_WIDTH), row(DN_WIDTH), row(BA_PAD),
                   pl.BlockSpec((1, n_ba, ts), lambda b, i: (b, 0, i))],
        out_shape=[jax.ShapeDtypeStruct((B, S, DN_WIDTH), F32)] * 3
        + [jax.ShapeDtypeStruct((B, S, BA_PAD), F32), jax.ShapeDtypeStruct((B, n_ba, S), F32)],
        scratch_shapes=[pltpu.VMEM((ts + 2 * SUBLANES, C), F32)],
        compiler_params=pltpu.CompilerParams(dimension_semantics=("parallel", "parallel")),
        name="dn_prep",
    )(dn_raw, dn_raw, dn_raw, conv_w.astype(F32), ba, bat, ab, abt)


def _bd_rows(y, left):
    return jnp.concatenate([jnp.where(left, y, 0.0), jnp.where(left, 0.0, y)], axis=0).astype(BF16)


def _dnscan_body(qf_ref, kf_ref, vf_ref, bgf_ref, gtf_ref, qb_ref, kb_ref, vb_ref, bgb_ref, gtb_ref,
                 of_ref, ob_ref, st_ref, *, G):
    C = DN_CHUNK
    H = DN_HEADS
    P = H // 2

    @pl.when(pl.program_id(1) == 0)
    def _():
        st_ref[...] = jnp.zeros(st_ref.shape, F32)

    lane1 = lax.broadcasted_iota(jnp.int32, (C, LANES), 1)
    row1 = lax.broadcasted_iota(jnp.int32, (C, LANES), 0)
    left1 = lane1 < C
    s_idx = lane1 % C
    eye = (row1 == s_idx).astype(F32)
    lane2 = lax.broadcasted_iota(jnp.int32, (C, 2 * LANES), 1)
    left2 = (lane2 % LANES) < C
    rr = lax.broadcasted_iota(jnp.int32, (LANES, LANES), 0)
    cc = lax.broadcasted_iota(jnp.int32, (LANES, LANES), 1)
    top = rr < C
    bdmask = (rr < C) == (cc < C)
    top_c = lax.broadcasted_iota(jnp.int32, (LANES, C), 0) < C

    dirs = ((0, qf_ref, kf_ref, vf_ref, bgf_ref, gtf_ref, of_ref),
            (1, qb_ref, kb_ref, vb_ref, bgb_ref, gtb_ref, ob_ref))
    kT_cache = {}
    by_key = {}

    def prepare(group):
        chains = []
        for d, p, c in group:
            _, q_ref, k_ref, v_ref, bg_ref, gt_ref, _ = dirs[d]
            tri = (row1 >= s_idx) if d == 0 else (row1 <= s_idx)
            strict = (row1 > s_idx) if d == 0 else (row1 < s_idx)
            last = C - 1 if d == 0 else 0
            sl = slice(p * LANES, (p + 1) * LANES)
            cb = d * H + 2 * p
            cg = 2 * H + d * H + 2 * p
            if (d, p) not in kT_cache:
                kT_cache[(d, p)] = k_ref[0, :, sl].T
            rs = slice(c * C, (c + 1) * C)
            qc, kc, vc = q_ref[0, rs, sl], k_ref[0, rs, sl], v_ref[0, rs, sl]
            bexp = jnp.where(left1, bg_ref[0, rs, cb:cb + 1], bg_ref[0, rs, cb + 1:cb + 2])
            gcc = jnp.where(left1, bg_ref[0, rs, cg:cg + 1], bg_ref[0, rs, cg + 1:cg + 2])
            g0 = gt_ref[0, cg:cg + 1, rs]
            g1 = gt_ref[0, cg + 1:cg + 2, rs]
            grow = jnp.concatenate([g0, g1], axis=1)
            glast = gcc[last:last + 1, :]
            gl0 = glast[:, 0:1]
            gl1 = glast[:, C:C + 1]
            eg = jnp.exp(gcc)
            kb_ = kc * bexp
            fac = jnp.exp(jnp.minimum(jnp.where(top_c, gl0 - g0, gl1 - g1), 0.0))
            chains.append(dict(
                key=(d, p, c),
                rhs=jnp.concatenate([vc * bexp, kb_ * eg], axis=1),
                qg=(qc * eg).astype(BF16),
                kq=jnp.concatenate([kb_, qc], axis=0).astype(BF16),
                Z=_bd_rows(kc, left1),
                dec=jnp.where(tri, jnp.exp(jnp.minimum(gcc - grow, 0.0)), 0.0),
                strict=strict,
                kg=kT_cache[(d, p)][:, rs] * fac,
                egl=jnp.exp(glast)))
        yield
        for ch in chains:
            kkqk = _dot_nt(ch["kq"], ch["Z"])
            ch["N"] = jnp.where(ch["strict"], -kkqk[:C] * ch["dec"], 0.0)
            ch["qk"] = (kkqk[C:] * ch["dec"]).astype(BF16)
        for ch in chains:
            ch["Pm"] = eye + ch["N"]
        for j in range(6):
            yield
            for ch in chains:
                nb = _bd_rows(ch["N"], left1)
                if j == 0:
                    ch["N"] = _dot(ch["N"].astype(BF16), nb)
                elif j < 5:
                    both = _dot(jnp.concatenate([ch["Pm"], ch["N"]], axis=0).astype(BF16), nb)
                    ch["Pm"] = ch["Pm"] + both[:C]
                    ch["N"] = both[C:]
                else:
                    ch["Pm"] = ch["Pm"] + _dot(ch["Pm"].astype(BF16), nb)
        yield
        for ch in chains:
            X = _dot(ch["Pm"].astype(BF16), _bd_rows(ch["rhs"], left2))
            by_key[ch["key"]] = dict(
                u=X[:, :LANES], egl=ch["egl"],
                qk_kg=jnp.concatenate(
                    [ch["qk"], jnp.concatenate([jnp.where(top_c, ch["kg"], 0.0),
                                                jnp.where(top_c, 0.0, ch["kg"])], axis=1).astype(BF16)], axis=0),
                wq=jnp.concatenate([X[:, LANES:].astype(BF16), ch["qg"]], axis=0))

    def scan_step(step):
        live = []
        for d, *_refs, o_ref in dirs:
            c = step if d == 0 else G - 1 - step
            for p in range(P):
                live.append((d * P + p, o_ref, slice(c * C, (c + 1) * C), p, by_key[(d, p, c)]))
        S = [st_ref[sidx] for sidx, *_ in live]
        t1 = [_dot(ch["wq"], S[i].astype(BF16)) for i, (*_, ch) in enumerate(live)]
        vn = [ch["u"] - t1[i][:C] for i, (*_, ch) in enumerate(live)]
        both = [_dot(ch["qk_kg"], _bd_rows(vn[i], left1)) for i, (*_, ch) in enumerate(live)]
        for i, (sidx, o_ref, rs, p, ch) in enumerate(live):
            o_ref[0, rs, p * LANES:(p + 1) * LANES] = t1[i][C:] + both[i][:C]
            decay = jnp.where(top, ch["egl"][:, 0:1], ch["egl"][:, C:C + 1])
            st_ref[sidx] = S[i] * decay + both[i][C:]

    bounds = sorted({0, G} | {G - (G * i) // DN_SCAN_GROUPS for i in range(1, DN_SCAN_GROUPS)})
    groups = [range(a, b) for a, b in zip(bounds[:-1], bounds[1:])]
    pending = iter(())
    for steps in groups:
        needed = [(0, p, c) for p in range(P) for c in steps]
        needed += [(1, p, G - 1 - c) for p in range(P) for c in steps]
        for _ in prepare(needed):
            step = next(pending, None)
            if step is not None:
                scan_step(step)
        for step in pending:
            scan_step(step)
        pending = iter(steps)
    for step in pending:
        scan_step(step)


def _dn_scan(q, k, v, bg, gt, *, G=16):
    B, S, _ = q.shape
    G = min(G, S // DN_CHUNK)
    R = G * DN_CHUNK
    assert S % R == 0
    nb = S // R
    n_ba = 4 * DN_HEADS
    P = DN_HEADS // 2
    fw = lambda n: pl.BlockSpec((1, R, n), lambda b, i: (b, i, 0))
    bw = lambda n: pl.BlockSpec((1, R, n), lambda b, i: (b, nb - 1 - i, 0))
    fwt = pl.BlockSpec((1, n_ba, R), lambda b, i: (b, 0, i))
    bwt = pl.BlockSpec((1, n_ba, R), lambda b, i: (b, 0, nb - 1 - i))
    return pl.pallas_call(
        functools.partial(_dnscan_body, G=G),
        grid=(B, nb),
        in_specs=[fw(DN_WIDTH), fw(DN_WIDTH), fw(DN_WIDTH), fw(BA_PAD), fwt,
                  bw(DN_WIDTH), bw(DN_WIDTH), bw(DN_WIDTH), bw(BA_PAD), bwt],
        out_specs=[fw(DN_WIDTH), bw(DN_WIDTH)],
        out_shape=[jax.ShapeDtypeStruct((B, S, DN_WIDTH), F32)] * 2,
        scratch_shapes=[pltpu.VMEM((2 * P, LANES, LANES), F32)],
        compiler_params=pltpu.CompilerParams(dimension_semantics=("parallel", "arbitrary")),
        name="dn_scan",
    )(q, k, v, bg, gt, q, k, v, bg, gt)


def _mixers(x, ln_mix, w_mix_in, conv_w, sink, diff_lam, diff_g, a_log, dt_bias, lam_init):
    B, S, _ = x.shape
    wq, wk, wv, dq, dkt, dv, dn_raw, z, ba, bat, q_sq, k_sq = _inproj(x, ln_mix, w_mix_in)
    win_slopes = 2.0 ** (-8.0 * jnp.arange(1, WIN_Q_HEADS + 1, dtype=F32) / WIN_Q_HEADS)
    o_win = _win_attention(wq, wk, wv, sink, win_slopes)
    o_diff = _diff_attention(dq, dkt, dv, q_sq, k_sq, diff_lam, diff_g, lam_init)
    q, k, v, bg, gt = _dn_prep(dn_raw, ba, bat, conv_w, a_log, dt_bias)
    o_f, o_b = _dn_scan(q, k, v, bg, gt)
    return tuple(t.reshape(B * S, t.shape[-1]) for t in (o_win, o_diff, o_f, o_b, z))


def kernel(x, ln_ffn1, ffn1_w_in, ffn1_w_out, ln_mix, w_mix_in, conv_w, sink_logits, diff_lambda,
           diff_norm_g, dn_A_log, dn_dt_bias, dn_norm_g, w_mix_out, ln_ffn2, ffn2_w_in, ffn2_w_out,
           ln_final):
    B, S, D = x.shape
    depth = ln_ffn1.shape[0]
    for l in range(depth):
        lam_init = 0.8 - 0.6 * math.exp(-0.3 * l)
        x = _ffn(x.reshape(B * S, D), ln_ffn1[l], ffn1_w_in[l], ffn1_w_out[l]).reshape(B, S, D)
        mixed = _mixers(x, ln_mix[l], w_mix_in[l], conv_w[l], sink_logits[l], diff_lambda[l], diff_norm_g[l],
                        dn_A_log[l], dn_dt_bias[l], lam_init)
        g_final = ln_final if l == depth - 1 else None
        x = _ffn(x.reshape(B * S, D), ln_ffn2[l], ffn2_w_in[l], ffn2_w_out[l], g_final,
                 mix=mixed + (dn_norm_g[l], w_mix_out[l])).reshape(B, S, D)
    return x
```
